```python
import math
import jax, jax.numpy as jnp
from jax import lax
import numpy as np

D_MODEL = 1024
BATCH = 8
SEQ = 4096
DEPTH = 2

GRID_W = 64
CTX_LEN = 256
HEAD_DIM = 64
GROUP_WIDTH = D_MODEL // 4
D_MIX = 4 * GROUP_WIDTH
BLOCK = 128
WINDOW = 128
A_HEADS = GROUP_WIDTH // HEAD_DIM
A_KV = A_HEADS // 2
B_HEADS = GROUP_WIDTH // HEAD_DIM
B_KV = B_HEADS // 2
C_HEADS = GROUP_WIDTH // HEAD_DIM
C_DK = HEAD_DIM
C_DV = HEAD_DIM
C_CHUNK = 128
M_HEADS = GROUP_WIDTH // HEAD_DIM
M_HEADDIM = HEAD_DIM
M_GROUPS = 2
M_DSTATE = 128
M_CONV = 3
M_CHUNK = 128
M_WIDTH = M_HEADS * M_HEADDIM
CONV_CH = M_WIDTH + 2 * M_GROUPS * M_DSTATE
D_FF = 4 * D_MODEL
ROPE_BASE = 10000.0
EPS = 1e-6
SPLIT_SIZES = (
    A_HEADS * HEAD_DIM, A_KV * HEAD_DIM, A_KV * HEAD_DIM,
    B_HEADS * HEAD_DIM, B_KV * HEAD_DIM, B_KV * HEAD_DIM,
    C_HEADS * C_DK, C_HEADS * C_DK, C_HEADS * C_DV, C_HEADS * C_DV,
    4 * C_HEADS,
    M_WIDTH, M_WIDTH, M_GROUPS * M_DSTATE, M_GROUPS * M_DSTATE,
    2 * M_HEADS,
)
SPLIT_POINTS = tuple(int(s) for s in np.cumsum(SPLIT_SIZES)[:-1])
N_IN = int(sum(SPLIT_SIZES))

kernel_name = 'hybrid_parallel_head_groups_dit'


def rmsnorm(x, g):
    xf = x.astype(jnp.float32)
    y = xf * lax.rsqrt(jnp.mean(xf * xf, axis=-1, keepdims=True) + EPS)
    return (y * g.astype(jnp.float32)).astype(x.dtype)


def axial_rope_tables(rows, dtype):
    row = jnp.repeat(jnp.arange(rows, dtype=jnp.float32), GRID_W)
    col = jnp.tile(jnp.arange(GRID_W, dtype=jnp.float32), rows)
    half = HEAD_DIM // 2
    inv_freq = ROPE_BASE ** (-jnp.arange(0, half, 2, dtype=jnp.float32) / half)
    ang = jnp.concatenate([row[:, None] * inv_freq, col[:, None] * inv_freq], axis=-1)
    return jnp.cos(ang).astype(dtype), jnp.sin(ang).astype(dtype)


def apply_axial_rope(x, cos, sin):
    half = HEAD_DIM // 2
    quarter = half // 2
    parts = []
    for axis in range(2):
        xa = x[..., axis * half:(axis + 1) * half]
        ca = cos[:, None, axis * quarter:(axis + 1) * quarter]
        sa = sin[:, None, axis * quarter:(axis + 1) * quarter]
        x1, x2 = xa[..., :quarter], xa[..., quarter:]
        parts += [x1 * ca - x2 * sa, x2 * ca + x1 * sa]
    return jnp.concatenate(parts, axis=-1)


def attend(qblk, keys, vals, mask=None, sink=None):
    s = jnp.einsum('bqgrd,bkgd->bgrqk', qblk, keys).astype(jnp.float32)
    if mask is not None:
        s = jnp.where(mask, s, -jnp.inf)
    if sink is not None:
        s_sink = jnp.broadcast_to(sink.astype(jnp.float32)[None, :, :, None, None], s.shape[:-1] + (1,))
        p = jax.nn.softmax(jnp.concatenate([s, s_sink], axis=-1), axis=-1)[..., :-1]
    else:
        p = jax.nn.softmax(s, axis=-1)
    return jnp.einsum('bgrqk,bkgd->bqgrd', p.astype(vals.dtype), vals)


def window_attention(lat, ctxp, sink, cos, sin, need_ctx):
    q, k, v = lat
    qc, kc, vc = ctxp
    bsz, n_tok = q.shape[:2]
    n_ctx = qc.shape[1]
    nb = n_tok // BLOCK
    rep = A_HEADS // A_KV
    scale = HEAD_DIM ** -0.5
    q = apply_axial_rope(q.reshape(bsz, n_tok, A_HEADS, HEAD_DIM), cos, sin) * scale
    k = apply_axial_rope(k.reshape(bsz, n_tok, A_KV, HEAD_DIM), cos, sin)
    v = v.reshape(bsz, n_tok, A_KV, HEAD_DIM)
    kc = kc.reshape(bsz, n_ctx, A_KV, HEAD_DIM)
    vc = vc.reshape(bsz, n_ctx, A_KV, HEAD_DIM)
    sink = sink.reshape(A_KV, rep)

    def band(t):
        tb = t.reshape(bsz, nb, BLOCK, A_KV, HEAD_DIM)
        pad = jnp.zeros_like(tb[:, :1])
        prev = jnp.concatenate([pad, tb[:, :-1]], axis=1)
        nxt = jnp.concatenate([tb[:, 1:], pad], axis=1)
        return jnp.moveaxis(jnp.concatenate([prev, tb, nxt], axis=2), 1, 0)

    qb = jnp.moveaxis(q.reshape(bsz, nb, BLOCK, A_KV, rep, HEAD_DIM), 1, 0)
    blk = jnp.arange(nb)[:, None, None]
    q_pos = blk * BLOCK + jnp.arange(BLOCK)[None, :, None]
    k_pos = (blk - 1) * BLOCK + jnp.arange(3 * BLOCK)[None, None, :]
    band_mask = (jnp.abs(q_pos - k_pos) <= WINDOW) & (k_pos >= 0) & (k_pos < n_tok)
    ctx_mask = jnp.ones((BLOCK, n_ctx), dtype=bool)

    def latent_block(args):
        qblk, kb, vb, m = args
        return attend(qblk, jnp.concatenate([kb, kc], axis=1), jnp.concatenate([vb, vc], axis=1),
                      jnp.concatenate([m, ctx_mask], axis=-1), sink)

    y = lax.map(latent_block, (qb, band(k), band(v), band_mask))
    y = jnp.moveaxis(y, 0, 1).reshape(bsz, n_tok, A_HEADS * HEAD_DIM)
    y_ctx = None
    if need_ctx:
        qch = qc.reshape(bsz, n_ctx, A_KV, rep, HEAD_DIM) * scale
        y_ctx = attend(qch, kc, vc, sink=sink).reshape(bsz, n_ctx, A_HEADS * HEAD_DIM)
    return y, y_ctx


def dense_attention(lat, ctxp, g_q, g_k, cos, sin, need_ctx):
    q, k, v = lat
    qc, kc, vc = ctxp
    bsz, n_tok = q.shape[:2]
    n_ctx = qc.shape[1]
    nb = n_tok // BLOCK
    rep = B_HEADS // B_KV
    scale = HEAD_DIM ** -0.5
    q = apply_axial_rope(rmsnorm(q.reshape(bsz, n_tok, B_HEADS, HEAD_DIM), g_q), cos, sin) * scale
    k = apply_axial_rope(rmsnorm(k.reshape(bsz, n_tok, B_KV, HEAD_DIM), g_k), cos, sin)
    v = v.reshape(bsz, n_tok, B_KV, HEAD_DIM)
    kc = rmsnorm(kc.reshape(bsz, n_ctx, B_KV, HEAD_DIM), g_k)
    vc = vc.reshape(bsz, n_ctx, B_KV, HEAD_DIM)
    keys = jnp.concatenate([kc, k], axis=1)
    vals = jnp.concatenate([vc, v], axis=1)
    qb = jnp.moveaxis(q.reshape(bsz, nb, BLOCK, B_KV, rep, HEAD_DIM), 1, 0)
    y = lax.map(lambda qblk: attend(qblk, keys, vals), qb)
    y = jnp.moveaxis(y, 0, 1).reshape(bsz, n_tok, B_HEADS * HEAD_DIM)
    y_ctx = None
    if need_ctx:
        qch = rmsnorm(qc.reshape(bsz, n_ctx, B_KV, rep, HEAD_DIM), g_q) * scale
        y_ctx = attend(qch, kc, vc).reshape(bsz, n_ctx, B_HEADS * HEAD_DIM)
    return y, y_ctx


def flip_time(t, direction):
    return t[:, ::-1] if direction == 1 else t


def mlstm_chunk_scan(q, k, v, ig, fg, state):
    bsz, n_tok, nh, _ = q.shape
    nc = n_tok // C_CHUNK

    def chunks(t):
        return jnp.moveaxis(t.reshape(bsz, nc, C_CHUNK, *t.shape[2:]), 1, 0)

    logf = jax.nn.log_sigmoid(fg)
    tril = jnp.tril(jnp.ones((C_CHUNK, C_CHUNK), dtype=bool))[None, :, :, None]

    def body(carry, inp):
        c_mat, n_vec, m_prev = carry
        qc, kc, vc, ic, lfc = inp
        b = jnp.cumsum(lfc, axis=1)
        dmat = jnp.where(tril, b[:, :, None, :] - b[:, None, :, :] + ic[:, None, :, :], -jnp.inf)
        m_prior = b + m_prev[:, None, :]
        m_t = jnp.maximum(m_prior, dmat.max(axis=2))
        w = jnp.exp(dmat - m_t[:, :, None, :])
        s = jnp.einsum('bthd,bshd->btsh', qc, kc) * w
        decay_prior = jnp.exp(m_prior - m_t)
        num = jnp.einsum('btsh,bshv->bthv', s, vc) + decay_prior[..., None] * jnp.einsum('bthd,bhvd->bthv', qc, c_mat)
        den = s.sum(axis=2) + decay_prior * jnp.einsum('bthd,bhd->bth', qc, n_vec)
        h = num / jnp.maximum(jnp.abs(den), jnp.exp(-m_t))[..., None]
        b_end = b[:, -1]
        g = b_end[:, None, :] - b + ic
        m_new = jnp.maximum(b_end + m_prev, g.max(axis=1))
        wk = jnp.exp(g - m_new[:, None, :])
        carry_decay = jnp.exp(b_end + m_prev - m_new)
        c_new = carry_decay[..., None, None] * c_mat + jnp.einsum('bsh,bshv,bshd->bhvd', wk, vc, kc)
        n_new = carry_decay[..., None] * n_vec + jnp.einsum('bsh,bshd->bhd', wk, kc)
        return (c_new, n_new, m_new), h

    state, hs = lax.scan(body, state, (chunks(q), chunks(k), chunks(v), chunks(ig), chunks(logf)))
    return jnp.moveaxis(hs, 0, 1).reshape(bsz, n_tok, nh, v.shape[-1]), state


def mlstm_mixer(lat, ctxp, b_i, b_f, g_head, need_ctx):
    f32 = jnp.float32

    def heads(q, k, v, gates):
        bsz, n = q.shape[:2]
        qh = q.reshape(bsz, n, C_HEADS, C_DK).astype(f32) * C_DK ** -0.5
        kh = k.reshape(bsz, n, C_HEADS, C_DK).astype(f32)
        vh = v.reshape(bsz, n, C_HEADS, C_DV).astype(f32)
        g = gates.reshape(bsz, n, 4, C_HEADS).astype(f32)
        ig = g[:, :, 0::2] + b_i.astype(f32)
        fg = g[:, :, 1::2] + b_f.astype(f32)
        return qh, kh, vh, ig, fg

    q, k, v, o, gates = lat
    qc, kc, vc, oc, gatesc = ctxp
    lat_h = heads(q, k, v, gates)
    ctx_h = heads(qc, kc, vc, gatesc)
    bsz = q.shape[0]
    outs_lat, outs_ctx = [], []
    for d in range(2):
        state0 = (jnp.zeros((bsz, C_HEADS, C_DV, C_DK), f32), jnp.zeros((bsz, C_HEADS, C_DK), f32),
                  jnp.zeros((bsz, C_HEADS), f32))
        cq, ck, cv, ci, cf = ctx_h
        hc, st = mlstm_chunk_scan(flip_time(cq, d), flip_time(ck, d), flip_time(cv, d),
                                  flip_time(ci[:, :, d], d), flip_time(cf[:, :, d], d), state0)
        lq, lk, lv, li, lf = lat_h
        hl, _ = mlstm_chunk_scan(flip_time(lq, d), flip_time(lk, d), flip_time(lv, d),
                                 flip_time(li[:, :, d], d), flip_time(lf[:, :, d], d), st)
        outs_lat.append(flip_time(hl, d))
        outs_ctx.append(flip_time(hc, d))

    def finish(h, og):
        b, n = h.shape[:2]
        hn = rmsnorm(h, g_head.reshape(C_HEADS, C_DV)).reshape(b, n, C_HEADS * C_DV)
        return (hn * jax.nn.sigmoid(og.astype(f32))).astype(og.dtype)

    y = finish(outs_lat[0] + outs_lat[1], o)
    y_ctx = finish(outs_ctx[0] + outs_ctx[1], oc) if need_ctx else None
    return y, y_ctx


def depthwise_conv(u, w, b):
    ch = u.shape[-1]
    y = lax.conv_general_dilated(u, w[:, None, :].astype(u.dtype), window_strides=(1,),
                                 padding=[(M_CONV // 2, M_CONV // 2)],
                                 dimension_numbers=('NWC', 'WIO', 'NWC'), feature_group_count=ch)
    return y + b.astype(u.dtype)


def ssd_chunk_scan(x, dt, a, bmat, cmat, h0):
    bsz, n_tok, nh, hp = x.shape
    nc = n_tok // M_CHUNK

    def chunks(t):
        return jnp.moveaxis(t.reshape(bsz, nc, M_CHUNK, *t.shape[2:]), 1, 0)

    tril = jnp.tril(jnp.ones((M_CHUNK, M_CHUNK), dtype=bool))[None, :, :, None]

    def body(h, inp):
        xc, dtc, bc, cc = inp
        cum = jnp.cumsum(dtc * a, axis=1)
        decay = jnp.exp(jnp.where(tril, cum[:, :, None, :] - cum[:, None, :, :], -jnp.inf))
        s = jnp.einsum('bthn,bshn->btsh', cc, bc) * decay * dtc[:, None, :, :]
        y = jnp.einsum('btsh,bshp->bthp', s, xc) + jnp.exp(cum)[..., None] * jnp.einsum('bthn,bhpn->bthp', cc, h)
        w_end = jnp.exp(cum[:, -1:, :] - cum) * dtc
        h_new = jnp.exp(cum[:, -1, :])[:, :, None, None] * h + jnp.einsum('bsh,bshp,bshn->bhpn', w_end, xc, bc)
        return h_new, y

    h_fin, ys = lax.scan(body, h0, (chunks(x), chunks(dt), chunks(bmat), chunks(cmat)))
    return jnp.moveaxis(ys, 0, 1).reshape(bsz, n_tok, nh, hp), h_fin


def mamba_mixer(lat, ctxp, conv_w, conv_b, a_log, dt_bias, d_skip, g_ssm, need_ctx):
    f32 = jnp.float32

    def prep(xm, bm, cm, dt):
        bsz, n = xm.shape[:2]
        u = jax.nn.silu(depthwise_conv(jnp.concatenate([xm, bm, cm], axis=-1), conv_w, conv_b)).astype(f32)
        xs, bs, cs = jnp.split(u, [M_WIDTH, M_WIDTH + M_GROUPS * M_DSTATE], axis=-1)
        rep = M_HEADS // M_GROUPS
        xs = xs.reshape(bsz, n, M_HEADS, M_HEADDIM)
        bs = jnp.repeat(bs.reshape(bsz, n, M_GROUPS, M_DSTATE), rep, axis=2)
        cs = jnp.repeat(cs.reshape(bsz, n, M_GROUPS, M_DSTATE), rep, axis=2)
        dts = jax.nn.softplus(dt.reshape(bsz, n, 2, M_HEADS).astype(f32) + dt_bias.astype(f32))
        return xs, bs, cs, dts

    xm, z, bm, cm, dt = lat
    xmc, zc, bmc, cmc, dtc = ctxp
    lat_p = prep(xm, bm, cm, dt)
    ctx_p = prep(xmc, bmc, cmc, dtc)
    a = -jnp.exp(a_log.astype(f32))
    bsz = xm.shape[0]
    outs_lat, outs_ctx = [], []
    for d in range(2):
        h0 = jnp.zeros((bsz, M_HEADS, M_HEADDIM, M_DSTATE), f32)
        cx, cb, cc, cdt = ctx_p
        yc, st = ssd_chunk_scan(flip_time(cx, d), flip_time(cdt[:, :, d], d), a[d],
                                flip_time(cb, d), flip_time(cc, d), h0)
        lx, lb, lc, ldt = lat_p
        yl, _ = ssd_chunk_scan(flip_time(lx, d), flip_time(ldt[:, :, d], d), a[d],
                               flip_time(lb, d), flip_time(lc, d), st)
        outs_lat.append(flip_time(yl, d))
        outs_ctx.append(flip_time(yc, d))

    def finish(y, xs, zz):
        b, n = zz.shape[:2]
        y = (y + d_skip.astype(f32)[:, None] * xs).reshape(b, n, M_WIDTH)
        return rmsnorm(y * jax.nn.silu(zz.astype(f32)), g_ssm).astype(zz.dtype)

    y = finish(outs_lat[0] + outs_lat[1], lat_p[0], z)
    y_ctx = finish(outs_ctx[0] + outs_ctx[1], ctx_p[0], zc) if need_ctx else None
    return y, y_ctx


def sq_relu_ffn(h, w1, w2):
    return jnp.square(jax.nn.relu(h @ w1)) @ w2


def setup_inputs(seed: int = 0) -> dict:
    key = jax.random.key(seed)
    ks = jax.random.split(key, 26)
    f32 = jnp.float32

    def nrm(k, shape, s):
        return jax.random.normal(k, shape, f32) * s

    dt0 = jnp.exp(jax.random.uniform(ks[18], (DEPTH, 2, M_HEADS), f32, math.log(1e-3), math.log(1e-1)))
    return {
        'x': nrm(ks[0], (BATCH, SEQ, D_MODEL), 1.0),
        'c': nrm(ks[1], (BATCH, D_MODEL), 1.0),
        'ctx': nrm(ks[2], (BATCH, CTX_LEN, D_MODEL), 1.0),
        'c_ctx': nrm(ks[3], (D_MODEL,), 1.0),
        'w_ada': nrm(ks[4], (DEPTH, D_MODEL, 6 * D_MODEL), 0.5 * D_MODEL ** -0.5),
        'b_ada': nrm(ks[5], (DEPTH, 6 * D_MODEL), 0.02),
        'g_norm1': 1.0 + nrm(ks[6], (DEPTH, D_MODEL), 0.02),
        'g_norm2': 1.0 + nrm(ks[7], (DEPTH, D_MODEL), 0.02),
        'w_in': nrm(ks[8], (DEPTH, D_MODEL, N_IN), D_MODEL ** -0.5),
        'sink_a': nrm(ks[9], (DEPTH, A_HEADS), 1.0),
        'g_q_b': 1.0 + nrm(ks[10], (DEPTH, HEAD_DIM), 0.02),
        'g_k_b': 1.0 + nrm(ks[11], (DEPTH, HEAD_DIM), 0.02),
        'b_igate': nrm(ks[12], (DEPTH, 2, C_HEADS), 0.1),
        'b_fgate': jnp.linspace(3.0, 6.0, C_HEADS, dtype=f32)[None, None, :] + nrm(ks[13], (DEPTH, 2, C_HEADS), 0.1),
        'g_mlstm': 1.0 + nrm(ks[14], (DEPTH, C_HEADS * C_DV), 0.02),
        'conv_w': nrm(ks[15], (DEPTH, M_CONV, CONV_CH), M_CONV ** -0.5),
        'conv_b': nrm(ks[16], (DEPTH, CONV_CH), 0.02),
        'a_log': jnp.log(jax.random.uniform(ks[17], (DEPTH, 2, M_HEADS), f32, 1.0, 16.0)),
        'dt_bias': dt0 + jnp.log(-jnp.expm1(-dt0)),
        'd_skip': 1.0 + nrm(ks[19], (DEPTH, M_HEADS), 0.1),
        'g_ssm': 1.0 + nrm(ks[20], (DEPTH, M_WIDTH), 0.02),
        'w_out': nrm(ks[21], (DEPTH, D_MIX, D_MODEL), D_MIX ** -0.5),
        'w_ff1': nrm(ks[22], (DEPTH, D_MODEL, D_FF), D_MODEL ** -0.5),
        'w_ff2': nrm(ks[23], (DEPTH, D_FF, D_MODEL), D_FF ** -0.5),
        'g_final': 1.0 + nrm(ks[24], (D_MODEL,), 0.02),
    }


def reference(x, c, ctx, c_ctx, w_ada, b_ada, g_norm1, g_norm2, w_in, sink_a, g_q_b, g_k_b,
              b_igate, b_fgate, g_mlstm, conv_w, conv_b, a_log, dt_bias, d_skip, g_ssm,
              w_out, w_ff1, w_ff2, g_final):
    n_tok = x.shape[1]
    rows = n_tok // GRID_W
    cos, sin = axial_rope_tables(rows, x.dtype)
    xc = ctx
    for layer in range(DEPTH):
        need_ctx = layer < DEPTH - 1
        mod = jax.nn.silu(c) @ w_ada[layer] + b_ada[layer]
        mod_c = jax.nn.silu(c_ctx) @ w_ada[layer] + b_ada[layer]
        sh1, sc1, gt1, sh2, sc2, gt2 = jnp.split(mod[:, None, :], 6, axis=-1)
        sh1c, sc1c, gt1c, sh2c, sc2c, gt2c = jnp.split(mod_c[None, None, :], 6, axis=-1)

        h = rmsnorm(x, g_norm1[layer]) * (1 + sc1) + sh1
        hc = rmsnorm(xc, g_norm1[layer]) * (1 + sc1c) + sh1c
        p = jnp.split(h @ w_in[layer], SPLIT_POINTS, axis=-1)
        pc = jnp.split(hc @ w_in[layer], SPLIT_POINTS, axis=-1)
        ya, ya_c = window_attention(p[0:3], pc[0:3], sink_a[layer], cos, sin, need_ctx)
        yb, yb_c = dense_attention(p[3:6], pc[3:6], g_q_b[layer], g_k_b[layer], cos, sin, need_ctx)
        ym, ym_c = mlstm_mixer(p[6:11], pc[6:11], b_igate[layer], b_fgate[layer], g_mlstm[layer], need_ctx)
        yd, yd_c = mamba_mixer(p[11:16], pc[11:16], conv_w[layer], conv_b[layer], a_log[layer],
                               dt_bias[layer], d_skip[layer], g_ssm[layer], need_ctx)
        x = x + gt1 * (jnp.concatenate([ya, yb, ym, yd], axis=-1) @ w_out[layer])
        h2 = rmsnorm(x, g_norm2[layer]) * (1 + sc2) + sh2
        x = x + gt2 * sq_relu_ffn(h2, w_ff1[layer], w_ff2[layer])

        if need_ctx:
            xc = xc + gt1c * (jnp.concatenate([ya_c, yb_c, ym_c, yd_c], axis=-1) @ w_out[layer])
            h2c = rmsnorm(xc, g_norm2[layer]) * (1 + sc2c) + sh2c
            xc = xc + gt2c * sq_relu_ffn(h2c, w_ff1[layer], w_ff2[layer])
    return rmsnorm(x, g_final)
```

```python
import functools

import numpy as np
import jax
import jax.numpy as jnp
from jax import lax
from jax.experimental import pallas as pl
from jax.experimental.pallas import tpu as pltpu

F32 = jnp.float32
MXU_DTYPE = jnp.bfloat16

D_MODEL = 1024
HEAD_DIM = 64
LANES = 128
GRID_W = 64
WINDOW = 128
ROPE_BASE = 10000.0
EPS = 1e-6
N_HEADS = 4
GROUP_WIDTH = N_HEADS * HEAD_DIM
D_STATE = 128
CHUNK = 128
ROW_TILE = 256
D_FF = 4 * D_MODEL
FF_CHUNK = 1024
HALO = 8
VMEM_LIMIT = 56 * 1024 * 1024

_SPLIT_SIZES = (256, 128, 128, 256, 128, 128, 256, 256, 256, 256, 16, 256, 256, 256, 256, 8)
_OFF = np.concatenate([[0], np.cumsum(_SPLIT_SIZES)]).astype(np.int64)
_Q_HEAD_ORDER = (0, 2, 1, 3)

_COL_A = 0
_COL_B = 512
_COL_C = 1024
_COL_F = 1792
_N_F = 768 + 256 + 256 + 128
_N_PROJ = _COL_F + _N_F

_HIGHEST = lax.Precision.HIGHEST


def _dot(a, b):
    return jnp.dot(a, b, preferred_element_type=F32)


def _dot_nt(a, b):
    return lax.dot_general(a, b, (((1,), (1,)), ((), ())), preferred_element_type=F32)


def _dot_tn(a, b):
    return lax.dot_general(a, b, (((0,), (0,)), ((), ())), preferred_element_type=F32)


def _dot_f32(a, b):
    return jnp.dot(a, b, precision=_HIGHEST, preferred_element_type=F32)


def _lane_is_low(shape):
    lane = lax.broadcasted_iota(jnp.int32, shape, len(shape) - 1)
    return (lane % LANES) < HEAD_DIM


def _head_block_diag(scale):
    r = lax.broadcasted_iota(jnp.int32, (LANES, LANES), 0)
    c = lax.broadcasted_iota(jnp.int32, (LANES, LANES), 1)
    return jnp.where((r < HEAD_DIM) == (c < HEAD_DIM), scale, 0.0).astype(F32)


def _params(*sem):
    return pltpu.CompilerParams(dimension_semantics=sem, vmem_limit_bytes=VMEM_LIMIT)


def _ada_kernel(c_ref, w_ref, b_ref, out_ref):
    cv = c_ref[...]
    cv = cv * jax.nn.sigmoid(cv)
    out_ref[...] = _dot(cv.astype(MXU_DTYPE), w_ref[...].astype(MXU_DTYPE)) + b_ref[...]


def _ada_mod(cvec, w_ada, b_ada):
    depth, d, n = w_ada.shape
    rows = cvec.shape[0]
    tn = 1536
    return pl.pallas_call(
        _ada_kernel,
        grid=(depth, n // tn),
        in_specs=[
            pl.BlockSpec((rows, d), lambda l, j: (0, 0)),
            pl.BlockSpec((None, d, tn), lambda l, j: (l, 0, j)),
            pl.BlockSpec((None, 1, tn), lambda l, j: (l, 0, j)),
        ],
        out_specs=pl.BlockSpec((None, rows, tn), lambda l, j: (l, 0, j)),
        out_shape=jax.ShapeDtypeStruct((depth, rows, n), F32),
        compiler_params=_params("parallel", "parallel"),
        name="ada_mod",
    )(cvec, w_ada, b_ada.reshape(depth, 1, n))


def _rmsnorm_mod(x, g, shift, scale):
    ms = jnp.mean(x * x, axis=-1, keepdims=True)
    return (x * lax.rsqrt(ms + EPS) * g) * (1.0 + scale) + shift


def _in_proj_kernel(x_ref, mod_ref, g1_ref, w_ref, cos_ref, sin_ref, gq_ref, gk_ref,
                    a_ref, b_ref, c_ref, f_ref):
    hn = _rmsnorm_mod(x_ref[...], g1_ref[...], mod_ref[0:1, :], mod_ref[1:2, :])
    hb = hn.astype(MXU_DTYPE)
    cos = cos_ref[...]
    sin = sin_ref[...]
    lane = lax.broadcasted_iota(jnp.int32, (1, LANES), 1)
    first = (lane % 32) < 16

    def rope(t):
        partner = jnp.where(first, pltpu.roll(t, LANES - 16, 1), pltpu.roll(t, 16, 1))
        return t * cos + partner * sin

    pa = _dot(hb, w_ref[:, _COL_A:_COL_A + 512])
    for j in range(3):
        a_ref[:, j * LANES:(j + 1) * LANES] = rope(pa[:, j * LANES:(j + 1) * LANES]).astype(a_ref.dtype)
    a_ref[:, 3 * LANES:] = pa[:, 3 * LANES:].astype(a_ref.dtype)

    pb = _dot(hb, w_ref[:, _COL_B:_COL_B + 512])
    bd = _head_block_diag(1.0 / HEAD_DIM)
    for j in range(3):
        t = pb[:, j * LANES:(j + 1) * LANES]
        g = gq_ref[...] if j < 2 else gk_ref[...]
        t = t * lax.rsqrt(_dot_f32(t * t, bd) + EPS) * g
        b_ref[:, j * LANES:(j + 1) * LANES] = rope(t).astype(b_ref.dtype)
    b_ref[:, 3 * LANES:] = pb[:, 3 * LANES:].astype(b_ref.dtype)

    c_ref[...] = _dot(hb, w_ref[:, _COL_C:_COL_F]).astype(c_ref.dtype)
    f_ref[...] = _dot(hb, w_ref[:, _COL_F:])


def _in_proj(xc, modv, g1, w, cos, sin, gq, gk):
    bsz, u, d = xc.shape
    nt = u // ROW_TILE
    row = lambda b, i: (b, i, 0)
    const = lambda b, i: (0, 0)
    return pl.pallas_call(
        _in_proj_kernel,
        grid=(bsz, nt),
        in_specs=[
            pl.BlockSpec((None, ROW_TILE, d), row),
            pl.BlockSpec((None, None, 8, d), lambda b, i: (b, jnp.minimum(i, 1), 0, 0)),
            pl.BlockSpec((1, d), const),
            pl.BlockSpec((d, _N_PROJ), const),
            pl.BlockSpec((ROW_TILE, LANES), lambda b, i: (i, 0)),
            pl.BlockSpec((ROW_TILE, LANES), lambda b, i: (i, 0)),
            pl.BlockSpec((1, LANES), const),
            pl.BlockSpec((1, LANES), const),
        ],
        out_specs=[
            pl.BlockSpec((None, ROW_TILE, 512), row),
            pl.BlockSpec((None, ROW_TILE, 512), row),
            pl.BlockSpec((None, ROW_TILE, 768), row),
            pl.BlockSpec((None, ROW_TILE, _N_F), row),
        ],
        out_shape=[
            jax.ShapeDtypeStruct((bsz, u, 512), MXU_DTYPE),
            jax.ShapeDtypeStruct((bsz, u, 512), MXU_DTYPE),
            jax.ShapeDtypeStruct((bsz, u, 768), MXU_DTYPE),
            jax.ShapeDtypeStruct((bsz, u, _N_F), F32),
        ],
        compiler_params=_params("parallel", "parallel"),
        name="in_proj",
    )(xc, modv, g1, w, cos, sin, gq, gk)


def _attn_a_kernel(sink_ref, q_ref, kp_ref, ko_ref, kn_ref, kc_ref, vp_ref, vo_ref, vn_ref, vc_ref,
                   out_ref, kbuf, vbuf, *, n_tok, n_ctx, q_off):
    i = pl.program_id(1) + q_off
    nb = 2 * WINDOW + ROW_TILE
    nk = nb + n_ctx
    kbuf[0:WINDOW] = kp_ref[...]
    kbuf[WINDOW:WINDOW + ROW_TILE] = ko_ref[...]
    kbuf[WINDOW + ROW_TILE:nb] = kn_ref[...]
    kbuf[nb:nk] = kc_ref[...]
    vbuf[0:WINDOW] = vp_ref[...]
    vbuf[WINDOW:WINDOW + ROW_TILE] = vo_ref[...]
    vbuf[WINDOW + ROW_TILE:nb] = vn_ref[...]
    vbuf[nb:nk] = vc_ref[...]

    r = lax.broadcasted_iota(jnp.int32, (ROW_TILE, nk), 0)
    c = lax.broadcasted_iota(jnp.int32, (ROW_TILE, nk), 1)
    kpos = (i - 1) * ROW_TILE - WINDOW + c
    dist = c - r
    band = jnp.where(dist >= 0, jnp.where(dist <= 2 * WINDOW, 1, 0), 0)
    band = jnp.where(kpos >= 0, jnp.where(kpos < n_tok, band, 0), 0)
    band = jnp.where(i >= 1, band, 0)
    bias = jnp.where(c >= nb, 0.0, jnp.where(band > 0, 0.0, -jnp.inf)).astype(F32)

    low = _lane_is_low((1, LANES))
    k = kbuf[...]
    v = vbuf[...]
    for jb in range(2):
        q = q_ref[:, jb * LANES:(jb + 1) * LANES]
        outs = []
        for half in range(2):
            sink = sink_ref[_Q_HEAD_ORDER[2 * jb + half]]
            qm = jnp.where(low if half == 0 else jnp.logical_not(low), q, jnp.zeros_like(q))
            s = _dot_nt(qm, k) + bias
            m = jnp.maximum(jnp.max(s, axis=-1, keepdims=True), sink)
            p = jnp.exp(s - m)
            l = jnp.sum(p, axis=-1, keepdims=True) + jnp.exp(sink - m)
            outs.append(_dot(p.astype(MXU_DTYPE), v) / l)
        out_ref[:, jb * LANES:(jb + 1) * LANES] = jnp.where(low, outs[0], outs[1]).astype(out_ref.dtype)


def _attn_a(pa, sink, n_ctx, need_ctx):
    bsz, u, _ = pa.shape
    n_tok = u - n_ctx
    q_off = 0 if need_ctx else n_ctx // ROW_TILE
    nt = u // ROW_TILE - q_off
    last_blk = u // WINDOW - 1
    rpw = ROW_TILE // WINDOW
    nk = 2 * WINDOW + ROW_TILE + n_ctx

    def own(col):
        return lambda b, i: (b, i + q_off, col)

    def prev(col):
        return lambda b, i: (b, jnp.maximum((i + q_off) * rpw - 1, 0), col)

    def nxt(col):
        return lambda b, i: (b, jnp.minimum((i + q_off + 1) * rpw, last_blk), col)

    def ctx(col):
        return lambda b, i: (b, 0, col)

    kern = functools.partial(_attn_a_kernel, n_tok=n_tok, n_ctx=n_ctx, q_off=q_off)
    return pl.pallas_call(
        kern,
        grid=(bsz, nt),
        in_specs=[
            pl.BlockSpec(memory_space=pltpu.SMEM),
            pl.BlockSpec((None, ROW_TILE, 2 * LANES), own(0)),
            pl.BlockSpec((None, WINDOW, LANES), prev(2)),
            pl.BlockSpec((None, ROW_TILE, LANES), own(2)),
            pl.BlockSpec((None, WINDOW, LANES), nxt(2)),
            pl.BlockSpec((None, n_ctx, LANES), ctx(2)),
            pl.BlockSpec((None, WINDOW, LANES), prev(3)),
            pl.BlockSpec((None, ROW_TILE, LANES), own(3)),
            pl.BlockSpec((None, WINDOW, LANES), nxt(3)),
            pl.BlockSpec((None, n_ctx, LANES), ctx(3)),
        ],
        out_specs=pl.BlockSpec((None, ROW_TILE, 2 * LANES), lambda b, i: (b, i + q_off, 0)),
        out_shape=jax.ShapeDtypeStruct((bsz, u, 2 * LANES), MXU_DTYPE),
        scratch_shapes=[pltpu.VMEM((nk, LANES), MXU_DTYPE), pltpu.VMEM((nk, LANES), MXU_DTYPE)],
        compiler_params=_params("parallel", "parallel"),
        name="attn_window",
    )(sink, pa, pa, pa, pa, pa, pa, pa, pa, pa)


def _attn_b_kernel(q_ref, k_ref, v_ref, out_ref, m_ref, l_ref, acc_ref, *, n_ctx, q_off):
    qi = pl.program_id(1) + q_off
    ki = pl.program_id(2)
    low = _lane_is_low((1, LANES))

    @pl.when(ki == 0)
    def _():
        m_ref[...] = jnp.full(m_ref.shape, -jnp.inf, F32)
        l_ref[...] = jnp.zeros(l_ref.shape, F32)
        acc_ref[...] = jnp.zeros(acc_ref.shape, F32)

    def step(n_keys):
        k = k_ref[0:n_keys, :]
        v = v_ref[0:n_keys, :]
        for jb in range(2):
            q = q_ref[:, jb * LANES:(jb + 1) * LANES]
            for half in range(2):
                idx = 2 * jb + half
                qm = jnp.where(low if half == 0 else jnp.logical_not(low), q, jnp.zeros_like(q))
                s = _dot_nt(qm, k)
                m_prev = m_ref[idx]
                m_new = jnp.maximum(m_prev, jnp.max(s, axis=-1, keepdims=True))
                alpha = jnp.exp(m_prev - m_new)
                p = jnp.exp(s - m_new)
                l_ref[idx] = alpha * l_ref[idx] + jnp.sum(p, axis=-1, keepdims=True)
                acc_ref[idx] = alpha * acc_ref[idx] + _dot(p.astype(MXU_DTYPE), v)
                m_ref[idx] = m_new

    if q_off == 0:
        @pl.when((qi == 0) & (ki == 0))
        def _():
            step(n_ctx)

        @pl.when(qi > 0)
        def _():
            step(k_ref.shape[0])
    else:
        step(k_ref.shape[0])

    @pl.when(ki == pl.num_programs(2) - 1)
    def _():
        for jb in range(2):
            o = jnp.where(low, acc_ref[2 * jb] / l_ref[2 * jb], acc_ref[2 * jb + 1] / l_ref[2 * jb + 1])
            out_ref[:, jb * LANES:(jb + 1) * LANES] = o.astype(out_ref.dtype)


def _attn_b(pb, n_ctx, need_ctx):
    bsz, u, _ = pb.shape
    q_off = 0 if need_ctx else n_ctx // ROW_TILE
    nt = u // ROW_TILE - q_off
    nk = 2
    tk = u // nk
    kern = functools.partial(_attn_b_kernel, n_ctx=n_ctx, q_off=q_off)
    return pl.pallas_call(
        kern,
        grid=(bsz, nt, nk),
        in_specs=[
            pl.BlockSpec((None, ROW_TILE, 2 * LANES), lambda b, i, k: (b, i + q_off, 0)),
            pl.BlockSpec((None, tk, LANES), lambda b, i, k: (b, k, 2)),
            pl.BlockSpec((None, tk, LANES), lambda b, i, k: (b, k, 3)),
        ],
        out_specs=pl.BlockSpec((None, ROW_TILE, 2 * LANES), lambda b, i, k: (b, i + q_off, 0)),
        out_shape=jax.ShapeDtypeStruct((bsz, u, 2 * LANES), MXU_DTYPE),
        scratch_shapes=[
            pltpu.VMEM((N_HEADS, ROW_TILE, 1), F32),
            pltpu.VMEM((N_HEADS, ROW_TILE, 1), F32),
            pltpu.VMEM((N_HEADS, ROW_TILE, LANES), F32),
        ],
        compiler_params=_params("parallel", "parallel", "arbitrary"),
        name="attn_dense",
    )(pb, pb, pb)


def _scan_chunk(j, direction, nc_ctx, nc):
    if direction == 0:
        return j
    return jnp.where(j < nc_ctx, nc_ctx - 1 - j, nc + nc_ctx - 1 - j)


def _causal_mask(direction):
    t = lax.broadcasted_iota(jnp.int32, (CHUNK, CHUNK), 0)
    s = lax.broadcasted_iota(jnp.int32, (CHUNK, CHUNK), 1)
    return (s <= t) if direction == 0 else (s >= t)


def _log_sigmoid(x):
    return jnp.minimum(x, 0.0) - jnp.log1p(jnp.exp(-jnp.abs(x)))


def _softplus(x):
    return jnp.maximum(x, 0.0) + jnp.log1p(jnp.exp(-jnp.abs(x)))


def _mlstm_kernel(*refs, direction, nc_ctx, nc):
    if direction == 0:
        qkv_ref, g_ref, gbias_ref, out_ref, st_ref, m_ref = refs
    else:
        qkv_ref, g_ref, gbias_ref, hf_ref, o_ref, gh_ref, out_ref, st_ref, m_ref = refs
    j = pl.program_id(1)

    @pl.when(j == 0)
    def _():
        st_ref[...] = jnp.zeros(st_ref.shape, F32)
        m_ref[...] = jnp.zeros(m_ref.shape, F32)

    mask = _causal_mask(direction)
    end = CHUNK - 1 if direction == 0 else 0
    low1 = _lane_is_low((1, LANES))
    low2 = _lane_is_low((1, 2 * LANES))

    gcb = g_ref[...] + gbias_ref[...]
    lf = _log_sigmoid(gcb)
    b_col = _dot_f32(mask.astype(F32), lf)
    b_row = b_col.T
    g_row = gcb.T

    ones = jnp.ones((CHUNK, LANES), MXU_DTYPE)
    r2 = lax.broadcasted_iota(jnp.int32, (LANES, 2 * LANES), 0)
    c2 = lax.broadcasted_iota(jnp.int32, (LANES, 2 * LANES), 1)
    state_mask = (r2 < HEAD_DIM) == ((c2 % LANES) < HEAD_DIM)

    outs = []
    for pair in range(2):
        q = qkv_ref[:, pair * LANES:(pair + 1) * LANES]
        k = qkv_ref[:, GROUP_WIDTH + pair * LANES:GROUP_WIDTH + (pair + 1) * LANES]
        v = qkv_ref[:, 2 * GROUP_WIDTH + pair * LANES:2 * GROUP_WIDTH + (pair + 1) * LANES]
        prior = _dot(q, st_ref[pair].astype(MXU_DTYPE))
        per_head = []
        for half in range(2):
            h = 2 * pair + half
            icol = 8 * direction + h
            fcol = 8 * direction + 4 + h
            bc = b_col[:, fcol:fcol + 1]
            br = b_row[fcol:fcol + 1, :]
            ic = gcb[:, icol:icol + 1]
            ir = g_row[icol:icol + 1, :]
            tot = br[:, end:end + 1]
            m_prev = m_ref[h:h + 1, 0:1]
            dmat = jnp.where(mask, bc + (ir - br), -jnp.inf)
            m_prior = bc + m_prev
            m_t = jnp.maximum(m_prior, jnp.max(dmat, axis=-1, keepdims=True))
            w = jnp.exp(dmat - m_t)
            qm = jnp.where(low1 if half == 0 else jnp.logical_not(low1), q, jnp.zeros_like(q))
            s = _dot_nt(qm, k) * w
            dp = jnp.exp(m_prior - m_t)
            sv = _dot(s.astype(MXU_DTYPE), v)
            den = jnp.sum(s, axis=-1, keepdims=True)
            gc = tot - bc + ic
            m_new = jnp.maximum(tot + m_prev, jnp.max(gc, axis=0, keepdims=True))
            wk = jnp.exp(gc - m_new)
            cd = jnp.exp(tot + m_prev - m_new)
            m_ref[h:h + 1, :] = jnp.broadcast_to(m_new, (1, LANES))
            per_head.append((sv, den, dp, m_t, wk, cd))
        (sv0, den0, dp0, mt0, wk0, cd0), (sv1, den1, dp1, mt1, wk1, cd1) = per_head
        num = jnp.where(low1, sv0 + dp0 * prior[:, :LANES], sv1 + dp1 * prior[:, :LANES])
        den = jnp.where(low1, den0 + dp0 * prior[:, LANES:], den1 + dp1 * prior[:, LANES:])
        m_l = jnp.where(low1, mt0, mt1)
        outs.append(num / jnp.maximum(jnp.abs(den), jnp.exp(-m_l)))
        kw = (k.astype(F32) * jnp.where(low1, wk0, wk1)).astype(MXU_DTYPE)
        upd = _dot_tn(kw, jnp.concatenate([v, ones], axis=1))
        st_ref[pair] = st_ref[pair] * jnp.where(low2, cd0, cd1) + jnp.where(state_mask, upd, 0.0)

    if direction == 0:
        for pair in range(2):
            out_ref[:, pair * LANES:(pair + 1) * LANES] = outs[pair]
    else:
        bd = _head_block_diag(1.0 / HEAD_DIM)
        for pair in range(2):
            sl = slice(pair * LANES, (pair + 1) * LANES)
            h = hf_ref[:, sl] + outs[pair]
            hn = h * lax.rsqrt(_dot_f32(h * h, bd) + EPS) * gh_ref[:, sl]
            out_ref[:, sl] = (hn * jax.nn.sigmoid(o_ref[:, sl])).astype(out_ref.dtype)


def _mlstm(pc, pf, gbias, g_head, n_ctx):
    bsz, u, _ = pc.shape
    nc = u // CHUNK
    nc_ctx = n_ctx // CHUNK
    gate_blk = (_N_F - LANES) // LANES
    o_blk = 768 // GROUP_WIDTH

    def call(direction, extra_in, extra_specs, out_dtype):
        chunk = lambda b, j: _scan_chunk(j, direction, nc_ctx, nc)
        kern = functools.partial(_mlstm_kernel, direction=direction, nc_ctx=nc_ctx, nc=nc)
        return pl.pallas_call(
            kern,
            grid=(bsz, nc),
            in_specs=[
                pl.BlockSpec((None, CHUNK, 768), lambda b, j: (b, chunk(b, j), 0)),
                pl.BlockSpec((None, CHUNK, LANES), lambda b, j: (b, chunk(b, j), gate_blk)),
                pl.BlockSpec((1, LANES), lambda b, j: (0, 0)),
            ] + extra_specs(chunk),
            out_specs=pl.BlockSpec((None, CHUNK, GROUP_WIDTH), lambda b, j: (b, chunk(b, j), 0)),
            out_shape=jax.ShapeDtypeStruct((bsz, u, GROUP_WIDTH), out_dtype),
            scratch_shapes=[pltpu.VMEM((2, LANES, 2 * LANES), F32), pltpu.VMEM((8, LANES), F32)],
            compiler_params=_params("parallel", "arbitrary"),
            name="mlstm_fwd" if direction == 0 else "mlstm_bwd",
        )(pc, pf, gbias, *extra_in)

    hf = call(0, (), lambda chunk: [], F32)
    bwd_specs = lambda chunk: [
        pl.BlockSpec((None, CHUNK, GROUP_WIDTH), lambda b, j: (b, chunk(b, j), 0)),
        pl.BlockSpec((None, CHUNK, GROUP_WIDTH), lambda b, j: (b, chunk(b, j), o_blk)),
        pl.BlockSpec((1, GROUP_WIDTH), lambda b, j: (0, 0)),
    ]
    return call(1, (hf, pf, g_head), bwd_specs, MXU_DTYPE)


def _ssd_kernel(*refs, direction, nc_ctx, nc):
    if direction == 0:
        x_ref, xp_ref, xn_ref, g_ref, gbias_ref, alog_ref, cw_ref, cb_ref, out_ref, st_ref = refs
    else:
        (x_ref, xp_ref, xn_ref, g_ref, gbias_ref, alog_ref, cw_ref, cb_ref,
         yf_ref, z_ref, dskip_ref, gs_ref, out_ref, st_ref) = refs
    j = pl.program_id(1)
    c = _scan_chunk(j, direction, nc_ctx, nc)

    @pl.when(j == 0)
    def _():
        st_ref[...] = jnp.zeros(st_ref.shape, F32)

    xin = x_ref[...]
    has_prev = (c != 0) & (c != nc_ctx)
    has_next = (c != nc_ctx - 1) & (c != nc - 1)
    prow = jnp.where(has_prev, xp_ref[HALO - 1:HALO, :], 0.0)
    nrow = jnp.where(has_next, xn_ref[0:1, :], 0.0)
    ridx = lax.broadcasted_iota(jnp.int32, (CHUNK, 1), 0)
    up = jnp.where(ridx == 0, prow, pltpu.roll(xin, 1, 0))
    dn = jnp.where(ridx == CHUNK - 1, nrow, pltpu.roll(xin, CHUNK - 1, 0))
    u = cw_ref[0:1, :] * up + cw_ref[1:2, :] * xin + cw_ref[2:3, :] * dn + cb_ref[...]
    u = u * jax.nn.sigmoid(u)
    xs = u[:, 0:GROUP_WIDTH]

    mask = _causal_mask(direction)
    end = CHUNK - 1 if direction == 0 else 0
    low1 = _lane_is_low((1, LANES))

    dt = _softplus(g_ref[...] + gbias_ref[...])
    dta = dt * (-jnp.exp(alog_ref[...]))
    cum_col = _dot_f32(mask.astype(F32), dta)
    cum_row = cum_col.T
    dt_row = dt.T

    outs = []
    for grp in range(2):
        bm = u[:, GROUP_WIDTH + grp * D_STATE:GROUP_WIDTH + (grp + 1) * D_STATE].astype(MXU_DTYPE)
        cm = u[:, 2 * GROUP_WIDTH + grp * D_STATE:2 * GROUP_WIDTH + (grp + 1) * D_STATE].astype(MXU_DTYPE)
        x_pair = xs[:, grp * LANES:(grp + 1) * LANES]
        xb = x_pair.astype(MXU_DTYPE)
        gmat = _dot_nt(cm, bm)
        ch = _dot(cm, st_ref[grp].astype(MXU_DTYPE))
        per_head = []
        for half in range(2):
            col = 16 + 4 * direction + 2 * grp + half
            cc = cum_col[:, col:col + 1]
            cr = cum_row[col:col + 1, :]
            tot = cr[:, end:end + 1]
            decay = jnp.exp(jnp.where(mask, cc - cr, -jnp.inf))
            s = gmat * decay * dt_row[col:col + 1, :]
            y = _dot(s.astype(MXU_DTYPE), xb)
            per_head.append((y, jnp.exp(cc), jnp.exp(tot - cc) * dt[:, col:col + 1], jnp.exp(tot)))
        (y0, e0, w0, d0), (y1, e1, w1, d1) = per_head
        outs.append(jnp.where(low1, y0 + e0 * ch, y1 + e1 * ch))
        xw = (x_pair * jnp.where(low1, w0, w1)).astype(MXU_DTYPE)
        st_ref[grp] = st_ref[grp] * jnp.where(low1, d0, d1) + _dot_tn(bm, xw)

    if direction == 0:
        for grp in range(2):
            out_ref[:, grp * LANES:(grp + 1) * LANES] = outs[grp]
    else:
        ys = []
        for grp in range(2):
            sl = slice(grp * LANES, (grp + 1) * LANES)
            z = z_ref[:, sl]
            y = yf_ref[:, sl] + outs[grp] + dskip_ref[:, sl] * xs[:, sl]
            ys.append(y * (z * jax.nn.sigmoid(z)))
        ms = (jnp.sum(ys[0] * ys[0], axis=-1, keepdims=True)
              + jnp.sum(ys[1] * ys[1], axis=-1, keepdims=True)) * (1.0 / GROUP_WIDTH)
        rs = lax.rsqrt(ms + EPS)
        for grp in range(2):
            sl = slice(grp * LANES, (grp + 1) * LANES)
            out_ref[:, sl] = (ys[grp] * rs * gs_ref[:, sl]).astype(out_ref.dtype)


def _ssd(pf, gbias, alog, conv_w, conv_b, d_skip, g_ssm, n_ctx):
    bsz, u, _ = pf.shape
    nc = u // CHUNK
    nc_ctx = n_ctx // CHUNK
    gate_blk = (_N_F - LANES) // LANES
    z_blk = (768 + GROUP_WIDTH) // GROUP_WIDTH
    hpc = CHUNK // HALO
    last_halo = u // HALO - 1

    def call(direction, extra_in, extra_specs, out_dtype):
        chunk = lambda b, j: _scan_chunk(j, direction, nc_ctx, nc)
        kern = functools.partial(_ssd_kernel, direction=direction, nc_ctx=nc_ctx, nc=nc)
        const = lambda b, j: (0, 0)
        return pl.pallas_call(
            kern,
            grid=(bsz, nc),
            in_specs=[
                pl.BlockSpec((None, CHUNK, 768), lambda b, j: (b, chunk(b, j), 0)),
                pl.BlockSpec((None, HALO, 768), lambda b, j: (b, jnp.maximum(chunk(b, j) * hpc - 1, 0), 0)),
                pl.BlockSpec((None, HALO, 768),
                             lambda b, j: (b, jnp.minimum((chunk(b, j) + 1) * hpc, last_halo), 0)),
                pl.BlockSpec((None, CHUNK, LANES), lambda b, j: (b, chunk(b, j), gate_blk)),
                pl.BlockSpec((1, LANES), const),
                pl.BlockSpec((1, LANES), const),
                pl.BlockSpec((3, 768), const),
                pl.BlockSpec((1, 768), const),
            ] + extra_specs(chunk),
            out_specs=pl.BlockSpec((None, CHUNK, GROUP_WIDTH), lambda b, j: (b, chunk(b, j), 0)),
            out_shape=jax.ShapeDtypeStruct((bsz, u, GROUP_WIDTH), out_dtype),
            scratch_shapes=[pltpu.VMEM((2, D_STATE, LANES), F32)],
            compiler_params=_params("parallel", "arbitrary"),
            name="ssd_fwd" if direction == 0 else "ssd_bwd",
        )(pf, pf, pf, pf, gbias, alog, conv_w, conv_b, *extra_in)

    yf = call(0, (), lambda chunk: [], F32)
    bwd_specs = lambda chunk: [
        pl.BlockSpec((None, CHUNK, GROUP_WIDTH), lambda b, j: (b, chunk(b, j), 0)),
        pl.BlockSpec((None, CHUNK, GROUP_WIDTH), lambda b, j: (b, chunk(b, j), z_blk)),
        pl.BlockSpec((1, GROUP_WIDTH), lambda b, j: (0, 0)),
        pl.BlockSpec((1, GROUP_WIDTH), lambda b, j: (0, 0)),
    ]
    return call(1, (yf, pf, d_skip, g_ssm), bwd_specs, MXU_DTYPE)


def _out_ffn_kernel(x_ref, ya_ref, yb_ref, ym_ref, yd_ref, mod_ref, g2_ref, wo_ref, w1_ref, w2_ref,
                    gfin_ref, out_ref, *, final):
    y = (_dot(ya_ref[...], wo_ref[0:GROUP_WIDTH, :])
         + _dot(yb_ref[...], wo_ref[GROUP_WIDTH:2 * GROUP_WIDTH, :])
         + _dot(ym_ref[...], wo_ref[2 * GROUP_WIDTH:3 * GROUP_WIDTH, :])
         + _dot(yd_ref[...], wo_ref[3 * GROUP_WIDTH:, :]))
    x1 = x_ref[...] + mod_ref[2:3, :] * y
    h2 = _rmsnorm_mod(x1, g2_ref[...], mod_ref[3:4, :], mod_ref[4:5, :]).astype(MXU_DTYPE)
    acc = jnp.zeros(x1.shape, F32)
    for f in range(D_FF // FF_CHUNK):
        sl = slice(f * FF_CHUNK, (f + 1) * FF_CHUNK)
        hf = jnp.maximum(_dot(h2, w1_ref[:, sl]), 0.0)
        acc = acc + _dot((hf * hf).astype(MXU_DTYPE), w2_ref[sl, :])
    x2 = x1 + mod_ref[5:6, :] * acc
    if final:
        ms = jnp.mean(x2 * x2, axis=-1, keepdims=True)
        x2 = x2 * lax.rsqrt(ms + EPS) * gfin_ref[...]
    out_ref[...] = x2


def _out_ffn(xc, ya, yb, ym, yd, modv, g2, wo, w1, w2, g_final, n_ctx, final):
    bsz, u, d = xc.shape
    q_off = n_ctx // ROW_TILE if final else 0
    nt = u // ROW_TILE - q_off
    row = lambda b, i: (b, i + q_off, 0)
    const = lambda b, i: (0, 0)
    resident = pl.Buffered(1)
    return pl.pallas_call(
        functools.partial(_out_ffn_kernel, final=final),
        grid=(bsz, nt),
        in_specs=[
            pl.BlockSpec((None, ROW_TILE, d), row),
            pl.BlockSpec((None, ROW_TILE, GROUP_WIDTH), row),
            pl.BlockSpec((None, ROW_TILE, GROUP_WIDTH), row),
            pl.BlockSpec((None, ROW_TILE, GROUP_WIDTH), row),
            pl.BlockSpec((None, ROW_TILE, GROUP_WIDTH), row),
            pl.BlockSpec((None, None, 8, d), lambda b, i: (b, jnp.minimum(i + q_off, 1), 0, 0)),
            pl.BlockSpec((1, d), const),
            pl.BlockSpec((d, d), const, pipeline_mode=resident),
            pl.BlockSpec((d, D_FF), const, pipeline_mode=resident),
            pl.BlockSpec((D_FF, d), const, pipeline_mode=resident),
            pl.BlockSpec((1, d), const),
        ],
        out_specs=pl.BlockSpec((None, ROW_TILE, d), lambda b, i: (b, i, 0)),
        out_shape=jax.ShapeDtypeStruct((bsz, nt * ROW_TILE, d), F32),
        compiler_params=_params("parallel", "parallel"),
        name="out_ffn_final" if final else "out_ffn",
    )(xc, ya, yb, ym, yd, modv, g2, wo, w1, w2, g_final)


def _proj_columns():
    (aq, ak, av, bq, bk, bv, cq, ck, cv, co, cg, dx, dz, db, dc, ddt) = [int(o) for o in _OFF[:-1]]

    def rng(start, n):
        return list(range(start, start + n))

    def qcols(base):
        return [base + h * HEAD_DIM + i for h in _Q_HEAD_ORDER for i in range(HEAD_DIM)]

    cols = (qcols(aq) + rng(ak, 128) + rng(av, 128)
            + qcols(bq) + rng(bk, 128) + rng(bv, 128)
            + rng(cq, 768)
            + rng(dx, 256) + rng(db, 256) + rng(dc, 256)
            + rng(co, 256) + rng(dz, 256)
            + rng(cg, 16) + rng(ddt, 8) + [-1] * (LANES - 24))
    cols = np.asarray(cols, np.int64)
    scale = np.ones(cols.shape, np.float32)
    scale[_COL_A:_COL_A + GROUP_WIDTH] = HEAD_DIM ** -0.5
    scale[_COL_C:_COL_C + GROUP_WIDTH] = HEAD_DIM ** -0.5
    scale[cols < 0] = 0.0
    return np.maximum(cols, 0), scale


def _rope_tables(n_tok, n_ctx):
    t = jnp.arange(n_tok)
    row = (t // GRID_W).astype(F32)
    col = (t % GRID_W).astype(F32)
    half = HEAD_DIM // 2
    inv_freq = ROPE_BASE ** (-jnp.arange(0, half, 2, dtype=F32) / half)
    lane = np.arange(LANES)
    hd = lane % HEAD_DIM
    use_col = (hd // half) == 1
    pos = jnp.where(use_col[None, :], col[:, None], row[:, None])
    ang = pos * inv_freq[hd % (half // 2)][None, :]
    first = (hd % half) < (half // 2)
    cos = jnp.cos(ang)
    sin = jnp.where(first[None, :], -jnp.sin(ang), jnp.sin(ang))
    cos = jnp.concatenate([jnp.ones((n_ctx, LANES), F32), cos], axis=0)
    sin = jnp.concatenate([jnp.zeros((n_ctx, LANES), F32), sin], axis=0)
    return cos, sin


def _pad_lanes(v, offset):
    out = jnp.zeros((1, LANES), F32)
    return lax.dynamic_update_slice(out, v.reshape(1, -1).astype(F32), (0, offset))


def kernel(x, c, ctx, c_ctx, w_ada, b_ada, g_norm1, g_norm2, w_in, sink_a, g_q_b, g_k_b, b_igate, b_fgate,
           g_mlstm, conv_w, conv_b, a_log, dt_bias, d_skip, g_ssm, w_out, w_ff1, w_ff2, g_final):
    bsz, n_tok, d = x.shape
    n_ctx = ctx.shape[1]
    depth = w_in.shape[0]
    assert d == D_MODEL and n_ctx % ROW_TILE == 0 and n_tok % ROW_TILE == 0
    assert ((n_tok + n_ctx) // 2) % 16 == 0

    cos, sin = _rope_tables(n_tok, n_ctx)
    cols, col_scale = _proj_columns()
    q_perm = np.asarray([h * HEAD_DIM + i for h in _Q_HEAD_ORDER for i in range(HEAD_DIM)])
    wo_rows = np.concatenate([q_perm, GROUP_WIDTH + q_perm, np.arange(2 * GROUP_WIDTH, 4 * GROUP_WIDTH)])

    n_rows = 16
    cvec = jnp.concatenate([c, c_ctx[None, :], jnp.zeros((n_rows - bsz - 1, d), F32)], axis=0)
    mod_all = _ada_mod(cvec, w_ada, b_ada)

    xc = jnp.concatenate([ctx, x], axis=1)
    out = None
    for layer in range(depth):
        need_ctx = layer < depth - 1
        mod = mod_all[layer].reshape(n_rows, 6, d)
        pad = jnp.zeros((bsz, 2, d), F32)
        mod_lat = jnp.concatenate([mod[:bsz], pad], axis=1)
        mod_ctx = jnp.broadcast_to(jnp.concatenate([mod[bsz], pad[0]], axis=0), (bsz, 8, d))
        modv = jnp.stack([mod_ctx, mod_lat], axis=1)

        w = (jnp.take(w_in[layer], cols, axis=1) * col_scale[None, :]).astype(MXU_DTYPE)
        gq = jnp.tile(g_q_b[layer] * (HEAD_DIM ** -0.5), 2).reshape(1, LANES)
        gk = jnp.tile(g_k_b[layer], 2).reshape(1, LANES)
        pa, pb, pc, pf = _in_proj(xc, modv, g_norm1[layer].reshape(1, d), w, cos, sin, gq, gk)

        ya = _attn_a(pa, sink_a[layer], n_ctx, need_ctx)
        yb = _attn_b(pb, n_ctx, need_ctx)

        gate_bias = jnp.concatenate([b_igate[layer, 0], b_fgate[layer, 0], b_igate[layer, 1], b_fgate[layer, 1],
                                     dt_bias[layer, 0], dt_bias[layer, 1]])
        gbias = _pad_lanes(gate_bias, 0)
        ym = _mlstm(pc, pf, gbias, g_mlstm[layer].reshape(1, GROUP_WIDTH), n_ctx)

        alog = _pad_lanes(a_log[layer].reshape(-1), 16)
        dsk = jnp.repeat(d_skip[layer], HEAD_DIM).reshape(1, GROUP_WIDTH)
        yd = _ssd(pf, gbias, alog, conv_w[layer], conv_b[layer].reshape(1, -1), dsk,
                  g_ssm[layer].reshape(1, GROUP_WIDTH), n_ctx)

        wo = jnp.take(w_out[layer], wo_rows, axis=0).astype(MXU_DTYPE)
        out = _out_ffn(xc, ya, yb, ym, yd, modv, g_norm2[layer].reshape(1, d), wo,
                       w_ff1[layer].astype(MXU_DTYPE), w_ff2[layer].astype(MXU_DTYPE),
                       g_final.reshape(1, d), n_ctx, final=not need_ctx)
        xc = out
    return out
```

```python
import functools
import math

import numpy as np
import jax
import jax.numpy as jnp
from jax import lax
from jax.experimental import pallas as pl
from jax.experimental.pallas import tpu as pltpu

F32 = jnp.float32
MXU_DTYPE = jnp.bfloat16

D_MODEL = 1024
HEAD_DIM = 64
LANES = 128
GRID_W = 64
WINDOW = 128
ROPE_BASE = 10000.0
EPS = 1e-6
N_HEADS = 4
GROUP_WIDTH = N_HEADS * HEAD_DIM
D_STATE = 128
CHUNK = 128
ROW_TILE = 256
CHUNKS_PER_TILE = ROW_TILE // CHUNK
KEY_TILE = 1024
D_FF = 4 * D_MODEL
FF_CHUNK = 1024
HALO = 8
VMEM_LIMIT = 56 * 1024 * 1024
LOG2E = math.log2(math.e)

_SPLIT_SIZES = (256, 128, 128, 256, 128, 128, 256, 256, 256, 256, 16, 256, 256, 256, 256, 8)
_OFF = [int(o) for o in np.concatenate([[0], np.cumsum(_SPLIT_SIZES)])]
_Q_HEAD_ORDER = (0, 2, 1, 3)

_COL_A = 0
_COL_B = 512
_COL_C = 1024
_COL_F = 1792
_N_F = 768 + 256 + 256 + 128
_N_PROJ = _COL_F + _N_F
_GATE_BLK = (_N_F - LANES) // LANES
_O_BLK = 768 // GROUP_WIDTH
_Z_BLK = (768 + GROUP_WIDTH) // GROUP_WIDTH


def _dot(a, b):
    return jnp.dot(a, b, preferred_element_type=F32)


def _dot_nt(a, b):
    return lax.dot_general(a, b, (((1,), (1,)), ((), ())), preferred_element_type=F32)


def _dot_tn(a, b):
    return lax.dot_general(a, b, (((0,), (0,)), ((), ())), preferred_element_type=F32)


def _split3(x):
    hi = x.astype(MXU_DTYPE)
    r1 = x - hi.astype(F32)
    mid = r1.astype(MXU_DTYPE)
    lo = (r1 - mid.astype(F32)).astype(MXU_DTYPE)
    return [hi, mid, lo]


def _rows_cumsum(rows, mt3):
    return _dot(jnp.concatenate(_split3(rows), axis=1), mt3)


def _rows_to_columns(rows, sel3):
    return _dot_tn(jnp.concatenate(_split3(rows), axis=0), sel3)


def _cummax_lanes(x, direction):
    n = x.shape[-1]
    lane = lax.broadcasted_iota(jnp.int32, x.shape, 1)
    shift = 1
    while shift < n:
        if direction == 0:
            moved = jnp.where(lane >= shift, pltpu.roll(x, shift, 1), -jnp.inf)
        else:
            moved = jnp.where(lane < n - shift, pltpu.roll(x, n - shift, 1), -jnp.inf)
        x = jnp.maximum(x, moved)
        shift *= 2
    return x


def _lane_is_low(shape):
    lane = lax.broadcasted_iota(jnp.int32, shape, len(shape) - 1)
    return (lane % LANES) < HEAD_DIM


def _head_mean_square(t, low):
    sq = t * t
    lo = jnp.sum(jnp.where(low, sq, 0.0), axis=-1, keepdims=True)
    hi = jnp.sum(jnp.where(low, 0.0, sq), axis=-1, keepdims=True)
    return jnp.where(low, lo, hi) * (1.0 / HEAD_DIM)


def _params(*sem):
    return pltpu.CompilerParams(dimension_semantics=sem, vmem_limit_bytes=VMEM_LIMIT)


_RESIDENT = pl.Buffered(1)


def _ada_kernel(c_ref, w_ref, b_ref, out_ref):
    cv = c_ref[...]
    cv = cv * jax.nn.sigmoid(cv)
    out_ref[...] = _dot(cv.astype(MXU_DTYPE), w_ref[...].astype(MXU_DTYPE)) + b_ref[...]


def _ada_mod(cvec, w_ada, b_ada):
    depth, d, n = w_ada.shape
    rows = cvec.shape[0]
    tn = 1536
    return pl.pallas_call(
        _ada_kernel,
        grid=(depth, n // tn),
        in_specs=[
            pl.BlockSpec((rows, d), lambda l, j: (0, 0)),
            pl.BlockSpec((None, d, tn), lambda l, j: (l, 0, j)),
            pl.BlockSpec((None, 1, tn), lambda l, j: (l, 0, j)),
        ],
        out_specs=pl.BlockSpec((None, rows, tn), lambda l, j: (l, 0, j)),
        out_shape=jax.ShapeDtypeStruct((depth, rows, n), F32),
        compiler_params=_params("parallel", "parallel"),
        name="ada_mod",
    )(cvec, w_ada, b_ada.reshape(depth, 1, n))


def _rmsnorm_mod(x, g, shift, scale):
    ms = jnp.mean(x * x, axis=-1, keepdims=True)
    return (x * lax.rsqrt(ms + EPS) * g) * (1.0 + scale) + shift


def _in_proj_kernel(x_ref, mod_ref, g1_ref, w_ref, cos_ref, sin_ref, gq_ref, gk_ref,
                    a_ref, b_ref, c_ref, f_ref):
    hn = _rmsnorm_mod(x_ref[...], g1_ref[...], mod_ref[0:1, :], mod_ref[1:2, :])
    hb = hn.astype(MXU_DTYPE)
    cos = cos_ref[...]
    sin = sin_ref[...]
    lane = lax.broadcasted_iota(jnp.int32, (1, LANES), 1)
    first = (lane % 32) < 16
    low = _lane_is_low((1, LANES))

    def rope(t):
        partner = jnp.where(first, pltpu.roll(t, LANES - 16, 1), pltpu.roll(t, 16, 1))
        return t * cos + partner * sin

    pa = _dot(hb, w_ref[:, _COL_A:_COL_A + 512])
    for j in range(3):
        a_ref[:, j * LANES:(j + 1) * LANES] = rope(pa[:, j * LANES:(j + 1) * LANES]).astype(a_ref.dtype)
    a_ref[:, 3 * LANES:] = pa[:, 3 * LANES:].astype(a_ref.dtype)

    pb = _dot(hb, w_ref[:, _COL_B:_COL_B + 512])
    for j in range(3):
        t = pb[:, j * LANES:(j + 1) * LANES]
        g = gq_ref[...] if j < 2 else gk_ref[...]
        t = t * lax.rsqrt(_head_mean_square(t, low) + EPS) * g
        b_ref[:, j * LANES:(j + 1) * LANES] = rope(t).astype(b_ref.dtype)
    b_ref[:, 3 * LANES:] = pb[:, 3 * LANES:].astype(b_ref.dtype)

    c_ref[...] = _dot(hb, w_ref[:, _COL_C:_COL_F]).astype(c_ref.dtype)
    f_ref[...] = _dot(hb, w_ref[:, _COL_F:])


def _in_proj(xc, modv, g1, w, cos, sin, gq, gk):
    bsz, u, d = xc.shape
    nt = u // ROW_TILE
    row = lambda b, i: (b, i, 0)
    const = lambda b, i: (0, 0)
    return pl.pallas_call(
        _in_proj_kernel,
        grid=(bsz, nt),
        in_specs=[
            pl.BlockSpec((None, ROW_TILE, d), row),
            pl.BlockSpec((None, None, 8, d), lambda b, i: (b, jnp.minimum(i, 1), 0, 0)),
            pl.BlockSpec((1, d), const),
            pl.BlockSpec((d, _N_PROJ), const, pipeline_mode=_RESIDENT),
            pl.BlockSpec((ROW_TILE, LANES), lambda b, i: (i, 0)),
            pl.BlockSpec((ROW_TILE, LANES), lambda b, i: (i, 0)),
            pl.BlockSpec((1, LANES), const),
            pl.BlockSpec((1, LANES), const),
        ],
        out_specs=[
            pl.BlockSpec((None, ROW_TILE, 512), row),
            pl.BlockSpec((None, ROW_TILE, 512), row),
            pl.BlockSpec((None, ROW_TILE, 768), row),
            pl.BlockSpec((None, ROW_TILE, _N_F), row),
        ],
        out_shape=[
            jax.ShapeDtypeStruct((bsz, u, 512), MXU_DTYPE),
            jax.ShapeDtypeStruct((bsz, u, 512), MXU_DTYPE),
            jax.ShapeDtypeStruct((bsz, u, 768), MXU_DTYPE),
            jax.ShapeDtypeStruct((bsz, u, _N_F), F32),
        ],
        compiler_params=_params("parallel", "parallel"),
        name="in_proj",
    )(xc, modv, g1, w, cos, sin, gq, gk)


def _stack_heads(q_ref, qs_ref):
    low = _lane_is_low((1, LANES))
    for jb in range(2):
        q = q_ref[:, jb * LANES:(jb + 1) * LANES]
        for half in range(2):
            idx = 2 * jb + half
            keep = low if half == 0 else jnp.logical_not(low)
            qs_ref[idx * ROW_TILE:(idx + 1) * ROW_TILE, :] = jnp.where(keep, q, jnp.zeros_like(q))


def _unstack_heads(o, out_ref):
    low = _lane_is_low((1, LANES))
    for jb in range(2):
        lo = o[(2 * jb) * ROW_TILE:(2 * jb + 1) * ROW_TILE, :]
        hi = o[(2 * jb + 1) * ROW_TILE:(2 * jb + 2) * ROW_TILE, :]
        out_ref[:, jb * LANES:(jb + 1) * LANES] = jnp.where(low, lo, hi).astype(out_ref.dtype)


def _attn_a_kernel(sink_ref, q_ref, kp_ref, ko_ref, kn_ref, kc_ref, vp_ref, vo_ref, vn_ref, vc_ref,
                   out_ref, qs_ref, kbuf, vbuf, *, n_tok, n_ctx, q_off):
    i = pl.program_id(1) + q_off
    nb = 2 * WINDOW + ROW_TILE
    nk = nb + n_ctx
    kbuf[0:WINDOW] = kp_ref[...]
    kbuf[WINDOW:WINDOW + ROW_TILE] = ko_ref[...]
    kbuf[WINDOW + ROW_TILE:nb] = kn_ref[...]
    kbuf[nb:nk] = kc_ref[...]
    vbuf[0:WINDOW] = vp_ref[...]
    vbuf[WINDOW:WINDOW + ROW_TILE] = vo_ref[...]
    vbuf[WINDOW + ROW_TILE:nb] = vn_ref[...]
    vbuf[nb:nk] = vc_ref[...]
    _stack_heads(q_ref, qs_ref)

    r = lax.broadcasted_iota(jnp.int32, (ROW_TILE, nk), 0)
    c = lax.broadcasted_iota(jnp.int32, (ROW_TILE, nk), 1)
    kpos = (i - 1) * ROW_TILE - WINDOW + c
    dist = c - r
    band = jnp.where(dist >= 0, jnp.where(dist <= 2 * WINDOW, 1, 0), 0)
    band = jnp.where(kpos >= 0, jnp.where(kpos < n_tok, band, 0), 0)
    band = jnp.where(i >= 1, band, 0)
    bias = jnp.where(c >= nb, 0.0, jnp.where(band > 0, 0.0, -jnp.inf)).astype(F32)

    hrow = lax.broadcasted_iota(jnp.int32, (N_HEADS * ROW_TILE, 1), 0) // ROW_TILE
    sink = jnp.zeros((N_HEADS * ROW_TILE, 1), F32)
    for idx in range(N_HEADS):
        sink = jnp.where(hrow == idx, sink_ref[_Q_HEAD_ORDER[idx]], sink)

    s = _dot_nt(qs_ref[...], kbuf[...])
    s = (s.reshape(N_HEADS, ROW_TILE, nk) + bias[None]).reshape(N_HEADS * ROW_TILE, nk)
    m = jnp.maximum(jnp.max(s, axis=-1, keepdims=True), sink)
    p = jnp.exp(s - m)
    l = jnp.sum(p, axis=-1, keepdims=True) + jnp.exp(sink - m)
    o = _dot(p.astype(MXU_DTYPE), vbuf[...]) / l
    _unstack_heads(o, out_ref)


def _attn_a(pa, sink, n_ctx, need_ctx):
    bsz, u, _ = pa.shape
    n_tok = u - n_ctx
    q_off = 0 if need_ctx else n_ctx // ROW_TILE
    nt = u // ROW_TILE - q_off
    last_blk = u // WINDOW - 1
    rpw = ROW_TILE // WINDOW
    nk = 2 * WINDOW + ROW_TILE + n_ctx

    def own(col):
        return lambda b, i: (b, i + q_off, col)

    def prev(col):
        return lambda b, i: (b, jnp.maximum((i + q_off) * rpw - 1, 0), col)

    def nxt(col):
        return lambda b, i: (b, jnp.minimum((i + q_off + 1) * rpw, last_blk), col)

    def ctx(col):
        return lambda b, i: (b, 0, col)

    kern = functools.partial(_attn_a_kernel, n_tok=n_tok, n_ctx=n_ctx, q_off=q_off)
    return pl.pallas_call(
        kern,
        grid=(bsz, nt),
        in_specs=[
            pl.BlockSpec(memory_space=pltpu.SMEM),
            pl.BlockSpec((None, ROW_TILE, 2 * LANES), own(0)),
            pl.BlockSpec((None, WINDOW, LANES), prev(2)),
            pl.BlockSpec((None, ROW_TILE, LANES), own(2)),
            pl.BlockSpec((None, WINDOW, LANES), nxt(2)),
            pl.BlockSpec((None, n_ctx, LANES), ctx(2)),
            pl.BlockSpec((None, WINDOW, LANES), prev(3)),
            pl.BlockSpec((None, ROW_TILE, LANES), own(3)),
            pl.BlockSpec((None, WINDOW, LANES), nxt(3)),
            pl.BlockSpec((None, n_ctx, LANES), ctx(3)),
        ],
        out_specs=pl.BlockSpec((None, ROW_TILE, 2 * LANES), lambda b, i: (b, i + q_off, 0)),
        out_shape=jax.ShapeDtypeStruct((bsz, u, 2 * LANES), MXU_DTYPE),
        scratch_shapes=[pltpu.VMEM((N_HEADS * ROW_TILE, LANES), MXU_DTYPE),
                        pltpu.VMEM((nk, LANES), MXU_DTYPE), pltpu.VMEM((nk, LANES), MXU_DTYPE)],
        compiler_params=_params("parallel", "parallel"),
        name="attn_window",
    )(sink, pa, pa, pa, pa, pa, pa, pa, pa, pa)


def _attn_b_kernel(q_ref, k_ref, v_ref, out_ref, qs_ref, m_ref, l_ref, acc_ref, *, n_ctx, n_tok, q_off, tk):
    qi = pl.program_id(1) + q_off
    _stack_heads(q_ref, qs_ref)

    def key_tile(start, size, first):
        k = k_ref[start:start + size, :]
        v = v_ref[start:start + size, :]
        s = _dot_nt(qs_ref[...], k)
        row_max = jnp.max(s, axis=-1, keepdims=True)
        if first:
            m_new = row_max
            p = jnp.exp2(s - m_new)
            l_ref[...] = jnp.sum(p, axis=-1, keepdims=True)
            acc_ref[...] = _dot(p.astype(MXU_DTYPE), v)
        else:
            m_prev = m_ref[...]
            m_new = jnp.maximum(m_prev, row_max)
            alpha = jnp.exp2(m_prev - m_new)
            p = jnp.exp2(s - m_new)
            l_ref[...] = alpha * l_ref[...] + jnp.sum(p, axis=-1, keepdims=True)
            acc_ref[...] = alpha * acc_ref[...] + _dot(p.astype(MXU_DTYPE), v)
        m_ref[...] = m_new

    key_tile(0, n_ctx, True)

    def latent_keys():
        for t in range(n_tok // tk):
            key_tile(n_ctx + t * tk, tk, False)

    if q_off == 0:
        pl.when(qi > 0)(latent_keys)
    else:
        latent_keys()
    _unstack_heads(acc_ref[...] / l_ref[...], out_ref)


def _attn_b(pb, n_ctx, need_ctx):
    bsz, u, _ = pb.shape
    n_tok = u - n_ctx
    q_off = 0 if need_ctx else n_ctx // ROW_TILE
    nt = u // ROW_TILE - q_off
    tk = min(KEY_TILE, n_tok)
    assert n_tok % tk == 0
    kern = functools.partial(_attn_b_kernel, n_ctx=n_ctx, n_tok=n_tok, q_off=q_off, tk=tk)
    return pl.pallas_call(
        kern,
        grid=(bsz, nt),
        in_specs=[
            pl.BlockSpec((None, ROW_TILE, 2 * LANES), lambda b, i: (b, i + q_off, 0)),
            pl.BlockSpec((None, u, LANES), lambda b, i: (b, 0, 2)),
            pl.BlockSpec((None, u, LANES), lambda b, i: (b, 0, 3)),
        ],
        out_specs=pl.BlockSpec((None, ROW_TILE, 2 * LANES), lambda b, i: (b, i + q_off, 0)),
        out_shape=jax.ShapeDtypeStruct((bsz, u, 2 * LANES), MXU_DTYPE),
        scratch_shapes=[
            pltpu.VMEM((N_HEADS * ROW_TILE, LANES), MXU_DTYPE),
            pltpu.VMEM((N_HEADS * ROW_TILE, 1), F32),
            pltpu.VMEM((N_HEADS * ROW_TILE, 1), F32),
            pltpu.VMEM((N_HEADS * ROW_TILE, LANES), F32),
        ],
        compiler_params=_params("parallel", "arbitrary"),
        name="attn_dense",
    )(pb, pb, pb)


def _scan_tile(j, direction, nt_ctx, nt):
    if direction == 0:
        return j
    return jnp.where(j < nt_ctx, nt_ctx - 1 - j, nt + nt_ctx - 1 - j)


def _causal_mask(direction):
    t = lax.broadcasted_iota(jnp.int32, (CHUNK, CHUNK), 0)
    s = lax.broadcasted_iota(jnp.int32, (CHUNK, CHUNK), 1)
    return (s <= t) if direction == 0 else (s >= t)


def _chunk_order(direction):
    order = range(CHUNKS_PER_TILE)
    return order if direction == 0 else reversed(order)


def _log_sigmoid(x):
    return jnp.minimum(x, 0.0) - jnp.log1p(jnp.exp(-jnp.abs(x)))


def _softplus(x):
    return jnp.maximum(x, 0.0) + jnp.log1p(jnp.exp(-jnp.abs(x)))


def _mlstm_gates(direction, g_rows, m_prev, mt3):
    end = CHUNK - 1 if direction == 0 else 0
    i_r = jnp.concatenate([g[16 * direction:16 * direction + 8] for g in g_rows], axis=0)
    f_r = jnp.concatenate([g[16 * direction + 8:16 * direction + 16] for g in g_rows], axis=0)
    b_r = _rows_cumsum(_log_sigmoid(f_r), mt3)
    r_r = i_r - b_r
    cmax = _cummax_lanes(r_r, direction)
    tot = jnp.broadcast_to(b_r[:, end:end + 1], b_r.shape)
    g2 = tot - b_r + i_r
    g2max = jnp.broadcast_to(jnp.max(g2, axis=-1, keepdims=True), g2.shape)
    per_chunk = []
    for idx in range(len(g_rows)):
        sl = slice(8 * idx, 8 * idx + 8)
        c_r = jnp.maximum(m_prev, cmax[sl])
        dp_r = jnp.exp(m_prev - c_r)
        em_r = jnp.exp(-(b_r[sl] + c_r))
        m_new = jnp.maximum(tot[sl] + m_prev, g2max[sl])
        wk_r = jnp.exp(g2[sl] - m_new)
        cd_r = jnp.exp(tot[sl] + m_prev - m_new)
        per_chunk.append((jnp.concatenate([c_r, dp_r, em_r, wk_r], axis=0), r_r[sl], cd_r))
        m_prev = m_new
    return per_chunk, m_prev


def _mlstm_chunk(direction, qkv, r_r, cd_r, cols, state):
    mask = _causal_mask(direction)
    low1 = _lane_is_low((1, LANES))
    low2 = _lane_is_low((1, 2 * LANES))

    ones = jnp.ones((CHUNK, LANES), MXU_DTYPE)
    r2 = lax.broadcasted_iota(jnp.int32, (LANES, 2 * LANES), 0)
    c2 = lax.broadcasted_iota(jnp.int32, (LANES, 2 * LANES), 1)
    state_mask = (r2 < HEAD_DIM) == ((c2 % LANES) < HEAD_DIM)

    outs, new_state = [], []
    for pair in range(2):
        q = qkv[:, pair * LANES:(pair + 1) * LANES]
        k = qkv[:, GROUP_WIDTH + pair * LANES:GROUP_WIDTH + (pair + 1) * LANES]
        v = qkv[:, 2 * GROUP_WIDTH + pair * LANES:2 * GROUP_WIDTH + (pair + 1) * LANES]
        vext = jnp.concatenate([v, ones], axis=1)
        prior = _dot(q, state[pair].astype(MXU_DTYPE))
        svs, dens = [], []
        for half in range(2):
            h = 2 * pair + half
            w = jnp.exp(jnp.where(mask, r_r[h:h + 1, :] - cols[:, h * LANES:(h + 1) * LANES], -jnp.inf))
            qm = jnp.where(low1 if half == 0 else jnp.logical_not(low1), q, jnp.zeros_like(q))
            s = _dot_nt(qm, k) * w
            s_hi = s.astype(MXU_DTYPE)
            s_lo = (s - s_hi.astype(F32)).astype(MXU_DTYPE)
            sv = _dot(s_hi, vext)
            svs.append(sv[:, :LANES])
            dens.append(sv[:, LANES:] + _dot(s_lo, ones))
        dp = cols[:, (4 + pair) * LANES:(5 + pair) * LANES]
        em = cols[:, (6 + pair) * LANES:(7 + pair) * LANES]
        wk = cols[:, (8 + pair) * LANES:(9 + pair) * LANES]
        num = jnp.where(low1, svs[0], svs[1]) + dp * prior[:, :LANES]
        den = jnp.where(low1, dens[0], dens[1]) + dp * prior[:, LANES:]
        outs.append(num / jnp.maximum(jnp.abs(den), em))
        kw = (k.astype(F32) * wk).astype(MXU_DTYPE)
        upd = _dot_tn(kw, vext)
        cd0 = jnp.concatenate([cd_r[2 * pair:2 * pair + 1]] * 2, axis=1)
        cd1 = jnp.concatenate([cd_r[2 * pair + 1:2 * pair + 2]] * 2, axis=1)
        new_state.append(state[pair] * jnp.where(low2, cd0, cd1) + jnp.where(state_mask, upd, 0.0))
    return outs, new_state


def _mlstm_kernel(qkvf_ref, gf_ref, qkvb_ref, gb_ref, gbias_ref, mt3_ref, sel3_ref, hf_ref, hb_ref,
                  st_ref, m_ref):
    @pl.when(pl.program_id(1) == 0)
    def _():
        st_ref[...] = jnp.zeros(st_ref.shape, F32)
        m_ref[...] = jnp.zeros(m_ref.shape, F32)

    gbias = gbias_ref[...]
    dirs = ((qkvf_ref, gf_ref, hf_ref), (qkvb_ref, gb_ref, hb_ref))
    gates = []
    for direction, (_, g_ref, _) in enumerate(dirs):
        g_rows = [(g_ref[ci * CHUNK:(ci + 1) * CHUNK, :] + gbias).T for ci in _chunk_order(direction)]
        per_chunk, m_new = _mlstm_gates(direction, g_rows, m_ref[direction], mt3_ref[direction])
        m_ref[direction] = m_new
        gates.append(per_chunk)
    work = [(direction, idx) for idx in range(CHUNKS_PER_TILE) for direction in range(2)]
    rows_all = jnp.concatenate([gates[direction][idx][0] for direction, idx in work], axis=1)
    cols_all = _rows_to_columns(rows_all, sel3_ref[...])

    state = [[st_ref[direction, pair] for pair in range(2)] for direction in range(2)]
    for n, (direction, idx) in enumerate(work):
        qkv_ref, _, out_ref = dirs[direction]
        ci = list(_chunk_order(direction))[idx]
        rows = slice(ci * CHUNK, (ci + 1) * CHUNK)
        _, r_r, cd_r = gates[direction][idx]
        outs, state[direction] = _mlstm_chunk(direction, qkv_ref[rows, :], r_r, cd_r,
                                              cols_all[n * CHUNK:(n + 1) * CHUNK, :], state[direction])
        for pair in range(2):
            out_ref[rows, pair * LANES:(pair + 1) * LANES] = outs[pair]
    for direction in range(2):
        for pair in range(2):
            st_ref[direction, pair] = state[direction][pair]


def _scan_masks():
    s = np.arange(CHUNK)[:, None]
    t = np.arange(CHUNK)[None, :]
    mats = [np.tile((s <= t).astype(np.float32), (3, 1)), np.tile((s >= t).astype(np.float32), (3, 1))]
    return jnp.asarray(np.stack(mats), MXU_DTYPE)


def _column_selector(n_head_groups, n_pair_groups):
    n_groups = n_head_groups + n_pair_groups
    n_rows = 8 * n_groups + (-8 * n_groups) % 16
    n_cols = (N_HEADS * n_head_groups + 2 * n_pair_groups) * LANES
    sel = np.zeros((n_rows, n_cols), np.float32)
    col = 0
    for g in range(n_head_groups):
        for h in range(N_HEADS):
            sel[8 * g + h, col:col + LANES] = 1.0
            col += LANES
    for g in range(n_head_groups, n_groups):
        for pair in range(2):
            sel[8 * g + 2 * pair, col:col + HEAD_DIM] = 1.0
            sel[8 * g + 2 * pair + 1, col + HEAD_DIM:col + LANES] = 1.0
            col += LANES
    return jnp.asarray(np.tile(sel, (3, 1)), MXU_DTYPE)


def _mlstm(pc, pf, gbias, n_ctx):
    bsz, u, _ = pc.shape
    nt = u // ROW_TILE
    nt_ctx = n_ctx // ROW_TILE
    fwd = lambda b, j: _scan_tile(j, 0, nt_ctx, nt)
    bwd = lambda b, j: _scan_tile(j, 1, nt_ctx, nt)
    out = jax.ShapeDtypeStruct((bsz, u, GROUP_WIDTH), F32)
    mt3 = _scan_masks()
    sel3 = _column_selector(1, 3)
    return pl.pallas_call(
        _mlstm_kernel,
        grid=(bsz, nt),
        in_specs=[
            pl.BlockSpec((None, ROW_TILE, 768), lambda b, j: (b, fwd(b, j), 0)),
            pl.BlockSpec((None, ROW_TILE, LANES), lambda b, j: (b, fwd(b, j), _GATE_BLK)),
            pl.BlockSpec((None, ROW_TILE, 768), lambda b, j: (b, bwd(b, j), 0)),
            pl.BlockSpec((None, ROW_TILE, LANES), lambda b, j: (b, bwd(b, j), _GATE_BLK)),
            pl.BlockSpec((1, LANES), lambda b, j: (0, 0)),
            pl.BlockSpec(mt3.shape, lambda b, j: (0, 0, 0)),
            pl.BlockSpec(sel3.shape, lambda b, j: (0, 0)),
        ],
        out_specs=[
            pl.BlockSpec((None, ROW_TILE, GROUP_WIDTH), lambda b, j: (b, fwd(b, j), 0)),
            pl.BlockSpec((None, ROW_TILE, GROUP_WIDTH), lambda b, j: (b, bwd(b, j), 0)),
        ],
        out_shape=[out, out],
        scratch_shapes=[pltpu.VMEM((2, 2, LANES, 2 * LANES), F32), pltpu.VMEM((2, 8, LANES), F32)],
        compiler_params=_params("parallel", "arbitrary"),
        name="mlstm_scan",
    )(pc, pf, pc, pf, gbias, mt3, sel3)


def _ssd_conv(x_ref, xp_ref, xn_ref, cw_ref, cb_ref, tile, nt_ctx, nt):
    xin = x_ref[...]
    has_prev = (tile != 0) & (tile != nt_ctx)
    has_next = (tile != nt_ctx - 1) & (tile != nt - 1)
    prow = jnp.where(has_prev, xp_ref[HALO - 1:HALO, :], 0.0)
    nrow = jnp.where(has_next, xn_ref[0:1, :], 0.0)
    ridx = lax.broadcasted_iota(jnp.int32, (ROW_TILE, 1), 0)
    up = jnp.where(ridx == 0, prow, pltpu.roll(xin, 1, 0))
    dn = jnp.where(ridx == ROW_TILE - 1, nrow, pltpu.roll(xin, ROW_TILE - 1, 0))
    u = cw_ref[0:1, :] * up + cw_ref[1:2, :] * xin + cw_ref[2:3, :] * dn + cb_ref[...]
    return u * jax.nn.sigmoid(u)


def _ssd_gates(direction, g_rows, neg_a, mt3):
    end = CHUNK - 1 if direction == 0 else 0
    n = len(g_rows)
    dt_r = _softplus(jnp.concatenate([g[32 + 8 * direction:40 + 8 * direction] for g in g_rows], axis=0))
    cum_r = _rows_cumsum(dt_r * jnp.concatenate([neg_a] * n, axis=0), mt3)
    tot = jnp.broadcast_to(cum_r[:, end:end + 1], cum_r.shape)
    e_r = jnp.exp(cum_r)
    wend_r = jnp.exp(tot - cum_r) * dt_r
    d_r = jnp.exp(tot)
    zero = jnp.zeros((8, CHUNK), F32)
    per_chunk = []
    for idx in range(n):
        sl = slice(8 * idx, 8 * idx + 8)
        per_chunk.append((jnp.concatenate([cum_r[sl], e_r[sl], wend_r[sl], zero], axis=0),
                          cum_r[sl], dt_r[sl], d_r[sl]))
    return per_chunk


def _ssd_chunk(direction, u, cum_r, dt_r, d_r, cols, state):
    mask = _causal_mask(direction)
    low1 = _lane_is_low((1, LANES))

    outs, new_state = [], []
    for grp in range(2):
        bm = u[:, GROUP_WIDTH + grp * D_STATE:GROUP_WIDTH + (grp + 1) * D_STATE].astype(MXU_DTYPE)
        cm = u[:, 2 * GROUP_WIDTH + grp * D_STATE:2 * GROUP_WIDTH + (grp + 1) * D_STATE].astype(MXU_DTYPE)
        x_pair = u[:, grp * LANES:(grp + 1) * LANES]
        xb = x_pair.astype(MXU_DTYPE)
        gmat = _dot_nt(cm, bm)
        ch = _dot(cm, state[grp].astype(MXU_DTYPE))
        ys = []
        for half in range(2):
            h = 2 * grp + half
            decay = jnp.exp(jnp.where(mask, cols[:, h * LANES:(h + 1) * LANES] - cum_r[h:h + 1, :], -jnp.inf))
            s = gmat * decay * dt_r[h:h + 1, :]
            ys.append(_dot(s.astype(MXU_DTYPE), xb))
        e_pair = cols[:, (4 + grp) * LANES:(5 + grp) * LANES]
        w_pair = cols[:, (6 + grp) * LANES:(7 + grp) * LANES]
        outs.append(jnp.where(low1, ys[0], ys[1]) + e_pair * ch)
        xw = (x_pair * w_pair).astype(MXU_DTYPE)
        d_pair = jnp.where(low1, d_r[2 * grp:2 * grp + 1], d_r[2 * grp + 1:2 * grp + 2])
        new_state.append(state[grp] * d_pair + _dot_tn(bm, xw))
    return outs, new_state


def _ssd_kernel(xf_ref, xfp_ref, xfn_ref, gf_ref, xb_ref, xbp_ref, xbn_ref, gb_ref,
                gbias_ref, alog_ref, cw_ref, cb_ref, dskip_ref, mt3_ref, sel3_ref, yf_ref, yb_ref, st_ref,
                *, nt_ctx, nt):
    j = pl.program_id(1)

    @pl.when(j == 0)
    def _():
        st_ref[...] = jnp.zeros(st_ref.shape, F32)

    gbias = gbias_ref[...]
    dirs = ((xf_ref, xfp_ref, xfn_ref, gf_ref, yf_ref), (xb_ref, xbp_ref, xbn_ref, gb_ref, yb_ref))
    gates, convs = [], []
    for direction, (x_ref, xp_ref, xn_ref, g_ref, _) in enumerate(dirs):
        g_rows = [(g_ref[ci * CHUNK:(ci + 1) * CHUNK, :] + gbias).T for ci in _chunk_order(direction)]
        gates.append(_ssd_gates(direction, g_rows, -jnp.exp(alog_ref[direction]), mt3_ref[direction]))
        tile = _scan_tile(j, direction, nt_ctx, nt)
        convs.append(_ssd_conv(x_ref, xp_ref, xn_ref, cw_ref, cb_ref, tile, nt_ctx, nt))
    work = [(direction, idx) for idx in range(CHUNKS_PER_TILE) for direction in range(2)]
    rows_all = jnp.concatenate([gates[direction][idx][0] for direction, idx in work], axis=1)
    cols_all = _rows_to_columns(rows_all, sel3_ref[...])

    state = [[st_ref[direction, grp] for grp in range(2)] for direction in range(2)]
    for n, (direction, idx) in enumerate(work):
        out_ref = dirs[direction][4]
        ci = list(_chunk_order(direction))[idx]
        rows = slice(ci * CHUNK, (ci + 1) * CHUNK)
        u = convs[direction][rows, :]
        _, cum_r, dt_r, d_r = gates[direction][idx]
        outs, state[direction] = _ssd_chunk(direction, u, cum_r, dt_r, d_r,
                                            cols_all[n * CHUNK:(n + 1) * CHUNK, :], state[direction])
        for grp in range(2):
            sl = slice(grp * LANES, (grp + 1) * LANES)
            y = outs[grp]
            if direction == 0:
                y = y + dskip_ref[:, sl] * u[:, sl]
            out_ref[rows, sl] = y
    for direction in range(2):
        for grp in range(2):
            st_ref[direction, grp] = state[direction][grp]


def _ssd(pf, gbias, alog, conv_w, conv_b, d_skip, n_ctx):
    bsz, u, _ = pf.shape
    nt = u // ROW_TILE
    nt_ctx = n_ctx // ROW_TILE
    hpt = ROW_TILE // HALO
    last_halo = u // HALO - 1
    const = lambda b, j: (0, 0)

    def tile_specs(direction):
        tile = lambda b, j: _scan_tile(j, direction, nt_ctx, nt)
        return [
            pl.BlockSpec((None, ROW_TILE, 768), lambda b, j: (b, tile(b, j), 0)),
            pl.BlockSpec((None, HALO, 768), lambda b, j: (b, jnp.maximum(tile(b, j) * hpt - 1, 0), 0)),
            pl.BlockSpec((None, HALO, 768), lambda b, j: (b, jnp.minimum((tile(b, j) + 1) * hpt, last_halo), 0)),
            pl.BlockSpec((None, ROW_TILE, LANES), lambda b, j: (b, tile(b, j), _GATE_BLK)),
        ]

    out = jax.ShapeDtypeStruct((bsz, u, GROUP_WIDTH), F32)
    mt3 = _scan_masks()
    sel3 = _column_selector(1, 2)
    return pl.pallas_call(
        functools.partial(_ssd_kernel, nt_ctx=nt_ctx, nt=nt),
        grid=(bsz, nt),
        in_specs=tile_specs(0) + tile_specs(1) + [
            pl.BlockSpec((1, LANES), const),
            pl.BlockSpec((2, 8, LANES), lambda b, j: (0, 0, 0)),
            pl.BlockSpec((3, 768), const),
            pl.BlockSpec((1, 768), const),
            pl.BlockSpec((1, GROUP_WIDTH), const),
            pl.BlockSpec(mt3.shape, lambda b, j: (0, 0, 0)),
            pl.BlockSpec(sel3.shape, const),
        ],
        out_specs=[
            pl.BlockSpec((None, ROW_TILE, GROUP_WIDTH), lambda b, j: (b, _scan_tile(j, 0, nt_ctx, nt), 0)),
            pl.BlockSpec((None, ROW_TILE, GROUP_WIDTH), lambda b, j: (b, _scan_tile(j, 1, nt_ctx, nt), 0)),
        ],
        out_shape=[out, out],
        scratch_shapes=[pltpu.VMEM((2, 2, D_STATE, LANES), F32)],
        compiler_params=_params("parallel", "arbitrary"),
        name="ssd_scan",
    )(pf, pf, pf, pf, pf, pf, pf, pf, gbias, alog, conv_w, conv_b, d_skip, mt3, sel3)


def _out_ffn_kernel(x_ref, ya_ref, yb_ref, hf_ref, hb_ref, o_ref, yf_ref, ybw_ref, z_ref, gm_ref, gs_ref,
                    mod_ref, g2_ref, wo_ref, w1_ref, w2_ref, gfin_ref, out_ref, *, final):
    low = _lane_is_low((1, LANES))
    ym = []
    for pair in range(2):
        sl = slice(pair * LANES, (pair + 1) * LANES)
        h = hf_ref[:, sl] + hb_ref[:, sl]
        hn = h * lax.rsqrt(_head_mean_square(h, low) + EPS) * gm_ref[:, sl]
        ym.append((hn * jax.nn.sigmoid(o_ref[:, sl])).astype(MXU_DTYPE))
    ys = []
    for grp in range(2):
        sl = slice(grp * LANES, (grp + 1) * LANES)
        z = z_ref[:, sl]
        ys.append((yf_ref[:, sl] + ybw_ref[:, sl]) * (z * jax.nn.sigmoid(z)))
    ms = (jnp.sum(ys[0] * ys[0], axis=-1, keepdims=True)
          + jnp.sum(ys[1] * ys[1], axis=-1, keepdims=True)) * (1.0 / GROUP_WIDTH)
    rs = lax.rsqrt(ms + EPS)
    yd = [(ys[grp] * rs * gs_ref[:, grp * LANES:(grp + 1) * LANES]).astype(MXU_DTYPE) for grp in range(2)]

    y = (_dot(ya_ref[...], wo_ref[0:GROUP_WIDTH, :])
         + _dot(yb_ref[...], wo_ref[GROUP_WIDTH:2 * GROUP_WIDTH, :])
         + _dot(jnp.concatenate(ym, axis=1), wo_ref[2 * GROUP_WIDTH:3 * GROUP_WIDTH, :])
         + _dot(jnp.concatenate(yd, axis=1), wo_ref[3 * GROUP_WIDTH:, :]))
    x1 = x_ref[...] + mod_ref[2:3, :] * y
    h2 = _rmsnorm_mod(x1, g2_ref[...], mod_ref[3:4, :], mod_ref[4:5, :]).astype(MXU_DTYPE)
    acc = jnp.zeros(x1.shape, F32)
    for f in range(D_FF // FF_CHUNK):
        sl = slice(f * FF_CHUNK, (f + 1) * FF_CHUNK)
        hf = jnp.maximum(_dot(h2, w1_ref[:, sl]), 0.0)
        acc = acc + _dot((hf * hf).astype(MXU_DTYPE), w2_ref[sl, :])
    x2 = x1 + mod_ref[5:6, :] * acc
    if final:
        ms2 = jnp.mean(x2 * x2, axis=-1, keepdims=True)
        x2 = x2 * lax.rsqrt(ms2 + EPS) * gfin_ref[...]
    out_ref[...] = x2


def _out_ffn(xc, ya, yb, hf, hb, yf, ybw, pf, gm, gs, modv, g2, wo, w1, w2, g_final, n_ctx, final):
    bsz, u, d = xc.shape
    q_off = n_ctx // ROW_TILE if final else 0
    nt = u // ROW_TILE - q_off
    row = lambda b, i: (b, i + q_off, 0)
    const = lambda b, i: (0, 0)
    grp_tile = pl.BlockSpec((None, ROW_TILE, GROUP_WIDTH), row)
    return pl.pallas_call(
        functools.partial(_out_ffn_kernel, final=final),
        grid=(bsz, nt),
        in_specs=[
            pl.BlockSpec((None, ROW_TILE, d), row),
            grp_tile, grp_tile, grp_tile, grp_tile,
            pl.BlockSpec((None, ROW_TILE, GROUP_WIDTH), lambda b, i: (b, i + q_off, _O_BLK)),
            grp_tile, grp_tile,
            pl.BlockSpec((None, ROW_TILE, GROUP_WIDTH), lambda b, i: (b, i + q_off, _Z_BLK)),
            pl.BlockSpec((1, GROUP_WIDTH), const),
            pl.BlockSpec((1, GROUP_WIDTH), const),
            pl.BlockSpec((None, None, 8, d), lambda b, i: (b, jnp.minimum(i + q_off, 1), 0, 0)),
            pl.BlockSpec((1, d), const),
            pl.BlockSpec((d, d), const, pipeline_mode=_RESIDENT),
            pl.BlockSpec((d, D_FF), const, pipeline_mode=_RESIDENT),
            pl.BlockSpec((D_FF, d), const, pipeline_mode=_RESIDENT),
            pl.BlockSpec((1, d), const),
        ],
        out_specs=pl.BlockSpec((None, ROW_TILE, d), lambda b, i: (b, i, 0)),
        out_shape=jax.ShapeDtypeStruct((bsz, nt * ROW_TILE, d), F32),
        compiler_params=_params("parallel", "parallel"),
        name="out_ffn_final" if final else "out_ffn",
    )(xc, ya, yb, hf, hb, pf, yf, ybw, pf, gm, gs, modv, g2, wo, w1, w2, g_final)


def _gate_starts():
    cg, ddt = _OFF[10], _OFF[15]
    return [cg, cg + N_HEADS, cg + 2 * N_HEADS, cg + 3 * N_HEADS, ddt, ddt + N_HEADS]


def _gate_lanes(groups):
    pad = jnp.zeros((8 - N_HEADS,), F32)
    parts = []
    for g in groups:
        parts += [g.astype(F32), pad]
    parts.append(jnp.zeros((LANES - 8 * len(groups),), F32))
    return jnp.concatenate(parts).reshape(1, LANES)


def _proj_weight(w_in):
    (aq, ak, av, bq, bk, bv, cq, ck, cv, co, cg, dx, dz, db, dc, ddt) = _OFF[:-1]
    qscale = HEAD_DIM ** -0.5
    segs = []
    for base, scale in ((aq, qscale), (bq, 1.0)):
        segs += [(base + h * HEAD_DIM, HEAD_DIM, scale) for h in _Q_HEAD_ORDER]
        segs += [(base + GROUP_WIDTH, 2 * LANES, 1.0)]
    segs += [(cq, GROUP_WIDTH, qscale), (ck, 2 * GROUP_WIDTH, 1.0)]
    segs += [(dx, GROUP_WIDTH, 1.0), (db, 2 * GROUP_WIDTH, 1.0), (co, GROUP_WIDTH, 1.0), (dz, GROUP_WIDTH, 1.0)]
    parts = [w_in[:, s:s + n] if scale == 1.0 else w_in[:, s:s + n] * scale for s, n, scale in segs]
    zeros4 = jnp.zeros((w_in.shape[0], 8 - N_HEADS), w_in.dtype)
    for start in _gate_starts():
        parts += [w_in[:, start:start + N_HEADS], zeros4]
    parts.append(jnp.zeros((w_in.shape[0], LANES - 8 * len(_gate_starts())), w_in.dtype))
    w = jnp.concatenate(parts, axis=1)
    assert w.shape[1] == _N_PROJ
    return w.astype(MXU_DTYPE)


def _out_weight(w_out):
    parts = []
    for base in (0, GROUP_WIDTH):
        parts += [w_out[base + h * HEAD_DIM:base + (h + 1) * HEAD_DIM] for h in _Q_HEAD_ORDER]
    parts.append(w_out[2 * GROUP_WIDTH:])
    return jnp.concatenate(parts, axis=0).astype(MXU_DTYPE)


def _rope_tables(n_tok, n_ctx):
    t = jnp.arange(n_tok)
    row = (t // GRID_W).astype(F32)
    col = (t % GRID_W).astype(F32)
    half = HEAD_DIM // 2
    inv_freq = ROPE_BASE ** (-jnp.arange(0, half, 2, dtype=F32) / half)
    lane = np.arange(LANES)
    hd = lane % HEAD_DIM
    use_col = (hd // half) == 1
    pos = jnp.where(use_col[None, :], col[:, None], row[:, None])
    ang = pos * inv_freq[hd % (half // 2)][None, :]
    first = (hd % half) < (half // 2)
    cos = jnp.cos(ang)
    sin = jnp.where(first[None, :], -jnp.sin(ang), jnp.sin(ang))
    cos = jnp.concatenate([jnp.ones((n_ctx, LANES), F32), cos], axis=0)
    sin = jnp.concatenate([jnp.zeros((n_ctx, LANES), F32), sin], axis=0)
    return cos, sin


def kernel(x, c, ctx, c_ctx, w_ada, b_ada, g_norm1, g_norm2, w_in, sink_a, g_q_b, g_k_b, b_igate, b_fgate,
           g_mlstm, conv_w, conv_b, a_log, dt_bias, d_skip, g_ssm, w_out, w_ff1, w_ff2, g_final):
    bsz, n_tok, d = x.shape
    n_ctx = ctx.shape[1]
    depth = w_in.shape[0]
    assert d == D_MODEL and n_ctx % ROW_TILE == 0 and n_tok % ROW_TILE == 0

    cos, sin = _rope_tables(n_tok, n_ctx)
    n_rows = 16
    cvec = jnp.concatenate([c, c_ctx[None, :], jnp.zeros((n_rows - bsz - 1, d), F32)], axis=0)
    mod_all = _ada_mod(cvec, w_ada, b_ada)

    xc = jnp.concatenate([ctx, x], axis=1)
    for layer in range(depth):
        need_ctx = layer < depth - 1
        mod = mod_all[layer].reshape(n_rows, 6, d)
        pad = jnp.zeros((bsz, 2, d), F32)
        mod_lat = jnp.concatenate([mod[:bsz], pad], axis=1)
        mod_ctx = jnp.broadcast_to(jnp.concatenate([mod[bsz], pad[0]], axis=0), (bsz, 8, d))
        modv = jnp.stack([mod_ctx, mod_lat], axis=1)

        gq = jnp.tile(g_q_b[layer] * (HEAD_DIM ** -0.5 * LOG2E), 2).reshape(1, LANES)
        gk = jnp.tile(g_k_b[layer], 2).reshape(1, LANES)
        pa, pb, pc, pf = _in_proj(xc, modv, g_norm1[layer].reshape(1, d), _proj_weight(w_in[layer]),
                                  cos, sin, gq, gk)

        ya = _attn_a(pa, sink_a[layer], n_ctx, need_ctx)
        yb = _attn_b(pb, n_ctx, need_ctx)

        gbias = _gate_lanes([b_igate[layer, 0], b_fgate[layer, 0], b_igate[layer, 1], b_fgate[layer, 1],
                             dt_bias[layer, 0], dt_bias[layer, 1]])
        hf, hb = _mlstm(pc, pf, gbias, n_ctx)

        alog = jnp.broadcast_to(jnp.pad(a_log[layer], ((0, 0), (0, 8 - N_HEADS)))[:, :, None], (2, 8, LANES))
        dsk = jnp.repeat(d_skip[layer], HEAD_DIM).reshape(1, GROUP_WIDTH)
        yf, ybw = _ssd(pf, gbias, alog, conv_w[layer], conv_b[layer].reshape(1, -1), dsk, n_ctx)

        xc = _out_ffn(xc, ya, yb, hf, hb, yf, ybw, pf, g_mlstm[layer].reshape(1, GROUP_WIDTH),
                      g_ssm[layer].reshape(1, GROUP_WIDTH), modv, g_norm2[layer].reshape(1, d),
                      _out_weight(w_out[layer]), w_ff1[layer].astype(MXU_DTYPE), w_ff2[layer].astype(MXU_DTYPE),
                      g_final.reshape(1, d), n_ctx, final=not need_ctx)
    return xc
```

```python
import functools
import math

import numpy as np
import jax
import jax.numpy as jnp
from jax import lax
from jax.experimental import pallas as pl
from jax.experimental.pallas import tpu as pltpu

F32 = jnp.float32
MXU_DTYPE = jnp.bfloat16

D_MODEL = 1024
HEAD_DIM = 64
LANES = 128
GRID_W = 64
WINDOW = 128
ROPE_BASE = 10000.0
EPS = 1e-6
N_HEADS = 4
GROUP_WIDTH = N_HEADS * HEAD_DIM
D_STATE = 128
CHUNK = 128
ROW_TILE = 256
CHUNKS_PER_TILE = ROW_TILE // CHUNK
KEY_TILE = 1024
SOFTMAX_ROWS = 64
D_FF = 4 * D_MODEL
FF_CHUNK = 1024
HALO = 8
VMEM_LIMIT = 56 * 1024 * 1024
LOG2E = math.log2(math.e)

_SPLIT_SIZES = (256, 128, 128, 256, 128, 128, 256, 256, 256, 256, 16, 256, 256, 256, 256, 8)
_OFF = [int(o) for o in np.concatenate([[0], np.cumsum(_SPLIT_SIZES)])]
_Q_HEAD_ORDER = (0, 2, 1, 3)

_COL_A = 0
_COL_B = 512
_COL_C = 1024
_COL_F = 1792
_N_F = 768 + 256 + 256 + 128
_N_PROJ = _COL_F + _N_F
_GATE_BLK = (_N_F - LANES) // LANES
_O_BLK = 768 // GROUP_WIDTH
_Z_BLK = (768 + GROUP_WIDTH) // GROUP_WIDTH


def _dot(a, b):
    return jnp.dot(a, b, preferred_element_type=F32)


def _dot_nt(a, b):
    return lax.dot_general(a, b, (((1,), (1,)), ((), ())), preferred_element_type=F32)


def _dot_tn(a, b):
    return lax.dot_general(a, b, (((0,), (0,)), ((), ())), preferred_element_type=F32)


def _split3(x):
    hi = x.astype(MXU_DTYPE)
    r1 = x - hi.astype(F32)
    mid = r1.astype(MXU_DTYPE)
    lo = (r1 - mid.astype(F32)).astype(MXU_DTYPE)
    return [hi, mid, lo]


def _rows_cumsum(rows, mt3):
    return _dot(jnp.concatenate(_split3(rows), axis=1), mt3)


def _rows_to_columns(rows, sel3):
    return _dot_tn(jnp.concatenate(_split3(rows), axis=0), sel3)


def _cummax_lanes(x, direction):
    n = x.shape[-1]
    lane = lax.broadcasted_iota(jnp.int32, x.shape, 1)
    shift = 1
    while shift < n:
        if direction == 0:
            moved = jnp.where(lane >= shift, pltpu.roll(x, shift, 1), -jnp.inf)
        else:
            moved = jnp.where(lane < n - shift, pltpu.roll(x, n - shift, 1), -jnp.inf)
        x = jnp.maximum(x, moved)
        shift *= 2
    return x


def _lane_is_low(shape):
    lane = lax.broadcasted_iota(jnp.int32, shape, len(shape) - 1)
    return (lane % LANES) < HEAD_DIM


def _head_mean_square(t, low):
    sq = t * t
    lo = jnp.sum(jnp.where(low, sq, 0.0), axis=-1, keepdims=True)
    hi = jnp.sum(jnp.where(low, 0.0, sq), axis=-1, keepdims=True)
    return jnp.where(low, lo, hi) * (1.0 / HEAD_DIM)


def _params(*sem):
    return pltpu.CompilerParams(dimension_semantics=sem, vmem_limit_bytes=VMEM_LIMIT)


_RESIDENT = pl.Buffered(1)


def _ada_kernel(c_ref, w_ref, b_ref, out_ref):
    cv = c_ref[...]
    cv = cv * jax.nn.sigmoid(cv)
    out_ref[...] = _dot(cv.astype(MXU_DTYPE), w_ref[...].astype(MXU_DTYPE)) + b_ref[...]


def _ada_mod(cvec, w_ada, b_ada):
    depth, d, n = w_ada.shape
    rows = cvec.shape[0]
    tn = 1536
    return pl.pallas_call(
        _ada_kernel,
        grid=(depth, n // tn),
        in_specs=[
            pl.BlockSpec((rows, d), lambda l, j: (0, 0)),
            pl.BlockSpec((None, d, tn), lambda l, j: (l, 0, j)),
            pl.BlockSpec((None, 1, tn), lambda l, j: (l, 0, j)),
        ],
        out_specs=pl.BlockSpec((None, rows, tn), lambda l, j: (l, 0, j)),
        out_shape=jax.ShapeDtypeStruct((depth, rows, n), F32),
        compiler_params=_params("parallel", "parallel"),
        name="ada_mod",
    )(cvec, w_ada, b_ada.reshape(depth, 1, n))


def _rmsnorm_mod(x, g, shift, scale):
    ms = jnp.mean(x * x, axis=-1, keepdims=True)
    return (x * lax.rsqrt(ms + EPS) * g) * (1.0 + scale) + shift


def _in_proj_kernel(x_ref, mod_ref, g1_ref, w_ref, cos_ref, sin_ref, gq_ref, gk_ref,
                    a_ref, b_ref, c_ref, f_ref):
    hn = _rmsnorm_mod(x_ref[...], g1_ref[...], mod_ref[0:1, :], mod_ref[1:2, :])
    hb = hn.astype(MXU_DTYPE)
    cos = cos_ref[...]
    sin = sin_ref[...]
    lane = lax.broadcasted_iota(jnp.int32, (1, LANES), 1)
    first = (lane % 32) < 16
    low = _lane_is_low((1, LANES))

    def rope(t):
        partner = jnp.where(first, pltpu.roll(t, LANES - 16, 1), pltpu.roll(t, 16, 1))
        return t * cos + partner * sin

    pa = _dot(hb, w_ref[:, _COL_A:_COL_A + 512])
    for j in range(3):
        a_ref[:, j * LANES:(j + 1) * LANES] = rope(pa[:, j * LANES:(j + 1) * LANES]).astype(a_ref.dtype)
    a_ref[:, 3 * LANES:] = pa[:, 3 * LANES:].astype(a_ref.dtype)

    pb = _dot(hb, w_ref[:, _COL_B:_COL_B + 512])
    for j in range(3):
        t = pb[:, j * LANES:(j + 1) * LANES]
        g = gq_ref[...] if j < 2 else gk_ref[...]
        t = t * lax.rsqrt(_head_mean_square(t, low) + EPS) * g
        b_ref[:, j * LANES:(j + 1) * LANES] = rope(t).astype(b_ref.dtype)
    b_ref[:, 3 * LANES:] = pb[:, 3 * LANES:].astype(b_ref.dtype)

    c_ref[...] = _dot(hb, w_ref[:, _COL_C:_COL_F]).astype(c_ref.dtype)
    f_ref[...] = _dot(hb, w_ref[:, _COL_F:])


def _in_proj(xc, modv, g1, w, cos, sin, gq, gk):
    bsz, u, d = xc.shape
    nt = u // ROW_TILE
    row = lambda b, i: (b, i, 0)
    const = lambda b, i: (0, 0)
    return pl.pallas_call(
        _in_proj_kernel,
        grid=(bsz, nt),
        in_specs=[
            pl.BlockSpec((None, ROW_TILE, d), row),
            pl.BlockSpec((None, None, 8, d), lambda b, i: (b, jnp.minimum(i, 1), 0, 0)),
            pl.BlockSpec((1, d), const),
            pl.BlockSpec((d, _N_PROJ), const, pipeline_mode=_RESIDENT),
            pl.BlockSpec((ROW_TILE, LANES), lambda b, i: (i, 0)),
            pl.BlockSpec((ROW_TILE, LANES), lambda b, i: (i, 0)),
            pl.BlockSpec((1, LANES), const),
            pl.BlockSpec((1, LANES), const),
        ],
        out_specs=[
            pl.BlockSpec((None, ROW_TILE, 512), row),
            pl.BlockSpec((None, ROW_TILE, 512), row),
            pl.BlockSpec((None, ROW_TILE, 768), row),
            pl.BlockSpec((None, ROW_TILE, _N_F), row),
        ],
        out_shape=[
            jax.ShapeDtypeStruct((bsz, u, 512), MXU_DTYPE),
            jax.ShapeDtypeStruct((bsz, u, 512), MXU_DTYPE),
            jax.ShapeDtypeStruct((bsz, u, 768), MXU_DTYPE),
            jax.ShapeDtypeStruct((bsz, u, _N_F), F32),
        ],
        compiler_params=_params("parallel", "parallel"),
        name="in_proj",
    )(xc, modv, g1, w, cos, sin, gq, gk)


def _stack_heads(q_ref, qs_ref):
    low = _lane_is_low((1, LANES))
    for jb in range(2):
        q = q_ref[:, jb * LANES:(jb + 1) * LANES]
        for half in range(2):
            idx = 2 * jb + half
            keep = low if half == 0 else jnp.logical_not(low)
            qs_ref[idx * ROW_TILE:(idx + 1) * ROW_TILE, :] = jnp.where(keep, q, jnp.zeros_like(q))


def _unstack_heads(o, out_ref):
    low = _lane_is_low((1, LANES))
    for jb in range(2):
        lo = o[(2 * jb) * ROW_TILE:(2 * jb + 1) * ROW_TILE, :]
        hi = o[(2 * jb + 1) * ROW_TILE:(2 * jb + 2) * ROW_TILE, :]
        out_ref[:, jb * LANES:(jb + 1) * LANES] = jnp.where(low, lo, hi).astype(out_ref.dtype)


def _attn_a_kernel(sink_ref, q_ref, kp_ref, ko_ref, kn_ref, kc_ref, vp_ref, vo_ref, vn_ref, vc_ref,
                   out_ref, qs_ref, kbuf, vbuf, s_ref, o_ref, *, n_tok, n_ctx, q_off):
    i = pl.program_id(1) + q_off
    nb = 2 * WINDOW + ROW_TILE
    nk = nb + n_ctx
    kbuf[0:WINDOW] = kp_ref[...]
    kbuf[WINDOW:WINDOW + ROW_TILE] = ko_ref[...]
    kbuf[WINDOW + ROW_TILE:nb] = kn_ref[...]
    kbuf[nb:nk] = kc_ref[...]
    vbuf[0:WINDOW] = vp_ref[...]
    vbuf[WINDOW:WINDOW + ROW_TILE] = vo_ref[...]
    vbuf[WINDOW + ROW_TILE:nb] = vn_ref[...]
    vbuf[nb:nk] = vc_ref[...]
    _stack_heads(q_ref, qs_ref)

    r = lax.broadcasted_iota(jnp.int32, (ROW_TILE, nk), 0)
    c = lax.broadcasted_iota(jnp.int32, (ROW_TILE, nk), 1)
    kpos = (i - 1) * ROW_TILE - WINDOW + c
    dist = c - r
    band = jnp.where(dist >= 0, jnp.where(dist <= 2 * WINDOW, 1, 0), 0)
    band = jnp.where(kpos >= 0, jnp.where(kpos < n_tok, band, 0), 0)
    band = jnp.where(i >= 1, band, 0)
    bias = jnp.where(c >= nb, 0.0, jnp.where(band > 0, 0.0, -jnp.inf)).astype(F32)

    for idx in range(N_HEADS):
        rows = slice(idx * ROW_TILE, (idx + 1) * ROW_TILE)
        s_ref[rows, :] = _dot_nt(qs_ref[rows, :], kbuf[...]) + bias
    for idx in range(N_HEADS):
        rows = slice(idx * ROW_TILE, (idx + 1) * ROW_TILE)
        sink = sink_ref[_Q_HEAD_ORDER[idx]]
        m = jnp.maximum(jnp.max(s_ref[rows, :], axis=-1, keepdims=True), sink)
        p = jnp.exp(s_ref[rows, :] - m)
        l = jnp.sum(p, axis=-1, keepdims=True) + jnp.exp(sink - m)
        o_ref[rows, :] = _dot(p.astype(MXU_DTYPE), vbuf[...]) / l
    _unstack_heads(o_ref[...], out_ref)


def _attn_a(pa, sink, n_ctx, need_ctx):
    bsz, u, _ = pa.shape
    n_tok = u - n_ctx
    q_off = 0 if need_ctx else n_ctx // ROW_TILE
    nt = u // ROW_TILE - q_off
    last_blk = u // WINDOW - 1
    rpw = ROW_TILE // WINDOW
    nk = 2 * WINDOW + ROW_TILE + n_ctx

    def own(col):
        return lambda b, i: (b, i + q_off, col)

    def prev(col):
        return lambda b, i: (b, jnp.maximum((i + q_off) * rpw - 1, 0), col)

    def nxt(col):
        return lambda b, i: (b, jnp.minimum((i + q_off + 1) * rpw, last_blk), col)

    def ctx(col):
        return lambda b, i: (b, 0, col)

    kern = functools.partial(_attn_a_kernel, n_tok=n_tok, n_ctx=n_ctx, q_off=q_off)
    return pl.pallas_call(
        kern,
        grid=(bsz, nt),
        in_specs=[
            pl.BlockSpec(memory_space=pltpu.SMEM),
            pl.BlockSpec((None, ROW_TILE, 2 * LANES), own(0)),
            pl.BlockSpec((None, WINDOW, LANES), prev(2)),
            pl.BlockSpec((None, ROW_TILE, LANES), own(2)),
            pl.BlockSpec((None, WINDOW, LANES), nxt(2)),
            pl.BlockSpec((None, n_ctx, LANES), ctx(2)),
            pl.BlockSpec((None, WINDOW, LANES), prev(3)),
            pl.BlockSpec((None, ROW_TILE, LANES), own(3)),
            pl.BlockSpec((None, WINDOW, LANES), nxt(3)),
            pl.BlockSpec((None, n_ctx, LANES), ctx(3)),
        ],
        out_specs=pl.BlockSpec((None, ROW_TILE, 2 * LANES), lambda b, i: (b, i + q_off, 0)),
        out_shape=jax.ShapeDtypeStruct((bsz, u, 2 * LANES), MXU_DTYPE),
        scratch_shapes=[pltpu.VMEM((N_HEADS * ROW_TILE, LANES), MXU_DTYPE),
                        pltpu.VMEM((nk, LANES), MXU_DTYPE), pltpu.VMEM((nk, LANES), MXU_DTYPE),
                        pltpu.VMEM((N_HEADS * ROW_TILE, nk), F32),
                        pltpu.VMEM((N_HEADS * ROW_TILE, LANES), F32)],
        compiler_params=_params("parallel", "parallel"),
        name="attn_window",
    )(sink, pa, pa, pa, pa, pa, pa, pa, pa, pa)


def _attn_b_kernel(q_ref, k_ref, v_ref, out_ref, qs_ref, vext_ref, s_ref, p_ref, m_ref, alpha_ref, acc_ref,
                   *, n_ctx, n_tok, q_off, tk):
    qi = pl.program_id(1) + q_off
    n_rows = N_HEADS * ROW_TILE

    @pl.when(pl.program_id(1) == 0)
    def _():
        vext_ref[:, 0:LANES] = v_ref[...]
        vext_ref[:, LANES:] = jnp.ones((vext_ref.shape[0], LANES), vext_ref.dtype)

    _stack_heads(q_ref, qs_ref)

    def scores(slot, start, size):
        s_ref[slot, :, 0:size] = _dot_nt(qs_ref[...], k_ref[start:start + size, :])

    def softmax(slot, size, first):
        m_new = jnp.max(s_ref[slot, :, 0:size], axis=-1, keepdims=True)
        if not first:
            m_old = m_ref[...]
            m_new = jnp.maximum(m_old, m_new)
            alpha_ref[slot] = jnp.exp2(m_old - m_new)
        m_ref[...] = m_new
        p_ref[slot, :, 0:size] = jnp.exp2(s_ref[slot, :, 0:size] - m_new).astype(p_ref.dtype)

    def accumulate(slot, start, size, first):
        pv = _dot(p_ref[slot, :, 0:size], vext_ref[start:start + size, :])
        if first:
            acc_ref[...] = pv
        else:
            acc_ref[...] = alpha_ref[slot] * acc_ref[...] + pv

    def attend(tiles):
        scores(0, *tiles[0])
        for t, (start, size) in enumerate(tiles):
            if t + 1 < len(tiles):
                scores((t + 1) % 2, *tiles[t + 1])
            softmax(t % 2, size, t == 0)
            accumulate(t % 2, start, size, t == 0)
        _unstack_heads(acc_ref[:, 0:LANES] / acc_ref[:, LANES:], out_ref)

    ctx_tiles = [(0, n_ctx)]
    all_tiles = ctx_tiles + [(n_ctx + t * tk, tk) for t in range(n_tok // tk)]
    if q_off == 0:
        pl.when(qi == 0)(functools.partial(attend, ctx_tiles))
        pl.when(qi > 0)(functools.partial(attend, all_tiles))
    else:
        attend(all_tiles)


def _attn_b(pb, n_ctx, need_ctx):
    bsz, u, _ = pb.shape
    n_tok = u - n_ctx
    q_off = 0 if need_ctx else n_ctx // ROW_TILE
    nt = u // ROW_TILE - q_off
    tk = min(KEY_TILE, n_tok)
    assert n_tok % tk == 0
    kern = functools.partial(_attn_b_kernel, n_ctx=n_ctx, n_tok=n_tok, q_off=q_off, tk=tk)
    return pl.pallas_call(
        kern,
        grid=(bsz, nt),
        in_specs=[
            pl.BlockSpec((None, ROW_TILE, 2 * LANES), lambda b, i: (b, i + q_off, 0)),
            pl.BlockSpec((None, u, LANES), lambda b, i: (b, 0, 2)),
            pl.BlockSpec((None, u, LANES), lambda b, i: (b, 0, 3)),
        ],
        out_specs=pl.BlockSpec((None, ROW_TILE, 2 * LANES), lambda b, i: (b, i + q_off, 0)),
        out_shape=jax.ShapeDtypeStruct((bsz, u, 2 * LANES), MXU_DTYPE),
        scratch_shapes=[
            pltpu.VMEM((N_HEADS * ROW_TILE, LANES), MXU_DTYPE),
            pltpu.VMEM((u, 2 * LANES), MXU_DTYPE),
            pltpu.VMEM((2, N_HEADS * ROW_TILE, tk), F32),
            pltpu.VMEM((2, N_HEADS * ROW_TILE, tk), MXU_DTYPE),
            pltpu.VMEM((N_HEADS * ROW_TILE, 1), F32),
            pltpu.VMEM((2, N_HEADS * ROW_TILE, 1), F32),
            pltpu.VMEM((N_HEADS * ROW_TILE, 2 * LANES), F32),
        ],
        compiler_params=_params("parallel", "arbitrary"),
        name="attn_dense",
    )(pb, pb, pb)


def _scan_tile(j, direction, nt_ctx, nt):
    if direction == 0:
        return j
    return jnp.where(j < nt_ctx, nt_ctx - 1 - j, nt + nt_ctx - 1 - j)


def _causal_mask(direction):
    t = lax.broadcasted_iota(jnp.int32, (CHUNK, CHUNK), 0)
    s = lax.broadcasted_iota(jnp.int32, (CHUNK, CHUNK), 1)
    return (s <= t) if direction == 0 else (s >= t)


def _chunk_order(direction):
    order = range(CHUNKS_PER_TILE)
    return order if direction == 0 else reversed(order)


def _log_sigmoid(x):
    return jnp.minimum(x, 0.0) - jnp.log1p(jnp.exp(-jnp.abs(x)))


def _softplus(x):
    return jnp.maximum(x, 0.0) + jnp.log1p(jnp.exp(-jnp.abs(x)))


def _mlstm_gates(direction, g_rows, m_prev, mt3):
    end = CHUNK - 1 if direction == 0 else 0
    i_r = jnp.concatenate([g[16 * direction:16 * direction + 8] for g in g_rows], axis=0)
    f_r = jnp.concatenate([g[16 * direction + 8:16 * direction + 16] for g in g_rows], axis=0)
    b_r = _rows_cumsum(_log_sigmoid(f_r), mt3)
    r_r = i_r - b_r
    cmax = _cummax_lanes(r_r, direction)
    tot = jnp.broadcast_to(b_r[:, end:end + 1], b_r.shape)
    g2 = tot - b_r + i_r
    g2max = jnp.broadcast_to(jnp.max(g2, axis=-1, keepdims=True), g2.shape)
    per_chunk = []
    for idx in range(len(g_rows)):
        sl = slice(8 * idx, 8 * idx + 8)
        c_r = jnp.maximum(m_prev, cmax[sl])
        dp_r = jnp.exp(m_prev - c_r)
        em_r = jnp.exp(-(b_r[sl] + c_r))
        m_new = jnp.maximum(tot[sl] + m_prev, g2max[sl])
        wk_r = jnp.exp(g2[sl] - m_new)
        cd_r = jnp.exp(tot[sl] + m_prev - m_new)
        per_chunk.append((jnp.concatenate([c_r, dp_r, em_r, wk_r], axis=0), r_r[sl], cd_r))
        m_prev = m_new
    return per_chunk, m_prev


def _mlstm_chunks(items, state):
    low1 = _lane_is_low((1, LANES))
    low2 = _lane_is_low((1, 2 * LANES))
    ones = jnp.ones((CHUNK, LANES), MXU_DTYPE)
    r2 = lax.broadcasted_iota(jnp.int32, (LANES, 2 * LANES), 0)
    c2 = lax.broadcasted_iota(jnp.int32, (LANES, 2 * LANES), 1)
    state_mask = (r2 < HEAD_DIM) == ((c2 % LANES) < HEAD_DIM)

    pre = []
    for direction, qkv, r_r, cd_r, cols in items:
        per_pair = []
        for pair in range(2):
            q = qkv[:, pair * LANES:(pair + 1) * LANES]
            k = qkv[:, GROUP_WIDTH + pair * LANES:GROUP_WIDTH + (pair + 1) * LANES]
            v = qkv[:, 2 * GROUP_WIDTH + pair * LANES:2 * GROUP_WIDTH + (pair + 1) * LANES]
            qk = [_dot_nt(jnp.where(low1 if half == 0 else jnp.logical_not(low1), q, jnp.zeros_like(q)), k)
                  for half in range(2)]
            per_pair.append((q, k, jnp.concatenate([v, ones], axis=1), qk))
        pre.append(per_pair)

    mid = []
    for (direction, qkv, r_r, cd_r, cols), per_pair in zip(items, pre):
        mask = _causal_mask(direction)
        res = []
        for pair, (q, k, vext, qk) in enumerate(per_pair):
            svs, dens = [], []
            for half in range(2):
                h = 2 * pair + half
                w = jnp.exp(jnp.where(mask, r_r[h:h + 1, :] - cols[:, h * LANES:(h + 1) * LANES], -jnp.inf))
                s = qk[half] * w
                s_hi = s.astype(MXU_DTYPE)
                s_lo = (s - s_hi.astype(F32)).astype(MXU_DTYPE)
                sv = _dot(s_hi, vext)
                svs.append(sv[:, :LANES])
                dens.append(sv[:, LANES:] + _dot(s_lo, ones))
            wk = cols[:, (8 + pair) * LANES:(9 + pair) * LANES]
            upd = _dot_tn((k.astype(F32) * wk).astype(MXU_DTYPE), vext)
            res.append((jnp.where(low1, svs[0], svs[1]), jnp.where(low1, dens[0], dens[1]),
                        jnp.where(state_mask, upd, 0.0)))
        mid.append(res)

    outs = []
    for (direction, qkv, r_r, cd_r, cols), per_pair, res in zip(items, pre, mid):
        out = []
        for pair in range(2):
            q = per_pair[pair][0]
            sv, den_intra, upd = res[pair]
            prior = _dot(q, state[direction][pair].astype(MXU_DTYPE))
            dp = cols[:, (4 + pair) * LANES:(5 + pair) * LANES]
            em = cols[:, (6 + pair) * LANES:(7 + pair) * LANES]
            num = sv + dp * prior[:, :LANES]
            den = den_intra + dp * prior[:, LANES:]
            out.append(num / jnp.maximum(jnp.abs(den), em))
            cd0 = jnp.concatenate([cd_r[2 * pair:2 * pair + 1]] * 2, axis=1)
            cd1 = jnp.concatenate([cd_r[2 * pair + 1:2 * pair + 2]] * 2, axis=1)
            state[direction][pair] = state[direction][pair] * jnp.where(low2, cd0, cd1) + upd
        outs.append(out)
    return outs


def _mlstm_kernel(qkvf_ref, gf_ref, qkvb_ref, gb_ref, gbias_ref, mt3_ref, sel3_ref, hf_ref, hb_ref,
                  st_ref, m_ref):
    @pl.when(pl.program_id(1) == 0)
    def _():
        st_ref[...] = jnp.zeros(st_ref.shape, F32)
        m_ref[...] = jnp.zeros(m_ref.shape, F32)

    gbias = gbias_ref[...]
    dirs = ((qkvf_ref, gf_ref, hf_ref), (qkvb_ref, gb_ref, hb_ref))
    gates = []
    for direction, (_, g_ref, _) in enumerate(dirs):
        g_rows = [(g_ref[ci * CHUNK:(ci + 1) * CHUNK, :] + gbias).T for ci in _chunk_order(direction)]
        per_chunk, m_new = _mlstm_gates(direction, g_rows, m_ref[direction], mt3_ref[direction])
        m_ref[direction] = m_new
        gates.append(per_chunk)
    work = [(direction, idx) for idx in range(CHUNKS_PER_TILE) for direction in range(2)]
    rows_all = jnp.concatenate([gates[direction][idx][0] for direction, idx in work], axis=1)
    cols_all = _rows_to_columns(rows_all, sel3_ref[...])

    state = [[st_ref[direction, pair] for pair in range(2)] for direction in range(2)]
    items, row_slices = [], []
    for n, (direction, idx) in enumerate(work):
        ci = list(_chunk_order(direction))[idx]
        rows = slice(ci * CHUNK, (ci + 1) * CHUNK)
        _, r_r, cd_r = gates[direction][idx]
        items.append((direction, dirs[direction][0][rows, :], r_r, cd_r, cols_all[n * CHUNK:(n + 1) * CHUNK, :]))
        row_slices.append(rows)
    outs = _mlstm_chunks(items, state)
    for (direction, *_), rows, out in zip(items, row_slices, outs):
        for pair in range(2):
            dirs[direction][2][rows, pair * LANES:(pair + 1) * LANES] = out[pair]
    for direction in range(2):
        for pair in range(2):
            st_ref[direction, pair] = state[direction][pair]


def _scan_masks():
    s = np.arange(CHUNK)[:, None]
    t = np.arange(CHUNK)[None, :]
    mats = [np.tile((s <= t).astype(np.float32), (3, 1)), np.tile((s >= t).astype(np.float32), (3, 1))]
    return jnp.asarray(np.stack(mats), MXU_DTYPE)


def _column_selector(n_head_groups, n_pair_groups):
    n_groups = n_head_groups + n_pair_groups
    n_rows = 8 * n_groups + (-8 * n_groups) % 16
    n_cols = (N_HEADS * n_head_groups + 2 * n_pair_groups) * LANES
    sel = np.zeros((n_rows, n_cols), np.float32)
    col = 0
    for g in range(n_head_groups):
        for h in range(N_HEADS):
            sel[8 * g + h, col:col + LANES] = 1.0
            col += LANES
    for g in range(n_head_groups, n_groups):
        for pair in range(2):
            sel[8 * g + 2 * pair, col:col + HEAD_DIM] = 1.0
            sel[8 * g + 2 * pair + 1, col + HEAD_DIM:col + LANES] = 1.0
            col += LANES
    return jnp.asarray(np.tile(sel, (3, 1)), MXU_DTYPE)


def _mlstm(pc, pf, gbias, n_ctx):
    bsz, u, _ = pc.shape
    nt = u // ROW_TILE
    nt_ctx = n_ctx // ROW_TILE
    fwd = lambda b, j: _scan_tile(j, 0, nt_ctx, nt)
    bwd = lambda b, j: _scan_tile(j, 1, nt_ctx, nt)
    out = jax.ShapeDtypeStruct((bsz, u, GROUP_WIDTH), F32)
    mt3 = _scan_masks()
    sel3 = _column_selector(1, 3)
    return pl.pallas_call(
        _mlstm_kernel,
        grid=(bsz, nt),
        in_specs=[
            pl.BlockSpec((None, ROW_TILE, 768), lambda b, j: (b, fwd(b, j), 0)),
            pl.BlockSpec((None, ROW_TILE, LANES), lambda b, j: (b, fwd(b, j), _GATE_BLK)),
            pl.BlockSpec((None, ROW_TILE, 768), lambda b, j: (b, bwd(b, j), 0)),
            pl.BlockSpec((None, ROW_TILE, LANES), lambda b, j: (b, bwd(b, j), _GATE_BLK)),
            pl.BlockSpec((1, LANES), lambda b, j: (0, 0)),
            pl.BlockSpec(mt3.shape, lambda b, j: (0, 0, 0)),
            pl.BlockSpec(sel3.shape, lambda b, j: (0, 0)),
        ],
        out_specs=[
            pl.BlockSpec((None, ROW_TILE, GROUP_WIDTH), lambda b, j: (b, fwd(b, j), 0)),
            pl.BlockSpec((None, ROW_TILE, GROUP_WIDTH), lambda b, j: (b, bwd(b, j), 0)),
        ],
        out_shape=[out, out],
        scratch_shapes=[pltpu.VMEM((2, 2, LANES, 2 * LANES), F32), pltpu.VMEM((2, 8, LANES), F32)],
        compiler_params=_params("parallel", "arbitrary"),
        name="mlstm_scan",
    )(pc, pf, pc, pf, gbias, mt3, sel3)


def _ssd_conv(x_ref, xp_ref, xn_ref, cw_ref, cb_ref, tile, nt_ctx, nt):
    xin = x_ref[...]
    has_prev = (tile != 0) & (tile != nt_ctx)
    has_next = (tile != nt_ctx - 1) & (tile != nt - 1)
    prow = jnp.where(has_prev, xp_ref[HALO - 1:HALO, :], 0.0)
    nrow = jnp.where(has_next, xn_ref[0:1, :], 0.0)
    ridx = lax.broadcasted_iota(jnp.int32, (ROW_TILE, 1), 0)
    up = jnp.where(ridx == 0, prow, pltpu.roll(xin, 1, 0))
    dn = jnp.where(ridx == ROW_TILE - 1, nrow, pltpu.roll(xin, ROW_TILE - 1, 0))
    u = cw_ref[0:1, :] * up + cw_ref[1:2, :] * xin + cw_ref[2:3, :] * dn + cb_ref[...]
    return u * jax.nn.sigmoid(u)


def _ssd_gates(direction, g_rows, neg_a, mt3):
    end = CHUNK - 1 if direction == 0 else 0
    n = len(g_rows)
    dt_r = _softplus(jnp.concatenate([g[32 + 8 * direction:40 + 8 * direction] for g in g_rows], axis=0))
    cum_r = _rows_cumsum(dt_r * jnp.concatenate([neg_a] * n, axis=0), mt3)
    tot = jnp.broadcast_to(cum_r[:, end:end + 1], cum_r.shape)
    e_r = jnp.exp(cum_r)
    wend_r = jnp.exp(tot - cum_r) * dt_r
    d_r = jnp.exp(tot)
    zero = jnp.zeros((8, CHUNK), F32)
    per_chunk = []
    for idx in range(n):
        sl = slice(8 * idx, 8 * idx + 8)
        per_chunk.append((jnp.concatenate([cum_r[sl], e_r[sl], wend_r[sl], zero], axis=0),
                          cum_r[sl], dt_r[sl], d_r[sl]))
    return per_chunk


def _ssd_chunks(items, state):
    low1 = _lane_is_low((1, LANES))

    pre = []
    for direction, u, cum_r, dt_r, d_r, cols in items:
        per_grp = []
        for grp in range(2):
            bm = u[:, GROUP_WIDTH + grp * D_STATE:GROUP_WIDTH + (grp + 1) * D_STATE].astype(MXU_DTYPE)
            cm = u[:, 2 * GROUP_WIDTH + grp * D_STATE:2 * GROUP_WIDTH + (grp + 1) * D_STATE].astype(MXU_DTYPE)
            x_pair = u[:, grp * LANES:(grp + 1) * LANES]
            per_grp.append((bm, cm, x_pair, _dot_nt(cm, bm)))
        pre.append(per_grp)

    mid = []
    for (direction, u, cum_r, dt_r, d_r, cols), per_grp in zip(items, pre):
        mask = _causal_mask(direction)
        res = []
        for grp, (bm, cm, x_pair, gmat) in enumerate(per_grp):
            xb = x_pair.astype(MXU_DTYPE)
            ys = []
            for half in range(2):
                h = 2 * grp + half
                decay = jnp.exp(jnp.where(mask, cols[:, h * LANES:(h + 1) * LANES] - cum_r[h:h + 1, :], -jnp.inf))
                s = gmat * decay * dt_r[h:h + 1, :]
                ys.append(_dot(s.astype(MXU_DTYPE), xb))
            w_pair = cols[:, (6 + grp) * LANES:(7 + grp) * LANES]
            upd = _dot_tn(bm, (x_pair * w_pair).astype(MXU_DTYPE))
            res.append((jnp.where(low1, ys[0], ys[1]), upd))
        mid.append(res)

    outs = []
    for (direction, u, cum_r, dt_r, d_r, cols), per_grp, res in zip(items, pre, mid):
        out = []
        for grp in range(2):
            cm = per_grp[grp][1]
            y_intra, upd = res[grp]
            ch = _dot(cm, state[direction][grp].astype(MXU_DTYPE))
            e_pair = cols[:, (4 + grp) * LANES:(5 + grp) * LANES]
            out.append(y_intra + e_pair * ch)
            d_pair = jnp.where(low1, d_r[2 * grp:2 * grp + 1], d_r[2 * grp + 1:2 * grp + 2])
            state[direction][grp] = state[direction][grp] * d_pair + upd
        outs.append(out)
    return outs


def _ssd_kernel(xf_ref, xfp_ref, xfn_ref, gf_ref, xb_ref, xbp_ref, xbn_ref, gb_ref,
                gbias_ref, alog_ref, cw_ref, cb_ref, dskip_ref, mt3_ref, sel3_ref, yf_ref, yb_ref, st_ref,
                *, nt_ctx, nt):
    j = pl.program_id(1)

    @pl.when(j == 0)
    def _():
        st_ref[...] = jnp.zeros(st_ref.shape, F32)

    gbias = gbias_ref[...]
    dirs = ((xf_ref, xfp_ref, xfn_ref, gf_ref, yf_ref), (xb_ref, xbp_ref, xbn_ref, gb_ref, yb_ref))
    gates, convs = [], []
    for direction, (x_ref, xp_ref, xn_ref, _, _) in enumerate(dirs):
        tile = _scan_tile(j, direction, nt_ctx, nt)
        convs.append(_ssd_conv(x_ref, xp_ref, xn_ref, cw_ref, cb_ref, tile, nt_ctx, nt))
    g_rows = [[(g_ref[ci * CHUNK:(ci + 1) * CHUNK, :] + gbias).T for ci in _chunk_order(direction)]
              for direction, (_, _, _, g_ref, _) in enumerate(dirs)]
    for direction in range(2):
        gates.append(_ssd_gates(direction, g_rows[direction], -jnp.exp(alog_ref[direction]), mt3_ref[direction]))
    work = [(direction, idx) for idx in range(CHUNKS_PER_TILE) for direction in range(2)]
    rows_all = jnp.concatenate([gates[direction][idx][0] for direction, idx in work], axis=1)
    cols_all = _rows_to_columns(rows_all, sel3_ref[...])

    state = [[st_ref[direction, grp] for grp in range(2)] for direction in range(2)]
    items, row_slices = [], []
    for n, (direction, idx) in enumerate(work):
        ci = list(_chunk_order(direction))[idx]
        rows = slice(ci * CHUNK, (ci + 1) * CHUNK)
        _, cum_r, dt_r, d_r = gates[direction][idx]
        items.append((direction, convs[direction][rows, :], cum_r, dt_r, d_r,
                      cols_all[n * CHUNK:(n + 1) * CHUNK, :]))
        row_slices.append(rows)
    outs = _ssd_chunks(items, state)
    for (direction, u, *_), rows, out in zip(items, row_slices, outs):
        for grp in range(2):
            sl = slice(grp * LANES, (grp + 1) * LANES)
            y = out[grp]
            if direction == 0:
                y = y + dskip_ref[:, sl] * u[:, sl]
            dirs[direction][4][rows, sl] = y
    for direction in range(2):
        for grp in range(2):
            st_ref[direction, grp] = state[direction][grp]


def _ssd(pf, gbias, alog, conv_w, conv_b, d_skip, n_ctx):
    bsz, u, _ = pf.shape
    nt = u // ROW_TILE
    nt_ctx = n_ctx // ROW_TILE
    hpt = ROW_TILE // HALO
    last_halo = u // HALO - 1
    const = lambda b, j: (0, 0)

    def tile_specs(direction):
        tile = lambda b, j: _scan_tile(j, direction, nt_ctx, nt)
        return [
            pl.BlockSpec((None, ROW_TILE, 768), lambda b, j: (b, tile(b, j), 0)),
            pl.BlockSpec((None, HALO, 768), lambda b, j: (b, jnp.maximum(tile(b, j) * hpt - 1, 0), 0)),
            pl.BlockSpec((None, HALO, 768), lambda b, j: (b, jnp.minimum((tile(b, j) + 1) * hpt, last_halo), 0)),
            pl.BlockSpec((None, ROW_TILE, LANES), lambda b, j: (b, tile(b, j), _GATE_BLK)),
        ]

    out = jax.ShapeDtypeStruct((bsz, u, GROUP_WIDTH), F32)
    mt3 = _scan_masks()
    sel3 = _column_selector(1, 2)
    return pl.pallas_call(
        functools.partial(_ssd_kernel, nt_ctx=nt_ctx, nt=nt),
        grid=(bsz, nt),
        in_specs=tile_specs(0) + tile_specs(1) + [
            pl.BlockSpec((1, LANES), const),
            pl.BlockSpec((2, 8, LANES), lambda b, j: (0, 0, 0)),
            pl.BlockSpec((3, 768), const),
            pl.BlockSpec((1, 768), const),
            pl.BlockSpec((1, GROUP_WIDTH), const),
            pl.BlockSpec(mt3.shape, lambda b, j: (0, 0, 0)),
            pl.BlockSpec(sel3.shape, const),
        ],
        out_specs=[
            pl.BlockSpec((None, ROW_TILE, GROUP_WIDTH), lambda b, j: (b, _scan_tile(j, 0, nt_ctx, nt), 0)),
            pl.BlockSpec((None, ROW_TILE, GROUP_WIDTH), lambda b, j: (b, _scan_tile(j, 1, nt_ctx, nt), 0)),
        ],
        out_shape=[out, out],
        scratch_shapes=[pltpu.VMEM((2, 2, D_STATE, LANES), F32)],
        compiler_params=_params("parallel", "arbitrary"),
        name="ssd_scan",
    )(pf, pf, pf, pf, pf, pf, pf, pf, gbias, alog, conv_w, conv_b, d_skip, mt3, sel3)


def _out_ffn_kernel(x_ref, ya_ref, yb_ref, hf_ref, hb_ref, o_ref, yf_ref, ybw_ref, z_ref, gm_ref, gs_ref,
                    mod_ref, g2_ref, wo_ref, w1_ref, w2_ref, gfin_ref, out_ref, *, final):
    low = _lane_is_low((1, LANES))
    ym = []
    for pair in range(2):
        sl = slice(pair * LANES, (pair + 1) * LANES)
        h = hf_ref[:, sl] + hb_ref[:, sl]
        hn = h * lax.rsqrt(_head_mean_square(h, low) + EPS) * gm_ref[:, sl]
        ym.append((hn * jax.nn.sigmoid(o_ref[:, sl])).astype(MXU_DTYPE))
    ys = []
    for grp in range(2):
        sl = slice(grp * LANES, (grp + 1) * LANES)
        z = z_ref[:, sl]
        ys.append((yf_ref[:, sl] + ybw_ref[:, sl]) * (z * jax.nn.sigmoid(z)))
    ms = (jnp.sum(ys[0] * ys[0], axis=-1, keepdims=True)
          + jnp.sum(ys[1] * ys[1], axis=-1, keepdims=True)) * (1.0 / GROUP_WIDTH)
    rs = lax.rsqrt(ms + EPS)
    yd = [(ys[grp] * rs * gs_ref[:, grp * LANES:(grp + 1) * LANES]).astype(MXU_DTYPE) for grp in range(2)]

    y = (_dot(ya_ref[...], wo_ref[0:GROUP_WIDTH, :])
         + _dot(yb_ref[...], wo_ref[GROUP_WIDTH:2 * GROUP_WIDTH, :])
         + _dot(jnp.concatenate(ym, axis=1), wo_ref[2 * GROUP_WIDTH:3 * GROUP_WIDTH, :])
         + _dot(jnp.concatenate(yd, axis=1), wo_ref[3 * GROUP_WIDTH:, :]))
    x1 = x_ref[...] + mod_ref[2:3, :] * y
    h2 = _rmsnorm_mod(x1, g2_ref[...], mod_ref[3:4, :], mod_ref[4:5, :]).astype(MXU_DTYPE)
    acc = jnp.zeros(x1.shape, F32)
    for f in range(D_FF // FF_CHUNK):
        sl = slice(f * FF_CHUNK, (f + 1) * FF_CHUNK)
        hf = jnp.maximum(_dot(h2, w1_ref[:, sl]), 0.0)
        acc = acc + _dot((hf * hf).astype(MXU_DTYPE), w2_ref[sl, :])
    x2 = x1 + mod_ref[5:6, :] * acc
    if final:
        ms2 = jnp.mean(x2 * x2, axis=-1, keepdims=True)
        x2 = x2 * lax.rsqrt(ms2 + EPS) * gfin_ref[...]
    out_ref[...] = x2


def _out_ffn(xc, ya, yb, hf, hb, yf, ybw, pf, gm, gs, modv, g2, wo, w1, w2, g_final, n_ctx, final):
    bsz, u, d = xc.shape
    q_off = n_ctx // ROW_TILE if final else 0
    nt = u // ROW_TILE - q_off
    row = lambda b, i: (b, i + q_off, 0)
    const = lambda b, i: (0, 0)
    grp_tile = pl.BlockSpec((None, ROW_TILE, GROUP_WIDTH), row)
    return pl.pallas_call(
        functools.partial(_out_ffn_kernel, final=final),
        grid=(bsz, nt),
        in_specs=[
            pl.BlockSpec((None, ROW_TILE, d), row),
            grp_tile, grp_tile, grp_tile, grp_tile,
            pl.BlockSpec((None, ROW_TILE, GROUP_WIDTH), lambda b, i: (b, i + q_off, _O_BLK)),
            grp_tile, grp_tile,
            pl.BlockSpec((None, ROW_TILE, GROUP_WIDTH), lambda b, i: (b, i + q_off, _Z_BLK)),
            pl.BlockSpec((1, GROUP_WIDTH), const),
            pl.BlockSpec((1, GROUP_WIDTH), const),
            pl.BlockSpec((None, None, 8, d), lambda b, i: (b, jnp.minimum(i + q_off, 1), 0, 0)),
            pl.BlockSpec((1, d), const),
            pl.BlockSpec((d, d), const, pipeline_mode=_RESIDENT),
            pl.BlockSpec((d, D_FF), const, pipeline_mode=_RESIDENT),
            pl.BlockSpec((D_FF, d), const, pipeline_mode=_RESIDENT),
            pl.BlockSpec((1, d), const),
        ],
        out_specs=pl.BlockSpec((None, ROW_TILE, d), lambda b, i: (b, i, 0)),
        out_shape=jax.ShapeDtypeStruct((bsz, nt * ROW_TILE, d), F32),
        compiler_params=_params("parallel", "parallel"),
        name="out_ffn_final" if final else "out_ffn",
    )(xc, ya, yb, hf, hb, pf, yf, ybw, pf, gm, gs, modv, g2, wo, w1, w2, g_final)


def _gate_starts():
    cg, ddt = _OFF[10], _OFF[15]
    return [cg, cg + N_HEADS, cg + 2 * N_HEADS, cg + 3 * N_HEADS, ddt, ddt + N_HEADS]


def _gate_lanes(groups):
    pad = jnp.zeros((8 - N_HEADS,), F32)
    parts = []
    for g in groups:
        parts += [g.astype(F32), pad]
    parts.append(jnp.zeros((LANES - 8 * len(groups),), F32))
    return jnp.concatenate(parts).reshape(1, LANES)


def _proj_weight(w_in):
    (aq, ak, av, bq, bk, bv, cq, ck, cv, co, cg, dx, dz, db, dc, ddt) = _OFF[:-1]
    qscale = HEAD_DIM ** -0.5
    segs = []
    for base, scale in ((aq, qscale), (bq, 1.0)):
        segs += [(base + h * HEAD_DIM, HEAD_DIM, scale) for h in _Q_HEAD_ORDER]
        segs += [(base + GROUP_WIDTH, 2 * LANES, 1.0)]
    segs += [(cq, GROUP_WIDTH, qscale), (ck, 2 * GROUP_WIDTH, 1.0)]
    segs += [(dx, GROUP_WIDTH, 1.0), (db, 2 * GROUP_WIDTH, 1.0), (co, GROUP_WIDTH, 1.0), (dz, GROUP_WIDTH, 1.0)]
    parts = [w_in[:, s:s + n] if scale == 1.0 else w_in[:, s:s + n] * scale for s, n, scale in segs]
    zeros4 = jnp.zeros((w_in.shape[0], 8 - N_HEADS), w_in.dtype)
    for start in _gate_starts():
        parts += [w_in[:, start:start + N_HEADS], zeros4]
    parts.append(jnp.zeros((w_in.shape[0], LANES - 8 * len(_gate_starts())), w_in.dtype))
    w = jnp.concatenate(parts, axis=1)
    assert w.shape[1] == _N_PROJ
    return w.astype(MXU_DTYPE)


def _out_weight(w_out):
    parts = []
    for base in (0, GROUP_WIDTH):
        parts += [w_out[base + h * HEAD_DIM:base + (h + 1) * HEAD_DIM] for h in _Q_HEAD_ORDER]
    parts.append(w_out[2 * GROUP_WIDTH:])
    return jnp.concatenate(parts, axis=0).astype(MXU_DTYPE)


def _rope_tables(n_tok, n_ctx):
    t = jnp.arange(n_tok)
    row = (t // GRID_W).astype(F32)
    col = (t % GRID_W).astype(F32)
    half = HEAD_DIM // 2
    inv_freq = ROPE_BASE ** (-jnp.arange(0, half, 2, dtype=F32) / half)
    lane = np.arange(LANES)
    hd = lane % HEAD_DIM
    use_col = (hd // half) == 1
    pos = jnp.where(use_col[None, :], col[:, None], row[:, None])
    ang = pos * inv_freq[hd % (half // 2)][None, :]
    first = (hd % half) < (half // 2)
    cos = jnp.cos(ang)
    sin = jnp.where(first[None, :], -jnp.sin(ang), jnp.sin(ang))
    cos = jnp.concatenate([jnp.ones((n_ctx, LANES), F32), cos], axis=0)
    sin = jnp.concatenate([jnp.zeros((n_ctx, LANES), F32), sin], axis=0)
    return cos, sin


def kernel(x, c, ctx, c_ctx, w_ada, b_ada, g_norm1, g_norm2, w_in, sink_a, g_q_b, g_k_b, b_igate, b_fgate,
           g_mlstm, conv_w, conv_b, a_log, dt_bias, d_skip, g_ssm, w_out, w_ff1, w_ff2, g_final):
    bsz, n_tok, d = x.shape
    n_ctx = ctx.shape[1]
    depth = w_in.shape[0]
    assert d == D_MODEL and n_ctx % ROW_TILE == 0 and n_tok % ROW_TILE == 0

    cos, sin = _rope_tables(n_tok, n_ctx)
    n_rows = 16
    cvec = jnp.concatenate([c, c_ctx[None, :], jnp.zeros((n_rows - bsz - 1, d), F32)], axis=0)
    mod_all = _ada_mod(cvec, w_ada, b_ada)

    xc = jnp.concatenate([ctx, x], axis=1)
    for layer in range(depth):
        need_ctx = layer < depth - 1
        mod = mod_all[layer].reshape(n_rows, 6, d)
        pad = jnp.zeros((bsz, 2, d), F32)
        mod_lat = jnp.concatenate([mod[:bsz], pad], axis=1)
        mod_ctx = jnp.broadcast_to(jnp.concatenate([mod[bsz], pad[0]], axis=0), (bsz, 8, d))
        modv = jnp.stack([mod_ctx, mod_lat], axis=1)

        gq = jnp.tile(g_q_b[layer] * (HEAD_DIM ** -0.5 * LOG2E), 2).reshape(1, LANES)
        gk = jnp.tile(g_k_b[layer], 2).reshape(1, LANES)
        pa, pb, pc, pf = _in_proj(xc, modv, g_norm1[layer].reshape(1, d), _proj_weight(w_in[layer]),
                                  cos, sin, gq, gk)

        ya = _attn_a(pa, sink_a[layer], n_ctx, need_ctx)
        yb = _attn_b(pb, n_ctx, need_ctx)

        gbias = _gate_lanes([b_igate[layer, 0], b_fgate[layer, 0], b_igate[layer, 1], b_fgate[layer, 1],
                             dt_bias[layer, 0], dt_bias[layer, 1]])
        hf, hb = _mlstm(pc, pf, gbias, n_ctx)

        alog = jnp.broadcast_to(jnp.pad(a_log[layer], ((0, 0), (0, 8 - N_HEADS)))[:, :, None], (2, 8, LANES))
        dsk = jnp.repeat(d_skip[layer], HEAD_DIM).reshape(1, GROUP_WIDTH)
        yf, ybw = _ssd(pf, gbias, alog, conv_w[layer], conv_b[layer].reshape(1, -1), dsk, n_ctx)

        xc = _out_ffn(xc, ya, yb, hf, hb, yf, ybw, pf, g_mlstm[layer].reshape(1, GROUP_WIDTH),
                      g_ssm[layer].reshape(1, GROUP_WIDTH), modv, g_norm2[layer].reshape(1, d),
                      _out_weight(w_out[layer]), w_ff1[layer].astype(MXU_DTYPE), w_ff2[layer].astype(MXU_DTYPE),
                      g_final.reshape(1, d), n_ctx, final=not need_ctx)
    return xc
```

```python
import functools
import math

import numpy as np
import jax
import jax.numpy as jnp
from jax import lax
from jax.experimental import pallas as pl
from jax.experimental.pallas import tpu as pltpu

F32 = jnp.float32
MXU_DTYPE = jnp.bfloat16

D_MODEL = 1024
HEAD_DIM = 64
LANES = 128
GRID_W = 64
WINDOW = 128
ROPE_BASE = 10000.0
EPS = 1e-6
N_HEADS = 4
GROUP_WIDTH = N_HEADS * HEAD_DIM
D_STATE = 128
CHUNK = 128
ROW_TILE = 256
CHUNKS_PER_TILE = ROW_TILE // CHUNK
KEY_TILE = 1024
SOFTMAX_ROWS = 64
D_FF = 4 * D_MODEL
FF_CHUNK = 1024
HALO = 8
VMEM_LIMIT = 56 * 1024 * 1024
LOG2E = math.log2(math.e)

_SPLIT_SIZES = (256, 128, 128, 256, 128, 128, 256, 256, 256, 256, 16, 256, 256, 256, 256, 8)
_OFF = [int(o) for o in np.concatenate([[0], np.cumsum(_SPLIT_SIZES)])]
_Q_HEAD_ORDER = (0, 2, 1, 3)

_COL_A = 0
_COL_B = 512
_COL_C = 1024
_COL_F = 1792
_N_F = 768 + 256 + 256 + 128
_N_PROJ = _COL_F + _N_F
_GATE_BLK = (_N_F - LANES) // LANES
_O_BLK = 768 // GROUP_WIDTH
_Z_BLK = (768 + GROUP_WIDTH) // GROUP_WIDTH


def _dot(a, b):
    return jnp.dot(a, b, preferred_element_type=F32)


def _dot_nt(a, b):
    return lax.dot_general(a, b, (((1,), (1,)), ((), ())), preferred_element_type=F32)


def _dot_tn(a, b):
    return lax.dot_general(a, b, (((0,), (0,)), ((), ())), preferred_element_type=F32)


def _split3(x):
    hi = x.astype(MXU_DTYPE)
    r1 = x - hi.astype(F32)
    mid = r1.astype(MXU_DTYPE)
    lo = (r1 - mid.astype(F32)).astype(MXU_DTYPE)
    return [hi, mid, lo]


def _rows_cumsum(rows, mt3):
    return _dot(jnp.concatenate(_split3(rows), axis=1), mt3)


def _rows_to_columns(rows, sel3):
    return _dot_tn(jnp.concatenate(_split3(rows), axis=0), sel3)


def _cummax_lanes(x, direction):
    n = x.shape[-1]
    lane = lax.broadcasted_iota(jnp.int32, x.shape, 1)
    shift = 1
    while shift < n:
        if direction == 0:
            moved = jnp.where(lane >= shift, pltpu.roll(x, shift, 1), -jnp.inf)
        else:
            moved = jnp.where(lane < n - shift, pltpu.roll(x, n - shift, 1), -jnp.inf)
        x = jnp.maximum(x, moved)
        shift *= 2
    return x


def _lane_is_low(shape):
    lane = lax.broadcasted_iota(jnp.int32, shape, len(shape) - 1)
    return (lane % LANES) < HEAD_DIM


def _head_mean_square(t, low):
    sq = t * t
    lo = jnp.sum(jnp.where(low, sq, 0.0), axis=-1, keepdims=True)
    hi = jnp.sum(jnp.where(low, 0.0, sq), axis=-1, keepdims=True)
    return jnp.where(low, lo, hi) * (1.0 / HEAD_DIM)


def _params(*sem):
    return pltpu.CompilerParams(dimension_semantics=sem, vmem_limit_bytes=VMEM_LIMIT)


_RESIDENT = pl.Buffered(1)


def _ada_kernel(c_ref, w_ref, b_ref, out_ref):
    cv = c_ref[...]
    cv = cv * jax.nn.sigmoid(cv)
    out_ref[...] = _dot(cv.astype(MXU_DTYPE), w_ref[...].astype(MXU_DTYPE)) + b_ref[...]


def _ada_mod(cvec, w_ada, b_ada):
    depth, d, n = w_ada.shape
    rows = cvec.shape[0]
    tn = 1536
    return pl.pallas_call(
        _ada_kernel,
        grid=(depth, n // tn),
        in_specs=[
            pl.BlockSpec((rows, d), lambda l, j: (0, 0)),
            pl.BlockSpec((None, d, tn), lambda l, j: (l, 0, j)),
            pl.BlockSpec((None, 1, tn), lambda l, j: (l, 0, j)),
        ],
        out_specs=pl.BlockSpec((None, rows, tn), lambda l, j: (l, 0, j)),
        out_shape=jax.ShapeDtypeStruct((depth, rows, n), F32),
        compiler_params=_params("parallel", "parallel"),
        name="ada_mod",
    )(cvec, w_ada, b_ada.reshape(depth, 1, n))


def _rmsnorm_mod(x, g, shift, scale):
    ms = jnp.mean(x * x, axis=-1, keepdims=True)
    return (x * lax.rsqrt(ms + EPS) * g) * (1.0 + scale) + shift


def _in_proj_kernel(xctx_ref, xlat_ref, mod_ref, g1_ref, w_ref, cos_ref, sin_ref, gq_ref, gk_ref,
                    a_ref, b_ref, c_ref, f_ref, *, nt_ctx):
    x = jnp.where(pl.program_id(1) < nt_ctx, xctx_ref[...], xlat_ref[...])
    hn = _rmsnorm_mod(x, g1_ref[...], mod_ref[0:1, :], mod_ref[1:2, :])
    hb = hn.astype(MXU_DTYPE)
    cos = cos_ref[...]
    sin = sin_ref[...]
    lane = lax.broadcasted_iota(jnp.int32, (1, LANES), 1)
    first = (lane % 32) < 16
    low = _lane_is_low((1, LANES))

    def rope(t):
        partner = jnp.where(first, pltpu.roll(t, LANES - 16, 1), pltpu.roll(t, 16, 1))
        return t * cos + partner * sin

    pa = _dot(hb, w_ref[:, _COL_A:_COL_A + 512])
    for j in range(3):
        a_ref[:, j * LANES:(j + 1) * LANES] = rope(pa[:, j * LANES:(j + 1) * LANES]).astype(a_ref.dtype)
    a_ref[:, 3 * LANES:] = pa[:, 3 * LANES:].astype(a_ref.dtype)

    pb = _dot(hb, w_ref[:, _COL_B:_COL_B + 512])
    for j in range(3):
        t = pb[:, j * LANES:(j + 1) * LANES]
        g = gq_ref[...] if j < 2 else gk_ref[...]
        t = t * lax.rsqrt(_head_mean_square(t, low) + EPS) * g
        b_ref[:, j * LANES:(j + 1) * LANES] = rope(t).astype(b_ref.dtype)
    b_ref[:, 3 * LANES:] = pb[:, 3 * LANES:].astype(b_ref.dtype)

    c_ref[...] = _dot(hb, w_ref[:, _COL_C:_COL_F]).astype(c_ref.dtype)
    f_ref[...] = _dot(hb, w_ref[:, _COL_F:])


def _residual_specs(nt_ctx, lat_off, q_off, d):
    ctx_spec = pl.BlockSpec((None, ROW_TILE, d), lambda b, i: (b, jnp.minimum(i + q_off, nt_ctx - 1), 0))
    lat_spec = pl.BlockSpec((None, ROW_TILE, d),
                            lambda b, i: (b, jnp.maximum(i + q_off - nt_ctx, 0) + lat_off, 0))
    return [ctx_spec, lat_spec]


def _mod_spec(nt_ctx, q_off, d):
    return pl.BlockSpec((None, None, 8, d), lambda b, i: (b, jnp.where(i + q_off < nt_ctx, 0, 1), 0, 0))


def _in_proj(x_ctx, x_lat, lat_off, u, modv, g1, w, cos, sin, gq, gk, n_ctx):
    bsz, _, d = x_lat.shape
    nt = u // ROW_TILE
    nt_ctx = n_ctx // ROW_TILE
    row = lambda b, i: (b, i, 0)
    const = lambda b, i: (0, 0)
    return pl.pallas_call(
        functools.partial(_in_proj_kernel, nt_ctx=nt_ctx),
        grid=(bsz, nt),
        in_specs=_residual_specs(nt_ctx, lat_off, 0, d) + [
            _mod_spec(nt_ctx, 0, d),
            pl.BlockSpec((1, d), const),
            pl.BlockSpec((d, _N_PROJ), const, pipeline_mode=_RESIDENT),
            pl.BlockSpec((ROW_TILE, LANES), lambda b, i: (i, 0)),
            pl.BlockSpec((ROW_TILE, LANES), lambda b, i: (i, 0)),
            pl.BlockSpec((1, LANES), const),
            pl.BlockSpec((1, LANES), const),
        ],
        out_specs=[
            pl.BlockSpec((None, ROW_TILE, 512), row),
            pl.BlockSpec((None, ROW_TILE, 512), row),
            pl.BlockSpec((None, ROW_TILE, 768), row),
            pl.BlockSpec((None, ROW_TILE, _N_F), row),
        ],
        out_shape=[
            jax.ShapeDtypeStruct((bsz, u, 512), MXU_DTYPE),
            jax.ShapeDtypeStruct((bsz, u, 512), MXU_DTYPE),
            jax.ShapeDtypeStruct((bsz, u, 768), MXU_DTYPE),
            jax.ShapeDtypeStruct((bsz, u, _N_F), F32),
        ],
        compiler_params=_params("parallel", "parallel"),
        name="in_proj",
    )(x_ctx, x_lat, modv, g1, w, cos, sin, gq, gk)


def _stack_heads(q_ref, qs_ref):
    low = _lane_is_low((1, LANES))
    for jb in range(2):
        q = q_ref[:, jb * LANES:(jb + 1) * LANES]
        for half in range(2):
            idx = 2 * jb + half
            keep = low if half == 0 else jnp.logical_not(low)
            qs_ref[idx * ROW_TILE:(idx + 1) * ROW_TILE, :] = jnp.where(keep, q, jnp.zeros_like(q))


def _unstack_heads(o, out_ref):
    low = _lane_is_low((1, LANES))
    for jb in range(2):
        lo = o[(2 * jb) * ROW_TILE:(2 * jb + 1) * ROW_TILE, :]
        hi = o[(2 * jb + 1) * ROW_TILE:(2 * jb + 2) * ROW_TILE, :]
        out_ref[:, jb * LANES:(jb + 1) * LANES] = jnp.where(low, lo, hi).astype(out_ref.dtype)


def _attn_a_kernel(sink_ref, q_ref, kp_ref, ko_ref, kn_ref, kc_ref, vp_ref, vo_ref, vn_ref, vc_ref,
                   out_ref, qs_ref, kbuf, vbuf, s_ref, o_ref, *, n_tok, n_ctx, q_off):
    i = pl.program_id(1) + q_off
    nb = 2 * WINDOW + ROW_TILE
    nk = nb + n_ctx
    kbuf[0:WINDOW] = kp_ref[...]
    kbuf[WINDOW:WINDOW + ROW_TILE] = ko_ref[...]
    kbuf[WINDOW + ROW_TILE:nb] = kn_ref[...]
    kbuf[nb:nk] = kc_ref[...]
    vbuf[0:WINDOW] = vp_ref[...]
    vbuf[WINDOW:WINDOW + ROW_TILE] = vo_ref[...]
    vbuf[WINDOW + ROW_TILE:nb] = vn_ref[...]
    vbuf[nb:nk] = vc_ref[...]
    _stack_heads(q_ref, qs_ref)

    r = lax.broadcasted_iota(jnp.int32, (ROW_TILE, nk), 0)
    c = lax.broadcasted_iota(jnp.int32, (ROW_TILE, nk), 1)
    kpos = (i - 1) * ROW_TILE - WINDOW + c
    dist = c - r
    band = jnp.where(dist >= 0, jnp.where(dist <= 2 * WINDOW, 1, 0), 0)
    band = jnp.where(kpos >= 0, jnp.where(kpos < n_tok, band, 0), 0)
    band = jnp.where(i >= 1, band, 0)
    bias = jnp.where(c >= nb, 0.0, jnp.where(band > 0, 0.0, -jnp.inf)).astype(F32)

    for idx in range(N_HEADS):
        rows = slice(idx * ROW_TILE, (idx + 1) * ROW_TILE)
        s_ref[rows, :] = _dot_nt(qs_ref[rows, :], kbuf[...]) + bias
    for idx in range(N_HEADS):
        rows = slice(idx * ROW_TILE, (idx + 1) * ROW_TILE)
        sink = sink_ref[_Q_HEAD_ORDER[idx]]
        m = jnp.maximum(jnp.max(s_ref[rows, :], axis=-1, keepdims=True), sink)
        p = jnp.exp(s_ref[rows, :] - m)
        l = jnp.sum(p, axis=-1, keepdims=True) + jnp.exp(sink - m)
        o_ref[rows, :] = _dot(p.astype(MXU_DTYPE), vbuf[...]) / l
    _unstack_heads(o_ref[...], out_ref)


def _attn_a(pa, sink, n_ctx, need_ctx):
    bsz, u, _ = pa.shape
    n_tok = u - n_ctx
    q_off = 0 if need_ctx else n_ctx // ROW_TILE
    nt = u // ROW_TILE - q_off
    last_blk = u // WINDOW - 1
    rpw = ROW_TILE // WINDOW
    nk = 2 * WINDOW + ROW_TILE + n_ctx

    def own(col):
        return lambda b, i: (b, i + q_off, col)

    def prev(col):
        return lambda b, i: (b, jnp.maximum((i + q_off) * rpw - 1, 0), col)

    def nxt(col):
        return lambda b, i: (b, jnp.minimum((i + q_off + 1) * rpw, last_blk), col)

    def ctx(col):
        return lambda b, i: (b, 0, col)

    kern = functools.partial(_attn_a_kernel, n_tok=n_tok, n_ctx=n_ctx, q_off=q_off)
    return pl.pallas_call(
        kern,
        grid=(bsz, nt),
        in_specs=[
            pl.BlockSpec(memory_space=pltpu.SMEM),
            pl.BlockSpec((None, ROW_TILE, 2 * LANES), own(0)),
            pl.BlockSpec((None, WINDOW, LANES), prev(2)),
            pl.BlockSpec((None, ROW_TILE, LANES), own(2)),
            pl.BlockSpec((None, WINDOW, LANES), nxt(2)),
            pl.BlockSpec((None, n_ctx, LANES), ctx(2)),
            pl.BlockSpec((None, WINDOW, LANES), prev(3)),
            pl.BlockSpec((None, ROW_TILE, LANES), own(3)),
            pl.BlockSpec((None, WINDOW, LANES), nxt(3)),
            pl.BlockSpec((None, n_ctx, LANES), ctx(3)),
        ],
        out_specs=pl.BlockSpec((None, ROW_TILE, 2 * LANES), lambda b, i: (b, i, 0)),
        out_shape=jax.ShapeDtypeStruct((bsz, nt * ROW_TILE, 2 * LANES), MXU_DTYPE),
        scratch_shapes=[pltpu.VMEM((N_HEADS * ROW_TILE, LANES), MXU_DTYPE),
                        pltpu.VMEM((nk, LANES), MXU_DTYPE), pltpu.VMEM((nk, LANES), MXU_DTYPE),
                        pltpu.VMEM((N_HEADS * ROW_TILE, nk), F32),
                        pltpu.VMEM((N_HEADS * ROW_TILE, LANES), F32)],
        compiler_params=_params("parallel", "parallel"),
        name="attn_window",
    )(sink, pa, pa, pa, pa, pa, pa, pa, pa, pa)


def _attn_b_kernel(q_ref, k_ref, v_ref, out_ref, qs_ref, vext_ref, s_ref, p_ref, m_ref, alpha_ref, acc_ref,
                   *, n_ctx, n_tok, q_off, tk):
    qi = pl.program_id(1) + q_off
    n_rows = N_HEADS * ROW_TILE

    @pl.when(pl.program_id(1) == 0)
    def _():
        vext_ref[:, 0:LANES] = v_ref[...]
        vext_ref[:, LANES:] = jnp.ones((vext_ref.shape[0], LANES), vext_ref.dtype)

    _stack_heads(q_ref, qs_ref)

    def scores(slot, start, size):
        s_ref[slot, :, 0:size] = _dot_nt(qs_ref[...], k_ref[start:start + size, :])

    def softmax(slot, size, first):
        m_new = jnp.max(s_ref[slot, :, 0:size], axis=-1, keepdims=True)
        if not first:
            m_old = m_ref[...]
            m_new = jnp.maximum(m_old, m_new)
            alpha_ref[slot] = jnp.exp2(m_old - m_new)
        m_ref[...] = m_new
        p_ref[slot, :, 0:size] = jnp.exp2(s_ref[slot, :, 0:size] - m_new).astype(p_ref.dtype)

    def accumulate(slot, start, size, first):
        pv = _dot(p_ref[slot, :, 0:size], vext_ref[start:start + size, :])
        if first:
            acc_ref[...] = pv
        else:
            acc_ref[...] = alpha_ref[slot] * acc_ref[...] + pv

    def attend(tiles):
        scores(0, *tiles[0])
        for t, (start, size) in enumerate(tiles):
            if t + 1 < len(tiles):
                scores((t + 1) % 2, *tiles[t + 1])
            softmax(t % 2, size, t == 0)
            accumulate(t % 2, start, size, t == 0)
        _unstack_heads(acc_ref[:, 0:LANES] / acc_ref[:, LANES:], out_ref)

    ctx_tiles = [(0, n_ctx)]
    all_tiles = ctx_tiles + [(n_ctx + t * tk, tk) for t in range(n_tok // tk)]
    if q_off == 0:
        pl.when(qi == 0)(functools.partial(attend, ctx_tiles))
        pl.when(qi > 0)(functools.partial(attend, all_tiles))
    else:
        attend(all_tiles)


def _attn_b(pb, n_ctx, need_ctx):
    bsz, u, _ = pb.shape
    n_tok = u - n_ctx
    q_off = 0 if need_ctx else n_ctx // ROW_TILE
    nt = u // ROW_TILE - q_off
    tk = min(KEY_TILE, n_tok)
    assert n_tok % tk == 0
    kern = functools.partial(_attn_b_kernel, n_ctx=n_ctx, n_tok=n_tok, q_off=q_off, tk=tk)
    return pl.pallas_call(
        kern,
        grid=(bsz, nt),
        in_specs=[
            pl.BlockSpec((None, ROW_TILE, 2 * LANES), lambda b, i: (b, i + q_off, 0)),
            pl.BlockSpec((None, u, LANES), lambda b, i: (b, 0, 2)),
            pl.BlockSpec((None, u, LANES), lambda b, i: (b, 0, 3)),
        ],
        out_specs=pl.BlockSpec((None, ROW_TILE, 2 * LANES), lambda b, i: (b, i, 0)),
        out_shape=jax.ShapeDtypeStruct((bsz, nt * ROW_TILE, 2 * LANES), MXU_DTYPE),
        scratch_shapes=[
            pltpu.VMEM((N_HEADS * ROW_TILE, LANES), MXU_DTYPE),
            pltpu.VMEM((u, 2 * LANES), MXU_DTYPE),
            pltpu.VMEM((2, N_HEADS * ROW_TILE, tk), F32),
            pltpu.VMEM((2, N_HEADS * ROW_TILE, tk), MXU_DTYPE),
            pltpu.VMEM((N_HEADS * ROW_TILE, 1), F32),
            pltpu.VMEM((2, N_HEADS * ROW_TILE, 1), F32),
            pltpu.VMEM((N_HEADS * ROW_TILE, 2 * LANES), F32),
        ],
        compiler_params=_params("parallel", "arbitrary"),
        name="attn_dense",
    )(pb, pb, pb)


def _scan_tile(j, direction, nt_ctx, nt):
    if direction == 0:
        return j
    return jnp.where(j < nt_ctx, nt_ctx - 1 - j, nt + nt_ctx - 1 - j)


def _causal_mask(direction):
    t = lax.broadcasted_iota(jnp.int32, (CHUNK, CHUNK), 0)
    s = lax.broadcasted_iota(jnp.int32, (CHUNK, CHUNK), 1)
    return (s <= t) if direction == 0 else (s >= t)


def _chunk_order(direction):
    order = range(CHUNKS_PER_TILE)
    return order if direction == 0 else reversed(order)


def _log_sigmoid(x):
    return jnp.minimum(x, 0.0) - jnp.log1p(jnp.exp(-jnp.abs(x)))


def _softplus(x):
    return jnp.maximum(x, 0.0) + jnp.log1p(jnp.exp(-jnp.abs(x)))


def _mlstm_gates(direction, g_rows, m_prev, mt3):
    end = CHUNK - 1 if direction == 0 else 0
    i_r = jnp.concatenate([g[16 * direction:16 * direction + 8] for g in g_rows], axis=0)
    f_r = jnp.concatenate([g[16 * direction + 8:16 * direction + 16] for g in g_rows], axis=0)
    b_r = _rows_cumsum(_log_sigmoid(f_r), mt3)
    r_r = i_r - b_r
    cmax = _cummax_lanes(r_r, direction)
    tot = jnp.broadcast_to(b_r[:, end:end + 1], b_r.shape)
    g2 = tot - b_r + i_r
    g2max = jnp.broadcast_to(jnp.max(g2, axis=-1, keepdims=True), g2.shape)
    per_chunk = []
    for idx in range(len(g_rows)):
        sl = slice(8 * idx, 8 * idx + 8)
        c_r = jnp.maximum(m_prev, cmax[sl])
        dp_r = jnp.exp(m_prev - c_r)
        em_r = jnp.exp(-(b_r[sl] + c_r))
        m_new = jnp.maximum(tot[sl] + m_prev, g2max[sl])
        wk_r = jnp.exp(g2[sl] - m_new)
        cd_r = jnp.exp(tot[sl] + m_prev - m_new)
        per_chunk.append((jnp.concatenate([c_r, dp_r, em_r, wk_r], axis=0), r_r[sl], cd_r))
        m_prev = m_new
    return per_chunk, m_prev


def _mlstm_scores(qkvs):
    low1 = _lane_is_low((1, LANES))
    ones = jnp.ones((CHUNK, LANES), MXU_DTYPE)
    pre = []
    for qkv in qkvs:
        per_pair = []
        for pair in range(2):
            q = qkv[:, pair * LANES:(pair + 1) * LANES]
            k = qkv[:, GROUP_WIDTH + pair * LANES:GROUP_WIDTH + (pair + 1) * LANES]
            v = qkv[:, 2 * GROUP_WIDTH + pair * LANES:2 * GROUP_WIDTH + (pair + 1) * LANES]
            qk = [_dot_nt(jnp.where(low1 if half == 0 else jnp.logical_not(low1), q, jnp.zeros_like(q)), k)
                  for half in range(2)]
            per_pair.append((q, k, jnp.concatenate([v, ones], axis=1), qk))
        pre.append(per_pair)
    return pre


def _mlstm_chunks(items, pre, state):
    low1 = _lane_is_low((1, LANES))
    low2 = _lane_is_low((1, 2 * LANES))
    ones = jnp.ones((CHUNK, LANES), MXU_DTYPE)
    r2 = lax.broadcasted_iota(jnp.int32, (LANES, 2 * LANES), 0)
    c2 = lax.broadcasted_iota(jnp.int32, (LANES, 2 * LANES), 1)
    state_mask = (r2 < HEAD_DIM) == ((c2 % LANES) < HEAD_DIM)

    mid = []
    for (direction, qkv, r_r, cd_r, cols), per_pair in zip(items, pre):
        mask = _causal_mask(direction)
        res = []
        for pair, (q, k, vext, qk) in enumerate(per_pair):
            svs, dens = [], []
            for half in range(2):
                h = 2 * pair + half
                w = jnp.exp(jnp.where(mask, r_r[h:h + 1, :] - cols[:, h * LANES:(h + 1) * LANES], -jnp.inf))
                s = qk[half] * w
                s_hi = s.astype(MXU_DTYPE)
                s_lo = (s - s_hi.astype(F32)).astype(MXU_DTYPE)
                sv = _dot(s_hi, vext)
                svs.append(sv[:, :LANES])
                dens.append(sv[:, LANES:] + _dot(s_lo, ones))
            wk = cols[:, (8 + pair) * LANES:(9 + pair) * LANES]
            upd = _dot_tn((k.astype(F32) * wk).astype(MXU_DTYPE), vext)
            res.append((jnp.where(low1, svs[0], svs[1]), jnp.where(low1, dens[0], dens[1]),
                        jnp.where(state_mask, upd, 0.0)))
        mid.append(res)

    outs = []
    for (direction, qkv, r_r, cd_r, cols), per_pair, res in zip(items, pre, mid):
        out = []
        for pair in range(2):
            q = per_pair[pair][0]
            sv, den_intra, upd = res[pair]
            prior = _dot(q, state[direction][pair].astype(MXU_DTYPE))
            dp = cols[:, (4 + pair) * LANES:(5 + pair) * LANES]
            em = cols[:, (6 + pair) * LANES:(7 + pair) * LANES]
            num = sv + dp * prior[:, :LANES]
            den = den_intra + dp * prior[:, LANES:]
            out.append(num / jnp.maximum(jnp.abs(den), em))
            cd0 = jnp.concatenate([cd_r[2 * pair:2 * pair + 1]] * 2, axis=1)
            cd1 = jnp.concatenate([cd_r[2 * pair + 1:2 * pair + 2]] * 2, axis=1)
            state[direction][pair] = state[direction][pair] * jnp.where(low2, cd0, cd1) + upd
        outs.append(out)
    return outs


def _mlstm_body(qkvf_ref, gf_ref, qkvb_ref, gb_ref, gbias_ref, mt3_ref, sel3_ref, hf_ref, hb_ref,
                st_ref, m_ref):
    gbias = gbias_ref[...]
    dirs = ((qkvf_ref, gf_ref, hf_ref), (qkvb_ref, gb_ref, hb_ref))
    work = [(direction, idx) for idx in range(CHUNKS_PER_TILE) for direction in range(2)]
    row_slices = []
    for direction, idx in work:
        ci = list(_chunk_order(direction))[idx]
        row_slices.append(slice(ci * CHUNK, (ci + 1) * CHUNK))
    qkvs = [dirs[direction][0][rows, :] for (direction, _), rows in zip(work, row_slices)]
    pre = _mlstm_scores(qkvs)
    gates = []
    for direction, (_, g_ref, _) in enumerate(dirs):
        g_rows = [(g_ref[ci * CHUNK:(ci + 1) * CHUNK, :] + gbias).T for ci in _chunk_order(direction)]
        per_chunk, m_new = _mlstm_gates(direction, g_rows, m_ref[direction], mt3_ref[direction])
        m_ref[direction] = m_new
        gates.append(per_chunk)
    rows_all = jnp.concatenate([gates[direction][idx][0] for direction, idx in work], axis=1)
    cols_all = _rows_to_columns(rows_all, sel3_ref[...])

    state = [[st_ref[direction, pair] for pair in range(2)] for direction in range(2)]
    items = []
    for n, (direction, idx) in enumerate(work):
        _, r_r, cd_r = gates[direction][idx]
        items.append((direction, qkvs[n], r_r, cd_r, cols_all[n * CHUNK:(n + 1) * CHUNK, :]))
    outs = _mlstm_chunks(items, pre, state)
    for (direction, *_), rows, out in zip(items, row_slices, outs):
        for pair in range(2):
            dirs[direction][2][rows, pair * LANES:(pair + 1) * LANES] = out[pair]
    for direction in range(2):
        for pair in range(2):
            st_ref[direction, pair] = state[direction][pair]


def _scan_masks():
    s = np.arange(CHUNK)[:, None]
    t = np.arange(CHUNK)[None, :]
    mats = [np.tile((s <= t).astype(np.float32), (3, 1)), np.tile((s >= t).astype(np.float32), (3, 1))]
    return jnp.asarray(np.stack(mats), MXU_DTYPE)


def _column_selector(n_head_groups, n_pair_groups):
    n_groups = n_head_groups + n_pair_groups
    n_rows = 8 * n_groups + (-8 * n_groups) % 16
    n_cols = (N_HEADS * n_head_groups + 2 * n_pair_groups) * LANES
    sel = np.zeros((n_rows, n_cols), np.float32)
    col = 0
    for g in range(n_head_groups):
        for h in range(N_HEADS):
            sel[8 * g + h, col:col + LANES] = 1.0
            col += LANES
    for g in range(n_head_groups, n_groups):
        for pair in range(2):
            sel[8 * g + 2 * pair, col:col + HEAD_DIM] = 1.0
            sel[8 * g + 2 * pair + 1, col + HEAD_DIM:col + LANES] = 1.0
            col += LANES
    return jnp.asarray(np.tile(sel, (3, 1)), MXU_DTYPE)


def _ssd_conv(x_ref, xp_ref, xn_ref, cw_ref, cb_ref, tile, nt_ctx, nt):
    xin = x_ref[...]
    has_prev = (tile != 0) & (tile != nt_ctx)
    has_next = (tile != nt_ctx - 1) & (tile != nt - 1)
    prow = jnp.where(has_prev, xp_ref[HALO - 1:HALO, :], 0.0)
    nrow = jnp.where(has_next, xn_ref[0:1, :], 0.0)
    ridx = lax.broadcasted_iota(jnp.int32, (ROW_TILE, 1), 0)
    up = jnp.where(ridx == 0, prow, pltpu.roll(xin, 1, 0))
    dn = jnp.where(ridx == ROW_TILE - 1, nrow, pltpu.roll(xin, ROW_TILE - 1, 0))
    u = cw_ref[0:1, :] * up + cw_ref[1:2, :] * xin + cw_ref[2:3, :] * dn + cb_ref[...]
    return u * jax.nn.sigmoid(u)


def _ssd_gates(direction, g_rows, neg_a, mt3):
    end = CHUNK - 1 if direction == 0 else 0
    n = len(g_rows)
    dt_r = _softplus(jnp.concatenate([g[32 + 8 * direction:40 + 8 * direction] for g in g_rows], axis=0))
    cum_r = _rows_cumsum(dt_r * jnp.concatenate([neg_a] * n, axis=0), mt3)
    tot = jnp.broadcast_to(cum_r[:, end:end + 1], cum_r.shape)
    e_r = jnp.exp(cum_r)
    wend_r = jnp.exp(tot - cum_r) * dt_r
    d_r = jnp.exp(tot)
    zero = jnp.zeros((8, CHUNK), F32)
    per_chunk = []
    for idx in range(n):
        sl = slice(8 * idx, 8 * idx + 8)
        per_chunk.append((jnp.concatenate([cum_r[sl], e_r[sl], wend_r[sl], zero], axis=0),
                          cum_r[sl], dt_r[sl], d_r[sl]))
    return per_chunk


def _ssd_chunks(items, state):
    low1 = _lane_is_low((1, LANES))

    pre = []
    for direction, u, cum_r, dt_r, d_r, cols in items:
        per_grp = []
        for grp in range(2):
            bm = u[:, GROUP_WIDTH + grp * D_STATE:GROUP_WIDTH + (grp + 1) * D_STATE].astype(MXU_DTYPE)
            cm = u[:, 2 * GROUP_WIDTH + grp * D_STATE:2 * GROUP_WIDTH + (grp + 1) * D_STATE].astype(MXU_DTYPE)
            x_pair = u[:, grp * LANES:(grp + 1) * LANES]
            per_grp.append((bm, cm, x_pair, _dot_nt(cm, bm)))
        pre.append(per_grp)

    mid = []
    for (direction, u, cum_r, dt_r, d_r, cols), per_grp in zip(items, pre):
        mask = _causal_mask(direction)
        res = []
        for grp, (bm, cm, x_pair, gmat) in enumerate(per_grp):
            xb = x_pair.astype(MXU_DTYPE)
            ys = []
            for half in range(2):
                h = 2 * grp + half
                decay = jnp.exp(jnp.where(mask, cols[:, h * LANES:(h + 1) * LANES] - cum_r[h:h + 1, :], -jnp.inf))
                s = gmat * decay * dt_r[h:h + 1, :]
                ys.append(_dot(s.astype(MXU_DTYPE), xb))
            w_pair = cols[:, (6 + grp) * LANES:(7 + grp) * LANES]
            upd = _dot_tn(bm, (x_pair * w_pair).astype(MXU_DTYPE))
            res.append((jnp.where(low1, ys[0], ys[1]), upd))
        mid.append(res)

    outs = []
    for (direction, u, cum_r, dt_r, d_r, cols), per_grp, res in zip(items, pre, mid):
        out = []
        for grp in range(2):
            cm = per_grp[grp][1]
            y_intra, upd = res[grp]
            ch = _dot(cm, state[direction][grp].astype(MXU_DTYPE))
            e_pair = cols[:, (4 + grp) * LANES:(5 + grp) * LANES]
            out.append(y_intra + e_pair * ch)
            d_pair = jnp.where(low1, d_r[2 * grp:2 * grp + 1], d_r[2 * grp + 1:2 * grp + 2])
            state[direction][grp] = state[direction][grp] * d_pair + upd
        outs.append(out)
    return outs


def _ssd_convs(xf_ref, xfp_ref, xfn_ref, xb_ref, xbp_ref, xbn_ref, cw_ref, cb_ref, *, nt_ctx, nt):
    j = pl.program_id(1)
    convs = []
    for direction, (x_ref, xp_ref, xn_ref) in enumerate(((xf_ref, xfp_ref, xfn_ref), (xb_ref, xbp_ref, xbn_ref))):
        tile = _scan_tile(j, direction, nt_ctx, nt)
        convs.append(_ssd_conv(x_ref, xp_ref, xn_ref, cw_ref, cb_ref, tile, nt_ctx, nt))
    return convs


def _ssd_body(convs, gf_ref, gb_ref, gbias_ref, alog_ref, dskip_ref, mt3_ref, sel3_ref, yf_ref, yb_ref, st_ref):
    gbias = gbias_ref[...]
    dirs = ((None, None, None, gf_ref, yf_ref), (None, None, None, gb_ref, yb_ref))
    gates = []
    g_rows = [[(g_ref[ci * CHUNK:(ci + 1) * CHUNK, :] + gbias).T for ci in _chunk_order(direction)]
              for direction, (_, _, _, g_ref, _) in enumerate(dirs)]
    for direction in range(2):
        gates.append(_ssd_gates(direction, g_rows[direction], -jnp.exp(alog_ref[direction]), mt3_ref[direction]))
    work = [(direction, idx) for idx in range(CHUNKS_PER_TILE) for direction in range(2)]
    rows_all = jnp.concatenate([gates[direction][idx][0] for direction, idx in work], axis=1)
    cols_all = _rows_to_columns(rows_all, sel3_ref[...])

    state = [[st_ref[direction, grp] for grp in range(2)] for direction in range(2)]
    items, row_slices = [], []
    for n, (direction, idx) in enumerate(work):
        ci = list(_chunk_order(direction))[idx]
        rows = slice(ci * CHUNK, (ci + 1) * CHUNK)
        _, cum_r, dt_r, d_r = gates[direction][idx]
        items.append((direction, convs[direction][rows, :], cum_r, dt_r, d_r,
                      cols_all[n * CHUNK:(n + 1) * CHUNK, :]))
        row_slices.append(rows)
    outs = _ssd_chunks(items, state)
    for (direction, u, *_), rows, out in zip(items, row_slices, outs):
        for grp in range(2):
            sl = slice(grp * LANES, (grp + 1) * LANES)
            y = out[grp]
            if direction == 0:
                y = y + dskip_ref[:, sl] * u[:, sl]
            dirs[direction][4][rows, sl] = y
    for direction in range(2):
        for grp in range(2):
            st_ref[direction, grp] = state[direction][grp]


def _scans_kernel(qkvf_ref, qkvb_ref, gf_ref, gb_ref, xf_ref, xfp_ref, xfn_ref, xb_ref, xbp_ref, xbn_ref,
                  gbias_ref, alog_ref, cw_ref, cb_ref, dskip_ref, mt3_ref, selm_ref, sels_ref,
                  hf_ref, hb_ref, yf_ref, yb_ref, stm_ref, m_ref, sts_ref, *, nt_ctx, nt):
    @pl.when(pl.program_id(1) == 0)
    def _():
        stm_ref[...] = jnp.zeros(stm_ref.shape, F32)
        m_ref[...] = jnp.zeros(m_ref.shape, F32)
        sts_ref[...] = jnp.zeros(sts_ref.shape, F32)

    convs = _ssd_convs(xf_ref, xfp_ref, xfn_ref, xb_ref, xbp_ref, xbn_ref, cw_ref, cb_ref, nt_ctx=nt_ctx, nt=nt)
    _mlstm_body(qkvf_ref, gf_ref, qkvb_ref, gb_ref, gbias_ref, mt3_ref, selm_ref, hf_ref, hb_ref, stm_ref, m_ref)
    _ssd_body(convs, gf_ref, gb_ref, gbias_ref, alog_ref, dskip_ref, mt3_ref, sels_ref, yf_ref, yb_ref, sts_ref)


def _scans(pc, pf, gbias, alog, conv_w, conv_b, d_skip, n_ctx):
    bsz, u, _ = pf.shape
    nt = u // ROW_TILE
    nt_ctx = n_ctx // ROW_TILE
    hpt = ROW_TILE // HALO
    last_halo = u // HALO - 1
    const = lambda b, j: (0, 0)
    const3 = lambda b, j: (0, 0, 0)
    tile = [lambda b, j, d=d: _scan_tile(j, d, nt_ctx, nt) for d in range(2)]

    def tile_spec(direction, width, col):
        return pl.BlockSpec((None, ROW_TILE, width), lambda b, j: (b, tile[direction](b, j), col))

    def conv_specs(direction):
        t = tile[direction]
        return [
            tile_spec(direction, 768, 0),
            pl.BlockSpec((None, HALO, 768), lambda b, j: (b, jnp.maximum(t(b, j) * hpt - 1, 0), 0)),
            pl.BlockSpec((None, HALO, 768), lambda b, j: (b, jnp.minimum((t(b, j) + 1) * hpt, last_halo), 0)),
        ]

    out = jax.ShapeDtypeStruct((bsz, u, GROUP_WIDTH), F32)
    mt3 = _scan_masks()
    sel_m = _column_selector(1, 3)
    sel_s = _column_selector(1, 2)
    return pl.pallas_call(
        functools.partial(_scans_kernel, nt_ctx=nt_ctx, nt=nt),
        grid=(bsz, nt),
        in_specs=[tile_spec(0, 768, 0), tile_spec(1, 768, 0),
                  tile_spec(0, LANES, _GATE_BLK), tile_spec(1, LANES, _GATE_BLK)]
        + conv_specs(0) + conv_specs(1) + [
            pl.BlockSpec((1, LANES), const),
            pl.BlockSpec((2, 8, LANES), const3),
            pl.BlockSpec((3, 768), const),
            pl.BlockSpec((1, 768), const),
            pl.BlockSpec((1, GROUP_WIDTH), const),
            pl.BlockSpec(mt3.shape, const3),
            pl.BlockSpec(sel_m.shape, const),
            pl.BlockSpec(sel_s.shape, const),
        ],
        out_specs=[tile_spec(0, GROUP_WIDTH, 0), tile_spec(1, GROUP_WIDTH, 0),
                   tile_spec(0, GROUP_WIDTH, 0), tile_spec(1, GROUP_WIDTH, 0)],
        out_shape=[out, out, out, out],
        scratch_shapes=[pltpu.VMEM((2, 2, LANES, 2 * LANES), F32), pltpu.VMEM((2, 8, LANES), F32),
                        pltpu.VMEM((2, 2, D_STATE, LANES), F32)],
        compiler_params=_params("parallel", "arbitrary"),
        name="scans",
    )(pc, pc, pf, pf, pf, pf, pf, pf, pf, pf, gbias, alog, conv_w, conv_b, d_skip, mt3, sel_m, sel_s)


def _out_ffn_kernel(xctx_ref, xlat_ref, ya_ref, yb_ref, hf_ref, hb_ref, o_ref, yf_ref, ybw_ref, z_ref,
                    gm_ref, gs_ref, mod_ref, g2_ref, wo_ref, w1_ref, w2_ref, gfin_ref, out_ref,
                    *, final, nt_ctx, q_off):
    if q_off >= nt_ctx:
        x_in = xlat_ref[...]
    else:
        x_in = jnp.where(pl.program_id(1) + q_off < nt_ctx, xctx_ref[...], xlat_ref[...])
    low = _lane_is_low((1, LANES))
    ym = []
    for pair in range(2):
        sl = slice(pair * LANES, (pair + 1) * LANES)
        h = hf_ref[:, sl] + hb_ref[:, sl]
        hn = h * lax.rsqrt(_head_mean_square(h, low) + EPS) * gm_ref[:, sl]
        ym.append((hn * jax.nn.sigmoid(o_ref[:, sl])).astype(MXU_DTYPE))
    ys = []
    for grp in range(2):
        sl = slice(grp * LANES, (grp + 1) * LANES)
        z = z_ref[:, sl]
        ys.append((yf_ref[:, sl] + ybw_ref[:, sl]) * (z * jax.nn.sigmoid(z)))
    ms = (jnp.sum(ys[0] * ys[0], axis=-1, keepdims=True)
          + jnp.sum(ys[1] * ys[1], axis=-1, keepdims=True)) * (1.0 / GROUP_WIDTH)
    rs = lax.rsqrt(ms + EPS)
    yd = [(ys[grp] * rs * gs_ref[:, grp * LANES:(grp + 1) * LANES]).astype(MXU_DTYPE) for grp in range(2)]

    y = (_dot(ya_ref[...], wo_ref[0:GROUP_WIDTH, :])
         + _dot(yb_ref[...], wo_ref[GROUP_WIDTH:2 * GROUP_WIDTH, :])
         + _dot(jnp.concatenate(ym, axis=1), wo_ref[2 * GROUP_WIDTH:3 * GROUP_WIDTH, :])
         + _dot(jnp.concatenate(yd, axis=1), wo_ref[3 * GROUP_WIDTH:, :]))
    x1 = x_in + mod_ref[2:3, :] * y
    h2 = _rmsnorm_mod(x1, g2_ref[...], mod_ref[3:4, :], mod_ref[4:5, :]).astype(MXU_DTYPE)
    acc = jnp.zeros(x1.shape, F32)
    for f in range(D_FF // FF_CHUNK):
        sl = slice(f * FF_CHUNK, (f + 1) * FF_CHUNK)
        hf = jnp.maximum(_dot(h2, w1_ref[:, sl]), 0.0)
        acc = acc + _dot((hf * hf).astype(MXU_DTYPE), w2_ref[sl, :])
    x2 = x1 + mod_ref[5:6, :] * acc
    if final:
        ms2 = jnp.mean(x2 * x2, axis=-1, keepdims=True)
        x2 = x2 * lax.rsqrt(ms2 + EPS) * gfin_ref[...]
    out_ref[...] = x2


def _out_ffn(x_ctx, x_lat, lat_off, ya, yb, hf, hb, yf, ybw, pf, gm, gs, modv, g2, wo, w1, w2, g_final,
             n_ctx, final):
    bsz, u, _ = pf.shape
    d = x_lat.shape[-1]
    nt_ctx = n_ctx // ROW_TILE
    q_off = nt_ctx if final else 0
    nt = u // ROW_TILE - q_off
    row = lambda b, i: (b, i + q_off, 0)
    const = lambda b, i: (0, 0)
    grp_tile = pl.BlockSpec((None, ROW_TILE, GROUP_WIDTH), row)
    attn_tile = pl.BlockSpec((None, ROW_TILE, GROUP_WIDTH), lambda b, i: (b, i, 0))
    return pl.pallas_call(
        functools.partial(_out_ffn_kernel, final=final, nt_ctx=nt_ctx, q_off=q_off),
        grid=(bsz, nt),
        in_specs=_residual_specs(nt_ctx, lat_off, q_off, d) + [
            attn_tile, attn_tile, grp_tile, grp_tile,
            pl.BlockSpec((None, ROW_TILE, GROUP_WIDTH), lambda b, i: (b, i + q_off, _O_BLK)),
            grp_tile, grp_tile,
            pl.BlockSpec((None, ROW_TILE, GROUP_WIDTH), lambda b, i: (b, i + q_off, _Z_BLK)),
            pl.BlockSpec((1, GROUP_WIDTH), const),
            pl.BlockSpec((1, GROUP_WIDTH), const),
            _mod_spec(nt_ctx, q_off, d),
            pl.BlockSpec((1, d), const),
            pl.BlockSpec((d, d), const, pipeline_mode=_RESIDENT),
            pl.BlockSpec((d, D_FF), const, pipeline_mode=_RESIDENT),
            pl.BlockSpec((D_FF, d), const, pipeline_mode=_RESIDENT),
            pl.BlockSpec((1, d), const),
        ],
        out_specs=pl.BlockSpec((None, ROW_TILE, d), lambda b, i: (b, i, 0)),
        out_shape=jax.ShapeDtypeStruct((bsz, nt * ROW_TILE, d), F32),
        compiler_params=_params("parallel", "parallel"),
        name="out_ffn_final" if final else "out_ffn",
    )(x_ctx, x_lat, ya, yb, hf, hb, pf, yf, ybw, pf, gm, gs, modv, g2, wo, w1, w2, g_final)


def _gate_starts():
    cg, ddt = _OFF[10], _OFF[15]
    return [cg, cg + N_HEADS, cg + 2 * N_HEADS, cg + 3 * N_HEADS, ddt, ddt + N_HEADS]


def _gate_lanes(groups):
    pad = jnp.zeros((8 - N_HEADS,), F32)
    parts = []
    for g in groups:
        parts += [g.astype(F32), pad]
    parts.append(jnp.zeros((LANES - 8 * len(groups),), F32))
    return jnp.concatenate(parts).reshape(1, LANES)


def _proj_weight(w_in):
    (aq, ak, av, bq, bk, bv, cq, ck, cv, co, cg, dx, dz, db, dc, ddt) = _OFF[:-1]
    qscale = HEAD_DIM ** -0.5
    segs = []
    for base, scale in ((aq, qscale), (bq, 1.0)):
        segs += [(base + h * HEAD_DIM, HEAD_DIM, scale) for h in _Q_HEAD_ORDER]
        segs += [(base + GROUP_WIDTH, 2 * LANES, 1.0)]
    segs += [(cq, GROUP_WIDTH, qscale), (ck, 2 * GROUP_WIDTH, 1.0)]
    segs += [(dx, GROUP_WIDTH, 1.0), (db, 2 * GROUP_WIDTH, 1.0), (co, GROUP_WIDTH, 1.0), (dz, GROUP_WIDTH, 1.0)]
    parts = [w_in[:, s:s + n] if scale == 1.0 else w_in[:, s:s + n] * scale for s, n, scale in segs]
    zeros4 = jnp.zeros((w_in.shape[0], 8 - N_HEADS), w_in.dtype)
    for start in _gate_starts():
        parts += [w_in[:, start:start + N_HEADS], zeros4]
    parts.append(jnp.zeros((w_in.shape[0], LANES - 8 * len(_gate_starts())), w_in.dtype))
    w = jnp.concatenate(parts, axis=1)
    assert w.shape[1] == _N_PROJ
    return w.astype(MXU_DTYPE)


def _out_weight(w_out):
    parts = []
    for base in (0, GROUP_WIDTH):
        parts += [w_out[base + h * HEAD_DIM:base + (h + 1) * HEAD_DIM] for h in _Q_HEAD_ORDER]
    parts.append(w_out[2 * GROUP_WIDTH:])
    return jnp.concatenate(parts, axis=0).astype(MXU_DTYPE)


def _rope_tables(n_tok, n_ctx):
    t = jnp.arange(n_tok)
    row = (t // GRID_W).astype(F32)
    col = (t % GRID_W).astype(F32)
    half = HEAD_DIM // 2
    inv_freq = ROPE_BASE ** (-jnp.arange(0, half, 2, dtype=F32) / half)
    lane = np.arange(LANES)
    hd = lane % HEAD_DIM
    use_col = (hd // half) == 1
    pos = jnp.where(use_col[None, :], col[:, None], row[:, None])
    ang = pos * inv_freq[hd % (half // 2)][None, :]
    first = (hd % half) < (half // 2)
    cos = jnp.cos(ang)
    sin = jnp.where(first[None, :], -jnp.sin(ang), jnp.sin(ang))
    cos = jnp.concatenate([jnp.ones((n_ctx, LANES), F32), cos], axis=0)
    sin = jnp.concatenate([jnp.zeros((n_ctx, LANES), F32), sin], axis=0)
    return cos, sin


def kernel(x, c, ctx, c_ctx, w_ada, b_ada, g_norm1, g_norm2, w_in, sink_a, g_q_b, g_k_b, b_igate, b_fgate,
           g_mlstm, conv_w, conv_b, a_log, dt_bias, d_skip, g_ssm, w_out, w_ff1, w_ff2, g_final):
    bsz, n_tok, d = x.shape
    n_ctx = ctx.shape[1]
    depth = w_in.shape[0]
    assert d == D_MODEL and n_ctx % ROW_TILE == 0 and n_tok % ROW_TILE == 0

    cos, sin = _rope_tables(n_tok, n_ctx)
    n_rows = 16
    cvec = jnp.concatenate([c, c_ctx[None, :], jnp.zeros((n_rows - bsz - 1, d), F32)], axis=0)
    mod_all = _ada_mod(cvec, w_ada, b_ada)

    u = n_ctx + n_tok
    x_ctx, x_lat, lat_off = ctx, x, 0
    for layer in range(depth):
        need_ctx = layer < depth - 1
        mod = mod_all[layer].reshape(n_rows, 6, d)
        pad = jnp.zeros((bsz, 2, d), F32)
        mod_lat = jnp.concatenate([mod[:bsz], pad], axis=1)
        mod_ctx = jnp.broadcast_to(jnp.concatenate([mod[bsz], pad[0]], axis=0), (bsz, 8, d))
        modv = jnp.stack([mod_ctx, mod_lat], axis=1)

        gq = jnp.tile(g_q_b[layer] * (HEAD_DIM ** -0.5 * LOG2E), 2).reshape(1, LANES)
        gk = jnp.tile(g_k_b[layer], 2).reshape(1, LANES)
        pa, pb, pc, pf = _in_proj(x_ctx, x_lat, lat_off, u, modv, g_norm1[layer].reshape(1, d),
                                  _proj_weight(w_in[layer]), cos, sin, gq, gk, n_ctx)

        ya = _attn_a(pa, sink_a[layer], n_ctx, need_ctx)
        yb = _attn_b(pb, n_ctx, need_ctx)

        gbias = _gate_lanes([b_igate[layer, 0], b_fgate[layer, 0], b_igate[layer, 1], b_fgate[layer, 1],
                             dt_bias[layer, 0], dt_bias[layer, 1]])
        alog = jnp.broadcast_to(jnp.pad(a_log[layer], ((0, 0), (0, 8 - N_HEADS)))[:, :, None], (2, 8, LANES))
        dsk = jnp.repeat(d_skip[layer], HEAD_DIM).reshape(1, GROUP_WIDTH)
        hf, hb, yf, ybw = _scans(pc, pf, gbias, alog, conv_w[layer], conv_b[layer].reshape(1, -1), dsk, n_ctx)

        xc = _out_ffn(x_ctx, x_lat, lat_off, ya, yb, hf, hb, yf, ybw, pf, g_mlstm[layer].reshape(1, GROUP_WIDTH),
                      g_ssm[layer].reshape(1, GROUP_WIDTH), modv, g_norm2[layer].reshape(1, d),
                      _out_weight(w_out[layer]), w_ff1[layer].astype(MXU_DTYPE), w_ff2[layer].astype(MXU_DTYPE),
                      g_final.reshape(1, d), n_ctx, final=not need_ctx)
        x_ctx, x_lat, lat_off = xc, xc, n_ctx // ROW_TILE
    return xc
```

```python
import functools
import math

import numpy as np
import jax
import jax.numpy as jnp
from jax import lax
from jax.experimental import pallas as pl
from jax.experimental.pallas import tpu as pltpu

F32 = jnp.float32
MXU_DTYPE = jnp.bfloat16

D_MODEL = 1024
HEAD_DIM = 64
LANES = 128
GRID_W = 64
WINDOW = 128
ROPE_BASE = 10000.0
EPS = 1e-6
N_HEADS = 4
GROUP_WIDTH = N_HEADS * HEAD_DIM
D_STATE = 128
CHUNK = 128
ROW_TILE = 256
CHUNKS_PER_TILE = ROW_TILE // CHUNK
KEY_TILE = 1024
SOFTMAX_ROWS = 64
D_FF = 4 * D_MODEL
FF_CHUNK = 1024
HALO = 8
VMEM_LIMIT = 56 * 1024 * 1024
LOG2E = math.log2(math.e)

_SPLIT_SIZES = (256, 128, 128, 256, 128, 128, 256, 256, 256, 256, 16, 256, 256, 256, 256, 8)
_OFF = [int(o) for o in np.concatenate([[0], np.cumsum(_SPLIT_SIZES)])]
_Q_HEAD_ORDER = (0, 2, 1, 3)

_COL_A = 0
_COL_B = 512
_COL_C = 1024
_COL_F = 1792
_N_F = 768 + 256 + 256 + 128
_N_PROJ = _COL_F + _N_F
_GATE_BLK = (_N_F - LANES) // LANES
_O_BLK = 768 // GROUP_WIDTH
_Z_BLK = (768 + GROUP_WIDTH) // GROUP_WIDTH


def _dot(a, b):
    return jnp.dot(a, b, preferred_element_type=F32)


def _dot_nt(a, b):
    return lax.dot_general(a, b, (((1,), (1,)), ((), ())), preferred_element_type=F32)


def _dot_tn(a, b):
    return lax.dot_general(a, b, (((0,), (0,)), ((), ())), preferred_element_type=F32)


def _split3(x):
    hi = x.astype(MXU_DTYPE)
    r1 = x - hi.astype(F32)
    mid = r1.astype(MXU_DTYPE)
    lo = (r1 - mid.astype(F32)).astype(MXU_DTYPE)
    return [hi, mid, lo]


def _rows_cumsum(rows, mt3):
    return _dot(jnp.concatenate(_split3(rows), axis=1), mt3)


def _rows_to_columns(rows, sel3):
    return _dot_tn(jnp.concatenate(_split3(rows), axis=0), sel3)


def _cummax_lanes(x, direction):
    n = x.shape[-1]
    lane = lax.broadcasted_iota(jnp.int32, x.shape, 1)

    def shifted(v, shift):
        if direction == 0:
            return jnp.where(lane >= shift, pltpu.roll(v, shift, 1), -jnp.inf)
        return jnp.where(lane < n - shift, pltpu.roll(v, n - shift, 1), -jnp.inf)

    def tree_max(vals):
        while len(vals) > 1:
            vals = [jnp.maximum(a, b) for a, b in zip(vals[::2], vals[1::2])] + ([vals[-1]] if len(vals) % 2 else [])
        return vals[0]

    window = tree_max([x] + [shifted(x, s) for s in range(1, 8)])
    return tree_max([window] + [shifted(window, s) for s in range(8, n, 8)])


def _lane_is_low(shape):
    lane = lax.broadcasted_iota(jnp.int32, shape, len(shape) - 1)
    return (lane % LANES) < HEAD_DIM


def _head_mean_square(t, low):
    sq = t * t
    lo = jnp.sum(jnp.where(low, sq, 0.0), axis=-1, keepdims=True)
    hi = jnp.sum(jnp.where(low, 0.0, sq), axis=-1, keepdims=True)
    return jnp.where(low, lo, hi) * (1.0 / HEAD_DIM)


def _params(*sem):
    return pltpu.CompilerParams(dimension_semantics=sem, vmem_limit_bytes=VMEM_LIMIT)


_RESIDENT = pl.Buffered(1)


def _ada_kernel(c_ref, w_ref, b_ref, out_ref):
    cv = c_ref[...]
    cv = cv * jax.nn.sigmoid(cv)
    out_ref[...] = _dot(cv.astype(MXU_DTYPE), w_ref[...].astype(MXU_DTYPE)) + b_ref[...]


def _ada_mod(cvec, w_ada, b_ada):
    depth, d, n = w_ada.shape
    rows = cvec.shape[0]
    tn = 1536
    return pl.pallas_call(
        _ada_kernel,
        grid=(depth, n // tn),
        in_specs=[
            pl.BlockSpec((rows, d), lambda l, j: (0, 0)),
            pl.BlockSpec((None, d, tn), lambda l, j: (l, 0, j)),
            pl.BlockSpec((None, 1, tn), lambda l, j: (l, 0, j)),
        ],
        out_specs=pl.BlockSpec((None, rows, tn), lambda l, j: (l, 0, j)),
        out_shape=jax.ShapeDtypeStruct((depth, rows, n), F32),
        compiler_params=_params("parallel", "parallel"),
        name="ada_mod",
    )(cvec, w_ada, b_ada.reshape(depth, 1, n))


def _rmsnorm_mod(x, g, shift, scale):
    ms = jnp.mean(x * x, axis=-1, keepdims=True)
    return (x * lax.rsqrt(ms + EPS) * g) * (1.0 + scale) + shift


def _in_proj_kernel(xctx_ref, xlat_ref, mod_ref, g1_ref, w_ref, cos_ref, sin_ref, gq_ref, gk_ref,
                    a_ref, b_ref, c_ref, f_ref, *, nt_ctx):
    x = jnp.where(pl.program_id(1) < nt_ctx, xctx_ref[...], xlat_ref[...])
    hn = _rmsnorm_mod(x, g1_ref[...], mod_ref[0:1, :], mod_ref[1:2, :])
    hb = hn.astype(MXU_DTYPE)
    cos = cos_ref[...]
    sin = sin_ref[...]
    lane = lax.broadcasted_iota(jnp.int32, (1, LANES), 1)
    first = (lane % 32) < 16
    low = _lane_is_low((1, LANES))

    def rope(t):
        partner = jnp.where(first, pltpu.roll(t, LANES - 16, 1), pltpu.roll(t, 16, 1))
        return t * cos + partner * sin

    pa = _dot(hb, w_ref[:, _COL_A:_COL_A + 512])
    for j in range(3):
        a_ref[:, j * LANES:(j + 1) * LANES] = rope(pa[:, j * LANES:(j + 1) * LANES]).astype(a_ref.dtype)
    a_ref[:, 3 * LANES:] = pa[:, 3 * LANES:].astype(a_ref.dtype)

    pb = _dot(hb, w_ref[:, _COL_B:_COL_B + 512])
    for j in range(3):
        t = pb[:, j * LANES:(j + 1) * LANES]
        g = gq_ref[...] if j < 2 else gk_ref[...]
        t = t * lax.rsqrt(_head_mean_square(t, low) + EPS) * g
        b_ref[:, j * LANES:(j + 1) * LANES] = rope(t).astype(b_ref.dtype)
    b_ref[:, 3 * LANES:] = pb[:, 3 * LANES:].astype(b_ref.dtype)

    c_ref[...] = _dot(hb, w_ref[:, _COL_C:_COL_F]).astype(c_ref.dtype)
    f_ref[...] = _dot(hb, w_ref[:, _COL_F:])


def _residual_specs(nt_ctx, lat_off, q_off, d):
    ctx_spec = pl.BlockSpec((None, ROW_TILE, d), lambda b, i: (b, jnp.minimum(i + q_off, nt_ctx - 1), 0))
    lat_spec = pl.BlockSpec((None, ROW_TILE, d),
                            lambda b, i: (b, jnp.maximum(i + q_off - nt_ctx, 0) + lat_off, 0))
    return [ctx_spec, lat_spec]


def _mod_spec(nt_ctx, q_off, d):
    return pl.BlockSpec((None, None, 8, d), lambda b, i: (b, jnp.where(i + q_off < nt_ctx, 0, 1), 0, 0))


def _in_proj(x_ctx, x_lat, lat_off, u, modv, g1, w, cos, sin, gq, gk, n_ctx):
    bsz, _, d = x_lat.shape
    nt = u // ROW_TILE
    nt_ctx = n_ctx // ROW_TILE
    row = lambda b, i: (b, i, 0)
    const = lambda b, i: (0, 0)
    return pl.pallas_call(
        functools.partial(_in_proj_kernel, nt_ctx=nt_ctx),
        grid=(bsz, nt),
        in_specs=_residual_specs(nt_ctx, lat_off, 0, d) + [
            _mod_spec(nt_ctx, 0, d),
            pl.BlockSpec((1, d), const),
            pl.BlockSpec((d, _N_PROJ), const, pipeline_mode=_RESIDENT),
            pl.BlockSpec((ROW_TILE, LANES), lambda b, i: (i, 0)),
            pl.BlockSpec((ROW_TILE, LANES), lambda b, i: (i, 0)),
            pl.BlockSpec((1, LANES), const),
            pl.BlockSpec((1, LANES), const),
        ],
        out_specs=[
            pl.BlockSpec((None, ROW_TILE, 512), row),
            pl.BlockSpec((None, ROW_TILE, 512), row),
            pl.BlockSpec((None, ROW_TILE, 768), row),
            pl.BlockSpec((None, ROW_TILE, _N_F), row),
        ],
        out_shape=[
            jax.ShapeDtypeStruct((bsz, u, 512), MXU_DTYPE),
            jax.ShapeDtypeStruct((bsz, u, 512), MXU_DTYPE),
            jax.ShapeDtypeStruct((bsz, u, 768), MXU_DTYPE),
            jax.ShapeDtypeStruct((bsz, u, _N_F), F32),
        ],
        compiler_params=_params("parallel", "parallel"),
        name="in_proj",
    )(x_ctx, x_lat, modv, g1, w, cos, sin, gq, gk)


def _stack_heads(q_ref, qs_ref):
    low = _lane_is_low((1, LANES))
    for jb in range(2):
        q = q_ref[:, jb * LANES:(jb + 1) * LANES]
        for half in range(2):
            idx = 2 * jb + half
            keep = low if half == 0 else jnp.logical_not(low)
            qs_ref[idx * ROW_TILE:(idx + 1) * ROW_TILE, :] = jnp.where(keep, q, jnp.zeros_like(q))


def _unstack_heads(o, out_ref):
    low = _lane_is_low((1, LANES))
    for jb in range(2):
        lo = o[(2 * jb) * ROW_TILE:(2 * jb + 1) * ROW_TILE, :]
        hi = o[(2 * jb + 1) * ROW_TILE:(2 * jb + 2) * ROW_TILE, :]
        out_ref[:, jb * LANES:(jb + 1) * LANES] = jnp.where(low, lo, hi).astype(out_ref.dtype)


def _attn_a_kernel(sink_ref, q_ref, kp_ref, ko_ref, kn_ref, kc_ref, vp_ref, vo_ref, vn_ref, vc_ref,
                   out_ref, qs_ref, kbuf, vbuf, s_ref, o_ref, *, n_tok, n_ctx, q_off):
    i = pl.program_id(1) + q_off
    nb = 2 * WINDOW + ROW_TILE
    nk = nb + n_ctx
    kbuf[0:WINDOW] = kp_ref[...]
    kbuf[WINDOW:WINDOW + ROW_TILE] = ko_ref[...]
    kbuf[WINDOW + ROW_TILE:nb] = kn_ref[...]
    kbuf[nb:nk] = kc_ref[...]
    vbuf[0:WINDOW] = vp_ref[...]
    vbuf[WINDOW:WINDOW + ROW_TILE] = vo_ref[...]
    vbuf[WINDOW + ROW_TILE:nb] = vn_ref[...]
    vbuf[nb:nk] = vc_ref[...]
    _stack_heads(q_ref, qs_ref)

    r = lax.broadcasted_iota(jnp.int32, (ROW_TILE, nk), 0)
    c = lax.broadcasted_iota(jnp.int32, (ROW_TILE, nk), 1)
    kpos = (i - 1) * ROW_TILE - WINDOW + c
    dist = c - r
    band = jnp.where(dist >= 0, jnp.where(dist <= 2 * WINDOW, 1, 0), 0)
    band = jnp.where(kpos >= 0, jnp.where(kpos < n_tok, band, 0), 0)
    band = jnp.where(i >= 1, band, 0)
    bias = jnp.where(c >= nb, 0.0, jnp.where(band > 0, 0.0, -jnp.inf)).astype(F32)

    for idx in range(N_HEADS):
        rows = slice(idx * ROW_TILE, (idx + 1) * ROW_TILE)
        s_ref[rows, :] = _dot_nt(qs_ref[rows, :], kbuf[...]) + bias
    for idx in range(N_HEADS):
        rows = slice(idx * ROW_TILE, (idx + 1) * ROW_TILE)
        sink = sink_ref[_Q_HEAD_ORDER[idx]]
        m = jnp.maximum(jnp.max(s_ref[rows, :], axis=-1, keepdims=True), sink)
        p = jnp.exp(s_ref[rows, :] - m)
        l = jnp.sum(p, axis=-1, keepdims=True) + jnp.exp(sink - m)
        o_ref[rows, :] = _dot(p.astype(MXU_DTYPE), vbuf[...]) / l
    _unstack_heads(o_ref[...], out_ref)


def _attn_a(pa, sink, n_ctx, need_ctx):
    bsz, u, _ = pa.shape
    n_tok = u - n_ctx
    q_off = 0 if need_ctx else n_ctx // ROW_TILE
    nt = u // ROW_TILE - q_off
    last_blk = u // WINDOW - 1
    rpw = ROW_TILE // WINDOW
    nk = 2 * WINDOW + ROW_TILE + n_ctx

    def own(col):
        return lambda b, i: (b, i + q_off, col)

    def prev(col):
        return lambda b, i: (b, jnp.maximum((i + q_off) * rpw - 1, 0), col)

    def nxt(col):
        return lambda b, i: (b, jnp.minimum((i + q_off + 1) * rpw, last_blk), col)

    def ctx(col):
        return lambda b, i: (b, 0, col)

    kern = functools.partial(_attn_a_kernel, n_tok=n_tok, n_ctx=n_ctx, q_off=q_off)
    return pl.pallas_call(
        kern,
        grid=(bsz, nt),
        in_specs=[
            pl.BlockSpec(memory_space=pltpu.SMEM),
            pl.BlockSpec((None, ROW_TILE, 2 * LANES), own(0)),
            pl.BlockSpec((None, WINDOW, LANES), prev(2)),
            pl.BlockSpec((None, ROW_TILE, LANES), own(2)),
            pl.BlockSpec((None, WINDOW, LANES), nxt(2)),
            pl.BlockSpec((None, n_ctx, LANES), ctx(2)),
            pl.BlockSpec((None, WINDOW, LANES), prev(3)),
            pl.BlockSpec((None, ROW_TILE, LANES), own(3)),
            pl.BlockSpec((None, WINDOW, LANES), nxt(3)),
            pl.BlockSpec((None, n_ctx, LANES), ctx(3)),
        ],
        out_specs=pl.BlockSpec((None, ROW_TILE, 2 * LANES), lambda b, i: (b, i, 0)),
        out_shape=jax.ShapeDtypeStruct((bsz, nt * ROW_TILE, 2 * LANES), MXU_DTYPE),
        scratch_shapes=[pltpu.VMEM((N_HEADS * ROW_TILE, LANES), MXU_DTYPE),
                        pltpu.VMEM((nk, LANES), MXU_DTYPE), pltpu.VMEM((nk, LANES), MXU_DTYPE),
                        pltpu.VMEM((N_HEADS * ROW_TILE, nk), F32),
                        pltpu.VMEM((N_HEADS * ROW_TILE, LANES), F32)],
        compiler_params=_params("parallel", "parallel"),
        name="attn_window",
    )(sink, pa, pa, pa, pa, pa, pa, pa, pa, pa)


def _attn_b_kernel(q_ref, k_ref, v_ref, out_ref, qs_ref, vext_ref, s_ref, p_ref, m_ref, alpha_ref, acc_ref,
                   *, n_ctx, n_tok, q_off, tk):
    qi = pl.program_id(1) + q_off
    n_rows = N_HEADS * ROW_TILE

    @pl.when(pl.program_id(1) == 0)
    def _():
        vext_ref[:, 0:LANES] = v_ref[...]
        vext_ref[:, LANES:] = jnp.ones((vext_ref.shape[0], LANES), vext_ref.dtype)

    _stack_heads(q_ref, qs_ref)

    def scores(slot, start, size):
        s_ref[slot, :, 0:size] = _dot_nt(qs_ref[...], k_ref[start:start + size, :])

    def softmax(slot, size, first):
        m_new = jnp.max(s_ref[slot, :, 0:size], axis=-1, keepdims=True)
        if not first:
            m_old = m_ref[...]
            m_new = jnp.maximum(m_old, m_new)
            alpha_ref[slot] = jnp.exp2(m_old - m_new)
        m_ref[...] = m_new
        p_ref[slot, :, 0:size] = jnp.exp2(s_ref[slot, :, 0:size] - m_new).astype(p_ref.dtype)

    def accumulate(slot, start, size, first):
        pv = _dot(p_ref[slot, :, 0:size], vext_ref[start:start + size, :])
        if first:
            acc_ref[...] = pv
        else:
            acc_ref[...] = alpha_ref[slot] * acc_ref[...] + pv

    def attend(tiles):
        scores(0, *tiles[0])
        for t, (start, size) in enumerate(tiles):
            if t + 1 < len(tiles):
                scores((t + 1) % 2, *tiles[t + 1])
            softmax(t % 2, size, t == 0)
            accumulate(t % 2, start, size, t == 0)
        _unstack_heads(acc_ref[:, 0:LANES] / acc_ref[:, LANES:], out_ref)

    ctx_tiles = [(0, n_ctx)]
    all_tiles = ctx_tiles + [(n_ctx + t * tk, tk) for t in range(n_tok // tk)]
    if q_off == 0:
        pl.when(qi == 0)(functools.partial(attend, ctx_tiles))
        pl.when(qi > 0)(functools.partial(attend, all_tiles))
    else:
        attend(all_tiles)


def _attn_b(pb, n_ctx, need_ctx):
    bsz, u, _ = pb.shape
    n_tok = u - n_ctx
    q_off = 0 if need_ctx else n_ctx // ROW_TILE
    nt = u // ROW_TILE - q_off
    tk = min(KEY_TILE, n_tok)
    assert n_tok % tk == 0
    kern = functools.partial(_attn_b_kernel, n_ctx=n_ctx, n_tok=n_tok, q_off=q_off, tk=tk)
    return pl.pallas_call(
        kern,
        grid=(bsz, nt),
        in_specs=[
            pl.BlockSpec((None, ROW_TILE, 2 * LANES), lambda b, i: (b, i + q_off, 0)),
            pl.BlockSpec((None, u, LANES), lambda b, i: (b, 0, 2)),
            pl.BlockSpec((None, u, LANES), lambda b, i: (b, 0, 3)),
        ],
        out_specs=pl.BlockSpec((None, ROW_TILE, 2 * LANES), lambda b, i: (b, i, 0)),
        out_shape=jax.ShapeDtypeStruct((bsz, nt * ROW_TILE, 2 * LANES), MXU_DTYPE),
        scratch_shapes=[
            pltpu.VMEM((N_HEADS * ROW_TILE, LANES), MXU_DTYPE),
            pltpu.VMEM((u, 2 * LANES), MXU_DTYPE),
            pltpu.VMEM((2, N_HEADS * ROW_TILE, tk), F32),
            pltpu.VMEM((2, N_HEADS * ROW_TILE, tk), MXU_DTYPE),
            pltpu.VMEM((N_HEADS * ROW_TILE, 1), F32),
            pltpu.VMEM((2, N_HEADS * ROW_TILE, 1), F32),
            pltpu.VMEM((N_HEADS * ROW_TILE, 2 * LANES), F32),
        ],
        compiler_params=_params("parallel", "arbitrary"),
        name="attn_dense",
    )(pb, pb, pb)


def _scan_tile(j, direction, nt_ctx, nt):
    if direction == 0:
        return j
    return jnp.where(j < nt_ctx, nt_ctx - 1 - j, nt + nt_ctx - 1 - j)


def _causal_mask(direction):
    t = lax.broadcasted_iota(jnp.int32, (CHUNK, CHUNK), 0)
    s = lax.broadcasted_iota(jnp.int32, (CHUNK, CHUNK), 1)
    return (s <= t) if direction == 0 else (s >= t)


def _chunk_order(direction):
    order = range(CHUNKS_PER_TILE)
    return order if direction == 0 else reversed(order)


def _log_sigmoid(x):
    return jnp.minimum(x, 0.0) - jnp.log1p(jnp.exp(-jnp.abs(x)))


def _softplus(x):
    return jnp.maximum(x, 0.0) + jnp.log1p(jnp.exp(-jnp.abs(x)))


def _mlstm_gates(direction, g_rows, m_prev, mt3):
    end = CHUNK - 1 if direction == 0 else 0
    i_r = jnp.concatenate([g[16 * direction:16 * direction + 8] for g in g_rows], axis=0)
    f_r = jnp.concatenate([g[16 * direction + 8:16 * direction + 16] for g in g_rows], axis=0)
    b_r = _rows_cumsum(_log_sigmoid(f_r), mt3)
    r_r = i_r - b_r
    cmax = _cummax_lanes(r_r, direction)
    tot = jnp.broadcast_to(b_r[:, end:end + 1], b_r.shape)
    g2 = tot - b_r + i_r
    g2max = jnp.broadcast_to(jnp.max(g2, axis=-1, keepdims=True), g2.shape)
    per_chunk = []
    for idx in range(len(g_rows)):
        sl = slice(8 * idx, 8 * idx + 8)
        c_r = jnp.maximum(m_prev, cmax[sl])
        dp_r = jnp.exp(m_prev - c_r)
        em_r = jnp.exp(-(b_r[sl] + c_r))
        m_new = jnp.maximum(tot[sl] + m_prev, g2max[sl])
        wk_r = jnp.exp(g2[sl] - m_new)
        cd_r = jnp.exp(tot[sl] + m_prev - m_new)
        per_chunk.append((jnp.concatenate([c_r, dp_r, em_r, wk_r], axis=0), r_r[sl], cd_r))
        m_prev = m_new
    return per_chunk, m_prev


def _mlstm_scores(qkvs):
    low1 = _lane_is_low((1, LANES))
    ones = jnp.ones((CHUNK, LANES), MXU_DTYPE)
    pre = []
    for qkv in qkvs:
        per_pair = []
        for pair in range(2):
            q = qkv[:, pair * LANES:(pair + 1) * LANES]
            k = qkv[:, GROUP_WIDTH + pair * LANES:GROUP_WIDTH + (pair + 1) * LANES]
            v = qkv[:, 2 * GROUP_WIDTH + pair * LANES:2 * GROUP_WIDTH + (pair + 1) * LANES]
            qk = [_dot_nt(jnp.where(low1 if half == 0 else jnp.logical_not(low1), q, jnp.zeros_like(q)), k)
                  for half in range(2)]
            per_pair.append((q, k, jnp.concatenate([v, ones], axis=1), qk))
        pre.append(per_pair)
    return pre


def _mlstm_intra(items, pre):
    low1 = _lane_is_low((1, LANES))
    ones = jnp.ones((CHUNK, LANES), MXU_DTYPE)
    r2 = lax.broadcasted_iota(jnp.int32, (LANES, 2 * LANES), 0)
    c2 = lax.broadcasted_iota(jnp.int32, (LANES, 2 * LANES), 1)
    state_mask = (r2 < HEAD_DIM) == ((c2 % LANES) < HEAD_DIM)
    mid = []
    for (direction, qkv, r_r, cd_r, cols), per_pair in zip(items, pre):
        mask = _causal_mask(direction)
        res = []
        for pair, (q, k, vext, qk) in enumerate(per_pair):
            svs, dens = [], []
            for half in range(2):
                h = 2 * pair + half
                w = jnp.exp(jnp.where(mask, r_r[h:h + 1, :] - cols[:, h * LANES:(h + 1) * LANES], -jnp.inf))
                s = qk[half] * w
                s_hi = s.astype(MXU_DTYPE)
                s_lo = (s - s_hi.astype(F32)).astype(MXU_DTYPE)
                sv = _dot(s_hi, vext)
                svs.append(sv[:, :LANES])
                dens.append(sv[:, LANES:] + _dot(s_lo, ones))
            wk = cols[:, (8 + pair) * LANES:(9 + pair) * LANES]
            upd = _dot_tn((k.astype(F32) * wk).astype(MXU_DTYPE), vext)
            res.append((jnp.where(low1, svs[0], svs[1]), jnp.where(low1, dens[0], dens[1]),
                        jnp.where(state_mask, upd, 0.0)))
        mid.append(res)
    return mid


def _mlstm_recurrence(items, pre, mid, state):
    low2 = _lane_is_low((1, 2 * LANES))
    outs = []
    for (direction, qkv, r_r, cd_r, cols), per_pair, res in zip(items, pre, mid):
        out = []
        for pair in range(2):
            q = per_pair[pair][0]
            sv, den_intra, upd = res[pair]
            prior = _dot(q, state[direction][pair].astype(MXU_DTYPE))
            dp = cols[:, (4 + pair) * LANES:(5 + pair) * LANES]
            em = cols[:, (6 + pair) * LANES:(7 + pair) * LANES]
            num = sv + dp * prior[:, :LANES]
            den = den_intra + dp * prior[:, LANES:]
            out.append(num / jnp.maximum(jnp.abs(den), em))
            cd0 = jnp.concatenate([cd_r[2 * pair:2 * pair + 1]] * 2, axis=1)
            cd1 = jnp.concatenate([cd_r[2 * pair + 1:2 * pair + 2]] * 2, axis=1)
            state[direction][pair] = state[direction][pair] * jnp.where(low2, cd0, cd1) + upd
        outs.append(out)
    return outs


def _gate_rows(g_refs, gbias_ref):
    gbias = gbias_ref[...]
    return [[(g_ref[ci * CHUNK:(ci + 1) * CHUNK, :] + gbias).T for ci in _chunk_order(direction)]
            for direction, g_ref in enumerate(g_refs)]


def _mlstm_body(g_rows, qkvf_ref, qkvb_ref, mt3_ref, sel3_ref, hf_ref, hb_ref, st_ref, m_ref):
    dirs = ((qkvf_ref, None, hf_ref), (qkvb_ref, None, hb_ref))
    work = [(direction, idx) for idx in range(CHUNKS_PER_TILE) for direction in range(2)]
    row_slices = []
    for direction, idx in work:
        ci = list(_chunk_order(direction))[idx]
        row_slices.append(slice(ci * CHUNK, (ci + 1) * CHUNK))
    gates = []
    for direction in range(2):
        per_chunk, m_new = _mlstm_gates(direction, g_rows[direction], m_ref[direction], mt3_ref[direction])
        m_ref[direction] = m_new
        gates.append(per_chunk)
    yield
    rows_all = jnp.concatenate([gates[direction][idx][0] for direction, idx in work], axis=1)
    cols_all = _rows_to_columns(rows_all, sel3_ref[...])
    qkvs = [dirs[direction][0][rows, :] for (direction, _), rows in zip(work, row_slices)]
    pre = _mlstm_scores(qkvs)
    yield
    state = [[st_ref[direction, pair] for pair in range(2)] for direction in range(2)]
    items = []
    for n, (direction, idx) in enumerate(work):
        _, r_r, cd_r = gates[direction][idx]
        items.append((direction, qkvs[n], r_r, cd_r, cols_all[n * CHUNK:(n + 1) * CHUNK, :]))
    mid = _mlstm_intra(items, pre)
    yield
    outs = _mlstm_recurrence(items, pre, mid, state)
    for (direction, *_), rows, out in zip(items, row_slices, outs):
        for pair in range(2):
            dirs[direction][2][rows, pair * LANES:(pair + 1) * LANES] = out[pair]
    for direction in range(2):
        for pair in range(2):
            st_ref[direction, pair] = state[direction][pair]


def _scan_masks():
    s = np.arange(CHUNK)[:, None]
    t = np.arange(CHUNK)[None, :]
    mats = [np.tile((s <= t).astype(np.float32), (3, 1)), np.tile((s >= t).astype(np.float32), (3, 1))]
    return jnp.asarray(np.stack(mats), MXU_DTYPE)


def _column_selector(n_head_groups, n_pair_groups):
    n_groups = n_head_groups + n_pair_groups
    n_rows = 8 * n_groups + (-8 * n_groups) % 16
    n_cols = (N_HEADS * n_head_groups + 2 * n_pair_groups) * LANES
    sel = np.zeros((n_rows, n_cols), np.float32)
    col = 0
    for g in range(n_head_groups):
        for h in range(N_HEADS):
            sel[8 * g + h, col:col + LANES] = 1.0
            col += LANES
    for g in range(n_head_groups, n_groups):
        for pair in range(2):
            sel[8 * g + 2 * pair, col:col + HEAD_DIM] = 1.0
            sel[8 * g + 2 * pair + 1, col + HEAD_DIM:col + LANES] = 1.0
            col += LANES
    return jnp.asarray(np.tile(sel, (3, 1)), MXU_DTYPE)


def _ssd_conv(x_ref, xp_ref, xn_ref, cw_ref, cb_ref, tile, nt_ctx, nt):
    xin = x_ref[...]
    has_prev = (tile != 0) & (tile != nt_ctx)
    has_next = (tile != nt_ctx - 1) & (tile != nt - 1)
    prow = jnp.where(has_prev, xp_ref[HALO - 1:HALO, :], 0.0)
    nrow = jnp.where(has_next, xn_ref[0:1, :], 0.0)
    ridx = lax.broadcasted_iota(jnp.int32, (ROW_TILE, 1), 0)
    up = jnp.where(ridx == 0, prow, pltpu.roll(xin, 1, 0))
    dn = jnp.where(ridx == ROW_TILE - 1, nrow, pltpu.roll(xin, ROW_TILE - 1, 0))
    u = cw_ref[0:1, :] * up + cw_ref[1:2, :] * xin + cw_ref[2:3, :] * dn + cb_ref[...]
    return u * jax.nn.sigmoid(u)


def _ssd_gates(direction, g_rows, neg_a, mt3):
    end = CHUNK - 1 if direction == 0 else 0
    n = len(g_rows)
    dt_r = _softplus(jnp.concatenate([g[32 + 8 * direction:40 + 8 * direction] for g in g_rows], axis=0))
    cum_r = _rows_cumsum(dt_r * jnp.concatenate([neg_a] * n, axis=0), mt3)
    tot = jnp.broadcast_to(cum_r[:, end:end + 1], cum_r.shape)
    e_r = jnp.exp(cum_r)
    wend_r = jnp.exp(tot - cum_r) * dt_r
    d_r = jnp.exp(tot)
    zero = jnp.zeros((8, CHUNK), F32)
    per_chunk = []
    for idx in range(n):
        sl = slice(8 * idx, 8 * idx + 8)
        per_chunk.append((jnp.concatenate([cum_r[sl], e_r[sl], wend_r[sl], zero], axis=0),
                          cum_r[sl], dt_r[sl], d_r[sl]))
    return per_chunk


def _ssd_scores(us):
    pre = []
    for u in us:
        per_grp = []
        for grp in range(2):
            bm = u[:, GROUP_WIDTH + grp * D_STATE:GROUP_WIDTH + (grp + 1) * D_STATE].astype(MXU_DTYPE)
            cm = u[:, 2 * GROUP_WIDTH + grp * D_STATE:2 * GROUP_WIDTH + (grp + 1) * D_STATE].astype(MXU_DTYPE)
            x_pair = u[:, grp * LANES:(grp + 1) * LANES]
            per_grp.append((bm, cm, x_pair, _dot_nt(cm, bm)))
        pre.append(per_grp)
    return pre


def _ssd_intra(items, pre):
    low1 = _lane_is_low((1, LANES))
    mid = []
    for (direction, u, cum_r, dt_r, d_r, cols), per_grp in zip(items, pre):
        mask = _causal_mask(direction)
        res = []
        for grp, (bm, cm, x_pair, gmat) in enumerate(per_grp):
            xb = x_pair.astype(MXU_DTYPE)
            ys = []
            for half in range(2):
                h = 2 * grp + half
                decay = jnp.exp(jnp.where(mask, cols[:, h * LANES:(h + 1) * LANES] - cum_r[h:h + 1, :], -jnp.inf))
                s = gmat * decay * dt_r[h:h + 1, :]
                ys.append(_dot(s.astype(MXU_DTYPE), xb))
            w_pair = cols[:, (6 + grp) * LANES:(7 + grp) * LANES]
            upd = _dot_tn(bm, (x_pair * w_pair).astype(MXU_DTYPE))
            res.append((jnp.where(low1, ys[0], ys[1]), upd))
        mid.append(res)
    return mid


def _ssd_recurrence(items, pre, mid, state):
    low1 = _lane_is_low((1, LANES))
    outs = []
    for (direction, u, cum_r, dt_r, d_r, cols), per_grp, res in zip(items, pre, mid):
        out = []
        for grp in range(2):
            cm = per_grp[grp][1]
            y_intra, upd = res[grp]
            ch = _dot(cm, state[direction][grp].astype(MXU_DTYPE))
            e_pair = cols[:, (4 + grp) * LANES:(5 + grp) * LANES]
            out.append(y_intra + e_pair * ch)
            d_pair = jnp.where(low1, d_r[2 * grp:2 * grp + 1], d_r[2 * grp + 1:2 * grp + 2])
            state[direction][grp] = state[direction][grp] * d_pair + upd
        outs.append(out)
    return outs


def _ssd_convs(xf_ref, xfp_ref, xfn_ref, xb_ref, xbp_ref, xbn_ref, cw_ref, cb_ref, *, nt_ctx, nt):
    j = pl.program_id(1)
    convs = []
    for direction, (x_ref, xp_ref, xn_ref) in enumerate(((xf_ref, xfp_ref, xfn_ref), (xb_ref, xbp_ref, xbn_ref))):
        tile = _scan_tile(j, direction, nt_ctx, nt)
        convs.append(_ssd_conv(x_ref, xp_ref, xn_ref, cw_ref, cb_ref, tile, nt_ctx, nt))
    return convs


def _ssd_body(g_rows, convs, alog_ref, dskip_ref, mt3_ref, sel3_ref, yf_ref, yb_ref, st_ref):
    dirs = ((None, None, None, None, yf_ref), (None, None, None, None, yb_ref))
    gates = []
    for direction in range(2):
        gates.append(_ssd_gates(direction, g_rows[direction], -jnp.exp(alog_ref[direction]), mt3_ref[direction]))
    work = [(direction, idx) for idx in range(CHUNKS_PER_TILE) for direction in range(2)]
    yield
    rows_all = jnp.concatenate([gates[direction][idx][0] for direction, idx in work], axis=1)
    cols_all = _rows_to_columns(rows_all, sel3_ref[...])
    items, row_slices = [], []
    for n, (direction, idx) in enumerate(work):
        ci = list(_chunk_order(direction))[idx]
        rows = slice(ci * CHUNK, (ci + 1) * CHUNK)
        _, cum_r, dt_r, d_r = gates[direction][idx]
        items.append((direction, convs[direction][rows, :], cum_r, dt_r, d_r,
                      cols_all[n * CHUNK:(n + 1) * CHUNK, :]))
        row_slices.append(rows)
    pre = _ssd_scores([item[1] for item in items])
    yield
    mid = _ssd_intra(items, pre)
    yield
    state = [[st_ref[direction, grp] for grp in range(2)] for direction in range(2)]
    outs = _ssd_recurrence(items, pre, mid, state)
    for (direction, u, *_), rows, out in zip(items, row_slices, outs):
        for grp in range(2):
            sl = slice(grp * LANES, (grp + 1) * LANES)
            y = out[grp]
            if direction == 0:
                y = y + dskip_ref[:, sl] * u[:, sl]
            dirs[direction][4][rows, sl] = y
    for direction in range(2):
        for grp in range(2):
            st_ref[direction, grp] = state[direction][grp]


def _scans_kernel(qkvf_ref, qkvb_ref, gf_ref, gb_ref, xf_ref, xfp_ref, xfn_ref, xb_ref, xbp_ref, xbn_ref,
                  gbias_ref, alog_ref, cw_ref, cb_ref, dskip_ref, mt3_ref, selm_ref, sels_ref,
                  hf_ref, hb_ref, yf_ref, yb_ref, stm_ref, m_ref, sts_ref, *, nt_ctx, nt):
    @pl.when(pl.program_id(1) == 0)
    def _():
        stm_ref[...] = jnp.zeros(stm_ref.shape, F32)
        m_ref[...] = jnp.zeros(m_ref.shape, F32)
        sts_ref[...] = jnp.zeros(sts_ref.shape, F32)

    convs = _ssd_convs(xf_ref, xfp_ref, xfn_ref, xb_ref, xbp_ref, xbn_ref, cw_ref, cb_ref, nt_ctx=nt_ctx, nt=nt)
    g_rows = _gate_rows((gf_ref, gb_ref), gbias_ref)
    bodies = [
        _mlstm_body(g_rows, qkvf_ref, qkvb_ref, mt3_ref, selm_ref, hf_ref, hb_ref, stm_ref, m_ref),
        _ssd_body(g_rows, convs, alog_ref, dskip_ref, mt3_ref, sels_ref, yf_ref, yb_ref, sts_ref),
    ]
    while bodies:
        bodies = [body for body in bodies if next(body, "done") != "done"]


def _scans(pc, pf, gbias, alog, conv_w, conv_b, d_skip, n_ctx):
    bsz, u, _ = pf.shape
    nt = u // ROW_TILE
    nt_ctx = n_ctx // ROW_TILE
    hpt = ROW_TILE // HALO
    last_halo = u // HALO - 1
    const = lambda b, j: (0, 0)
    const3 = lambda b, j: (0, 0, 0)
    tile = [lambda b, j, d=d: _scan_tile(j, d, nt_ctx, nt) for d in range(2)]

    def tile_spec(direction, width, col):
        return pl.BlockSpec((None, ROW_TILE, width), lambda b, j: (b, tile[direction](b, j), col))

    def conv_specs(direction):
        t = tile[direction]
        return [
            tile_spec(direction, 768, 0),
            pl.BlockSpec((None, HALO, 768), lambda b, j: (b, jnp.maximum(t(b, j) * hpt - 1, 0), 0)),
            pl.BlockSpec((None, HALO, 768), lambda b, j: (b, jnp.minimum((t(b, j) + 1) * hpt, last_halo), 0)),
        ]

    out = jax.ShapeDtypeStruct((bsz, u, GROUP_WIDTH), F32)
    mt3 = _scan_masks()
    sel_m = _column_selector(1, 3)
    sel_s = _column_selector(1, 2)
    return pl.pallas_call(
        functools.partial(_scans_kernel, nt_ctx=nt_ctx, nt=nt),
        grid=(bsz, nt),
        in_specs=[tile_spec(0, 768, 0), tile_spec(1, 768, 0),
                  tile_spec(0, LANES, _GATE_BLK), tile_spec(1, LANES, _GATE_BLK)]
        + conv_specs(0) + conv_specs(1) + [
            pl.BlockSpec((1, LANES), const),
            pl.BlockSpec((2, 8, LANES), const3),
            pl.BlockSpec((3, 768), const),
            pl.BlockSpec((1, 768), const),
            pl.BlockSpec((1, GROUP_WIDTH), const),
            pl.BlockSpec(mt3.shape, const3),
            pl.BlockSpec(sel_m.shape, const),
            pl.BlockSpec(sel_s.shape, const),
        ],
        out_specs=[tile_spec(0, GROUP_WIDTH, 0), tile_spec(1, GROUP_WIDTH, 0),
                   tile_spec(0, GROUP_WIDTH, 0), tile_spec(1, GROUP_WIDTH, 0)],
        out_shape=[out, out, out, out],
        scratch_shapes=[pltpu.VMEM((2, 2, LANES, 2 * LANES), F32), pltpu.VMEM((2, 8, LANES), F32),
                        pltpu.VMEM((2, 2, D_STATE, LANES), F32)],
        compiler_params=_params("parallel", "arbitrary"),
        name="scans",
    )(pc, pc, pf, pf, pf, pf, pf, pf, pf, pf, gbias, alog, conv_w, conv_b, d_skip, mt3, sel_m, sel_s)


def _out_ffn_kernel(xctx_ref, xlat_ref, ya_ref, yb_ref, hf_ref, hb_ref, o_ref, yf_ref, ybw_ref, z_ref,
                    gm_ref, gs_ref, mod_ref, g2_ref, wo_ref, w1_ref, w2_ref, gfin_ref, out_ref,
                    *, final, nt_ctx, q_off):
    if q_off >= nt_ctx:
        x_in = xlat_ref[...]
    else:
        x_in = jnp.where(pl.program_id(1) + q_off < nt_ctx, xctx_ref[...], xlat_ref[...])
    low = _lane_is_low((1, LANES))
    ym = []
    for pair in range(2):
        sl = slice(pair * LANES, (pair + 1) * LANES)
        h = hf_ref[:, sl] + hb_ref[:, sl]
        hn = h * lax.rsqrt(_head_mean_square(h, low) + EPS) * gm_ref[:, sl]
        ym.append((hn * jax.nn.sigmoid(o_ref[:, sl])).astype(MXU_DTYPE))
    ys = []
    for grp in range(2):
        sl = slice(grp * LANES, (grp + 1) * LANES)
        z = z_ref[:, sl]
        ys.append((yf_ref[:, sl] + ybw_ref[:, sl]) * (z * jax.nn.sigmoid(z)))
    ms = (jnp.sum(ys[0] * ys[0], axis=-1, keepdims=True)
          + jnp.sum(ys[1] * ys[1], axis=-1, keepdims=True)) * (1.0 / GROUP_WIDTH)
    rs = lax.rsqrt(ms + EPS)
    yd = [(ys[grp] * rs * gs_ref[:, grp * LANES:(grp + 1) * LANES]).astype(MXU_DTYPE) for grp in range(2)]

    y = (_dot(ya_ref[...], wo_ref[0:GROUP_WIDTH, :])
         + _dot(yb_ref[...], wo_ref[GROUP_WIDTH:2 * GROUP_WIDTH, :])
         + _dot(jnp.concatenate(ym, axis=1), wo_ref[2 * GROUP_WIDTH:3 * GROUP_WIDTH, :])
         + _dot(jnp.concatenate(yd, axis=1), wo_ref[3 * GROUP_WIDTH:, :]))
    x1 = x_in + mod_ref[2:3, :] * y
    h2 = _rmsnorm_mod(x1, g2_ref[...], mod_ref[3:4, :], mod_ref[4:5, :]).astype(MXU_DTYPE)
    acc = jnp.zeros(x1.shape, F32)
    for f in range(D_FF // FF_CHUNK):
        sl = slice(f * FF_CHUNK, (f + 1) * FF_CHUNK)
        hf = jnp.maximum(_dot(h2, w1_ref[:, sl]), 0.0)
        acc = acc + _dot((hf * hf).astype(MXU_DTYPE), w2_ref[sl, :])
    x2 = x1 + mod_ref[5:6, :] * acc
    if final:
        ms2 = jnp.mean(x2 * x2, axis=-1, keepdims=True)
        x2 = x2 * lax.rsqrt(ms2 + EPS) * gfin_ref[...]
    out_ref[...] = x2


def _out_ffn(x_ctx, x_lat, lat_off, ya, yb, hf, hb, yf, ybw, pf, gm, gs, modv, g2, wo, w1, w2, g_final,
             n_ctx, final):
    bsz, u, _ = pf.shape
    d = x_lat.shape[-1]
    nt_ctx = n_ctx // ROW_TILE
    q_off = nt_ctx if final else 0
    nt = u // ROW_TILE - q_off
    row = lambda b, i: (b, i + q_off, 0)
    const = lambda b, i: (0, 0)
    grp_tile = pl.BlockSpec((None, ROW_TILE, GROUP_WIDTH), row)
    attn_tile = pl.BlockSpec((None, ROW_TILE, GROUP_WIDTH), lambda b, i: (b, i, 0))
    return pl.pallas_call(
        functools.partial(_out_ffn_kernel, final=final, nt_ctx=nt_ctx, q_off=q_off),
        grid=(bsz, nt),
        in_specs=_residual_specs(nt_ctx, lat_off, q_off, d) + [
            attn_tile, attn_tile, grp_tile, grp_tile,
            pl.BlockSpec((None, ROW_TILE, GROUP_WIDTH), lambda b, i: (b, i + q_off, _O_BLK)),
            grp_tile, grp_tile,
            pl.BlockSpec((None, ROW_TILE, GROUP_WIDTH), lambda b, i: (b, i + q_off, _Z_BLK)),
            pl.BlockSpec((1, GROUP_WIDTH), const),
            pl.BlockSpec((1, GROUP_WIDTH), const),
            _mod_spec(nt_ctx, q_off, d),
            pl.BlockSpec((1, d), const),
            pl.BlockSpec((d, d), const, pipeline_mode=_RESIDENT),
            pl.BlockSpec((d, D_FF), const, pipeline_mode=_RESIDENT),
            pl.BlockSpec((D_FF, d), const, pipeline_mode=_RESIDENT),
            pl.BlockSpec((1, d), const),
        ],
        out_specs=pl.BlockSpec((None, ROW_TILE, d), lambda b, i: (b, i, 0)),
        out_shape=jax.ShapeDtypeStruct((bsz, nt * ROW_TILE, d), F32),
        compiler_params=_params("parallel", "parallel"),
        name="out_ffn_final" if final else "out_ffn",
    )(x_ctx, x_lat, ya, yb, hf, hb, pf, yf, ybw, pf, gm, gs, modv, g2, wo, w1, w2, g_final)


def _gate_starts():
    cg, ddt = _OFF[10], _OFF[15]
    return [cg, cg + N_HEADS, cg + 2 * N_HEADS, cg + 3 * N_HEADS, ddt, ddt + N_HEADS]


def _gate_lanes(groups):
    pad = jnp.zeros((8 - N_HEADS,), F32)
    parts = []
    for g in groups:
        parts += [g.astype(F32), pad]
    parts.append(jnp.zeros((LANES - 8 * len(groups),), F32))
    return jnp.concatenate(parts).reshape(1, LANES)


def _proj_weight(w_in):
    (aq, ak, av, bq, bk, bv, cq, ck, cv, co, cg, dx, dz, db, dc, ddt) = _OFF[:-1]
    qscale = HEAD_DIM ** -0.5
    segs = []
    for base, scale in ((aq, qscale), (bq, 1.0)):
        segs += [(base + h * HEAD_DIM, HEAD_DIM, scale) for h in _Q_HEAD_ORDER]
        segs += [(base + GROUP_WIDTH, 2 * LANES, 1.0)]
    segs += [(cq, GROUP_WIDTH, qscale), (ck, 2 * GROUP_WIDTH, 1.0)]
    segs += [(dx, GROUP_WIDTH, 1.0), (db, 2 * GROUP_WIDTH, 1.0), (co, GROUP_WIDTH, 1.0), (dz, GROUP_WIDTH, 1.0)]
    parts = [w_in[:, s:s + n] if scale == 1.0 else w_in[:, s:s + n] * scale for s, n, scale in segs]
    zeros4 = jnp.zeros((w_in.shape[0], 8 - N_HEADS), w_in.dtype)
    for start in _gate_starts():
        parts += [w_in[:, start:start + N_HEADS], zeros4]
    parts.append(jnp.zeros((w_in.shape[0], LANES - 8 * len(_gate_starts())), w_in.dtype))
    w = jnp.concatenate(parts, axis=1)
    assert w.shape[1] == _N_PROJ
    return w.astype(MXU_DTYPE)


def _out_weight(w_out):
    parts = []
    for base in (0, GROUP_WIDTH):
        parts += [w_out[base + h * HEAD_DIM:base + (h + 1) * HEAD_DIM] for h in _Q_HEAD_ORDER]
    parts.append(w_out[2 * GROUP_WIDTH:])
    return jnp.concatenate(parts, axis=0).astype(MXU_DTYPE)


def _rope_tables(n_tok, n_ctx):
    t = jnp.arange(n_tok)
    row = (t // GRID_W).astype(F32)
    col = (t % GRID_W).astype(F32)
    half = HEAD_DIM // 2
    inv_freq = ROPE_BASE ** (-jnp.arange(0, half, 2, dtype=F32) / half)
    lane = np.arange(LANES)
    hd = lane % HEAD_DIM
    use_col = (hd // half) == 1
    pos = jnp.where(use_col[None, :], col[:, None], row[:, None])
    ang = pos * inv_freq[hd % (half // 2)][None, :]
    first = (hd % half) < (half // 2)
    cos = jnp.cos(ang)
    sin = jnp.where(first[None, :], -jnp.sin(ang), jnp.sin(ang))
    cos = jnp.concatenate([jnp.ones((n_ctx, LANES), F32), cos], axis=0)
    sin = jnp.concatenate([jnp.zeros((n_ctx, LANES), F32), sin], axis=0)
    return cos, sin


def kernel(x, c, ctx, c_ctx, w_ada, b_ada, g_norm1, g_norm2, w_in, sink_a, g_q_b, g_k_b, b_igate, b_fgate,
           g_mlstm, conv_w, conv_b, a_log, dt_bias, d_skip, g_ssm, w_out, w_ff1, w_ff2, g_final):
    bsz, n_tok, d = x.shape
    n_ctx = ctx.shape[1]
    depth = w_in.shape[0]
    assert d == D_MODEL and n_ctx % ROW_TILE == 0 and n_tok % ROW_TILE == 0

    cos, sin = _rope_tables(n_tok, n_ctx)
    n_rows = 16
    cvec = jnp.concatenate([c, c_ctx[None, :], jnp.zeros((n_rows - bsz - 1, d), F32)], axis=0)
    mod_all = _ada_mod(cvec, w_ada, b_ada)

    u = n_ctx + n_tok
    x_ctx, x_lat, lat_off = ctx, x, 0
    for layer in range(depth):
        need_ctx = layer < depth - 1
        mod = mod_all[layer].reshape(n_rows, 6, d)
        pad = jnp.zeros((bsz, 2, d), F32)
        mod_lat = jnp.concatenate([mod[:bsz], pad], axis=1)
        mod_ctx = jnp.broadcast_to(jnp.concatenate([mod[bsz], pad[0]], axis=0), (bsz, 8, d))
        modv = jnp.stack([mod_ctx, mod_lat], axis=1)

        gq = jnp.tile(g_q_b[layer] * (HEAD_DIM ** -0.5 * LOG2E), 2).reshape(1, LANES)
        gk = jnp.tile(g_k_b[layer], 2).reshape(1, LANES)
        pa, pb, pc, pf = _in_proj(x_ctx, x_lat, lat_off, u, modv, g_norm1[layer].reshape(1, d),
                                  _proj_weight(w_in[layer]), cos, sin, gq, gk, n_ctx)

        ya = _attn_a(pa, sink_a[layer], n_ctx, need_ctx)
        yb = _attn_b(pb, n_ctx, need_ctx)

        gbias = _gate_lanes([b_igate[layer, 0], b_fgate[layer, 0], b_igate[layer, 1], b_fgate[layer, 1],
                             dt_bias[layer, 0], dt_bias[layer, 1]])
        alog = jnp.broadcast_to(jnp.pad(a_log[layer], ((0, 0), (0, 8 - N_HEADS)))[:, :, None], (2, 8, LANES))
        dsk = jnp.repeat(d_skip[layer], HEAD_DIM).reshape(1, GROUP_WIDTH)
        hf, hb, yf, ybw = _scans(pc, pf, gbias, alog, conv_w[layer], conv_b[layer].reshape(1, -1), dsk, n_ctx)

        xc = _out_ffn(x_ctx, x_lat, lat_off, ya, yb, hf, hb, yf, ybw, pf, g_mlstm[layer].reshape(1, GROUP_WIDTH),
                      g_ssm[layer].reshape(1, GROUP_WIDTH), modv, g_norm2[layer].reshape(1, d),
                      _out_weight(w_out[layer]), w_ff1[layer].astype(MXU_DTYPE), w_ff2[layer].astype(MXU_DTYPE),
                      g_final.reshape(1, d), n_ctx, final=not need_ctx)
        x_ctx, x_lat, lat_off = xc, xc, n_ctx // ROW_TILE
    return xc
```

```python
import functools
import math

import numpy as np
import jax
import jax.numpy as jnp
from jax import lax
from jax.experimental import pallas as pl
from jax.experimental.pallas import tpu as pltpu

F32 = jnp.float32
MXU_DTYPE = jnp.bfloat16

D_MODEL = 1024
HEAD_DIM = 64
LANES = 128
GRID_W = 64
WINDOW = 128
ROPE_BASE = 10000.0
EPS = 1e-6
N_HEADS = 4
GROUP_WIDTH = N_HEADS * HEAD_DIM
D_STATE = 128
CHUNK = 128
ROW_TILE = 256
CHUNKS_PER_TILE = ROW_TILE // CHUNK
KEY_TILE = 1024
D_FF = 4 * D_MODEL
FF_CHUNK = 1024
HALO = 8
VMEM_LIMIT = 56 * 1024 * 1024
LOG2E = math.log2(math.e)
EXP_ARG_MAX = 88.0

_SPLIT_SIZES = (256, 128, 128, 256, 128, 128, 256, 256, 256, 256, 16, 256, 256, 256, 256, 8)
_OFF = [int(o) for o in np.concatenate([[0], np.cumsum(_SPLIT_SIZES)])]
_Q_HEAD_ORDER = (0, 2, 1, 3)

_COL_A = 0
_COL_B = 512
_COL_C = 1024
_COL_F = 1792
_N_F = 768 + 256 + 256 + 128
_N_PROJ = _COL_F + _N_F
_GATE_BLK = (_N_F - LANES) // LANES
_O_BLK = 768 // GROUP_WIDTH
_Z_BLK = (768 + GROUP_WIDTH) // GROUP_WIDTH


def _dot(a, b):
    return jnp.dot(a, b, preferred_element_type=F32)


def _dot_nt(a, b):
    return lax.dot_general(a, b, (((1,), (1,)), ((), ())), preferred_element_type=F32)


def _dot_tn(a, b):
    return lax.dot_general(a, b, (((0,), (0,)), ((), ())), preferred_element_type=F32)


def _split3(x):
    hi = x.astype(MXU_DTYPE)
    r1 = x - hi.astype(F32)
    mid = r1.astype(MXU_DTYPE)
    lo = (r1 - mid.astype(F32)).astype(MXU_DTYPE)
    return [hi, mid, lo]


def _rows_cumsum(rows, mt3):
    return _dot(jnp.concatenate(_split3(rows), axis=1), mt3)


def _rows_to_columns(rows, sel3):
    return _dot_tn(jnp.concatenate(_split3(rows), axis=0), sel3)


def _cummax_lanes(x, direction):
    n = x.shape[-1]
    lane = lax.broadcasted_iota(jnp.int32, x.shape, 1)

    def shifted(v, shift):
        if direction == 0:
            return jnp.where(lane >= shift, pltpu.roll(v, shift, 1), -jnp.inf)
        return jnp.where(lane < n - shift, pltpu.roll(v, n - shift, 1), -jnp.inf)

    def tree_max(vals):
        while len(vals) > 1:
            vals = [jnp.maximum(a, b) for a, b in zip(vals[::2], vals[1::2])] + ([vals[-1]] if len(vals) % 2 else [])
        return vals[0]

    window = tree_max([x] + [shifted(x, s) for s in range(1, 8)])
    return tree_max([window] + [shifted(window, s) for s in range(8, n, 8)])


def _lane_is_low(shape):
    lane = lax.broadcasted_iota(jnp.int32, shape, len(shape) - 1)
    return (lane % LANES) < HEAD_DIM


def _head_mean_square(t, low):
    sq = t * t
    lo = jnp.sum(jnp.where(low, sq, 0.0), axis=-1, keepdims=True)
    hi = jnp.sum(jnp.where(low, 0.0, sq), axis=-1, keepdims=True)
    return jnp.where(low, lo, hi) * (1.0 / HEAD_DIM)


def _params(*sem):
    return pltpu.CompilerParams(dimension_semantics=sem, vmem_limit_bytes=VMEM_LIMIT)


_RESIDENT = pl.Buffered(1)


def _ada_kernel(c_ref, w_ref, b_ref, out_ref):
    cv = c_ref[...]
    cv = cv * jax.nn.sigmoid(cv)
    out_ref[...] = _dot(cv.astype(MXU_DTYPE), w_ref[...].astype(MXU_DTYPE)) + b_ref[...]


def _ada_mod(cvec, w_ada, b_ada):
    depth, d, n = w_ada.shape
    rows = cvec.shape[0]
    tn = 1536
    return pl.pallas_call(
        _ada_kernel,
        grid=(depth, n // tn),
        in_specs=[
            pl.BlockSpec((rows, d), lambda l, j: (0, 0)),
            pl.BlockSpec((None, d, tn), lambda l, j: (l, 0, j)),
            pl.BlockSpec((None, 1, tn), lambda l, j: (l, 0, j)),
        ],
        out_specs=pl.BlockSpec((None, rows, tn), lambda l, j: (l, 0, j)),
        out_shape=jax.ShapeDtypeStruct((depth, rows, n), F32),
        compiler_params=_params("parallel", "parallel"),
        name="ada_mod",
    )(cvec, w_ada, b_ada.reshape(depth, 1, n))


def _rmsnorm_mod(x, g, shift, scale):
    ms = jnp.mean(x * x, axis=-1, keepdims=True)
    return (x * lax.rsqrt(ms + EPS) * g) * (1.0 + scale) + shift


def _in_proj_kernel(xctx_ref, xlat_ref, mod_ref, g1_ref, w_ref, cos_ref, sin_ref, gq_ref, gk_ref,
                    a_ref, b_ref, c_ref, f_ref, *, nt_ctx):
    x = jnp.where(pl.program_id(1) < nt_ctx, xctx_ref[...], xlat_ref[...])
    hn = _rmsnorm_mod(x, g1_ref[...], mod_ref[0:1, :], mod_ref[1:2, :])
    hb = hn.astype(MXU_DTYPE)
    cos = cos_ref[...]
    sin = sin_ref[...]
    lane = lax.broadcasted_iota(jnp.int32, (1, LANES), 1)
    first = (lane % 32) < 16
    low = _lane_is_low((1, LANES))

    def rope(t):
        partner = jnp.where(first, pltpu.roll(t, LANES - 16, 1), pltpu.roll(t, 16, 1))
        return t * cos + partner * sin

    pa = _dot(hb, w_ref[:, _COL_A:_COL_A + 512])
    for j in range(3):
        a_ref[:, j * LANES:(j + 1) * LANES] = rope(pa[:, j * LANES:(j + 1) * LANES]).astype(a_ref.dtype)
    a_ref[:, 3 * LANES:] = pa[:, 3 * LANES:].astype(a_ref.dtype)

    pb = _dot(hb, w_ref[:, _COL_B:_COL_B + 512])
    for j in range(3):
        t = pb[:, j * LANES:(j + 1) * LANES]
        g = gq_ref[...] if j < 2 else gk_ref[...]
        t = t * lax.rsqrt(_head_mean_square(t, low) + EPS) * g
        b_ref[:, j * LANES:(j + 1) * LANES] = rope(t).astype(b_ref.dtype)
    b_ref[:, 3 * LANES:] = pb[:, 3 * LANES:].astype(b_ref.dtype)

    c_ref[...] = _dot(hb, w_ref[:, _COL_C:_COL_F]).astype(c_ref.dtype)
    f_ref[...] = _dot(hb, w_ref[:, _COL_F:])


def _residual_specs(nt_ctx, lat_off, q_off, d):
    ctx_spec = pl.BlockSpec((None, ROW_TILE, d), lambda b, i: (b, jnp.minimum(i + q_off, nt_ctx - 1), 0))
    lat_spec = pl.BlockSpec((None, ROW_TILE, d),
                            lambda b, i: (b, jnp.maximum(i + q_off - nt_ctx, 0) + lat_off, 0))
    return [ctx_spec, lat_spec]


def _mod_spec(nt_ctx, q_off, d):
    return pl.BlockSpec((None, None, 8, d), lambda b, i: (b, jnp.where(i + q_off < nt_ctx, 0, 1), 0, 0))


def _in_proj(x_ctx, x_lat, lat_off, u, modv, g1, w, cos, sin, gq, gk, n_ctx):
    bsz, _, d = x_lat.shape
    nt = u // ROW_TILE
    nt_ctx = n_ctx // ROW_TILE
    row = lambda b, i: (b, i, 0)
    const = lambda b, i: (0, 0)
    return pl.pallas_call(
        functools.partial(_in_proj_kernel, nt_ctx=nt_ctx),
        grid=(bsz, nt),
        in_specs=_residual_specs(nt_ctx, lat_off, 0, d) + [
            _mod_spec(nt_ctx, 0, d),
            pl.BlockSpec((1, d), const),
            pl.BlockSpec((d, _N_PROJ), const, pipeline_mode=_RESIDENT),
            pl.BlockSpec((ROW_TILE, LANES), lambda b, i: (i, 0)),
            pl.BlockSpec((ROW_TILE, LANES), lambda b, i: (i, 0)),
            pl.BlockSpec((1, LANES), const),
            pl.BlockSpec((1, LANES), const),
        ],
        out_specs=[
            pl.BlockSpec((None, ROW_TILE, 512), row),
            pl.BlockSpec((None, ROW_TILE, 512), row),
            pl.BlockSpec((None, ROW_TILE, 768), row),
            pl.BlockSpec((None, ROW_TILE, _N_F), row),
        ],
        out_shape=[
            jax.ShapeDtypeStruct((bsz, u, 512), MXU_DTYPE),
            jax.ShapeDtypeStruct((bsz, u, 512), MXU_DTYPE),
            jax.ShapeDtypeStruct((bsz, u, 768), MXU_DTYPE),
            jax.ShapeDtypeStruct((bsz, u, _N_F), F32),
        ],
        compiler_params=_params("parallel", "parallel"),
        name="in_proj",
    )(x_ctx, x_lat, modv, g1, w, cos, sin, gq, gk)


def _stack_heads(q_ref, qs_ref):
    low = _lane_is_low((1, LANES))
    for jb in range(2):
        q = q_ref[:, jb * LANES:(jb + 1) * LANES]
        for half in range(2):
            idx = 2 * jb + half
            keep = low if half == 0 else jnp.logical_not(low)
            qs_ref[idx * ROW_TILE:(idx + 1) * ROW_TILE, :] = jnp.where(keep, q, jnp.zeros_like(q))


def _unstack_heads(o, out_ref):
    low = _lane_is_low((1, LANES))
    for jb in range(2):
        lo = o[(2 * jb) * ROW_TILE:(2 * jb + 1) * ROW_TILE, :]
        hi = o[(2 * jb + 1) * ROW_TILE:(2 * jb + 2) * ROW_TILE, :]
        out_ref[:, jb * LANES:(jb + 1) * LANES] = jnp.where(low, lo, hi).astype(out_ref.dtype)


def _attn_a_kernel(sink_ref, q_ref, kp_ref, ko_ref, kn_ref, kc_ref, vp_ref, vo_ref, vn_ref, vc_ref,
                   out_ref, qs_ref, kbuf, vbuf, s_ref, o_ref, *, n_tok, n_ctx, q_off):
    i = pl.program_id(1) + q_off
    nb = 2 * WINDOW + ROW_TILE
    nk = nb + n_ctx
    kbuf[0:WINDOW] = kp_ref[...]
    kbuf[WINDOW:WINDOW + ROW_TILE] = ko_ref[...]
    kbuf[WINDOW + ROW_TILE:nb] = kn_ref[...]
    kbuf[nb:nk] = kc_ref[...]
    vbuf[0:WINDOW] = vp_ref[...]
    vbuf[WINDOW:WINDOW + ROW_TILE] = vo_ref[...]
    vbuf[WINDOW + ROW_TILE:nb] = vn_ref[...]
    vbuf[nb:nk] = vc_ref[...]
    _stack_heads(q_ref, qs_ref)

    r = lax.broadcasted_iota(jnp.int32, (ROW_TILE, nk), 0)
    c = lax.broadcasted_iota(jnp.int32, (ROW_TILE, nk), 1)
    kpos = (i - 1) * ROW_TILE - WINDOW + c
    dist = c - r
    band = jnp.where(dist >= 0, jnp.where(dist <= 2 * WINDOW, 1, 0), 0)
    band = jnp.where(kpos >= 0, jnp.where(kpos < n_tok, band, 0), 0)
    band = jnp.where(i >= 1, band, 0)
    bias = jnp.where(c >= nb, 0.0, jnp.where(band > 0, 0.0, -jnp.inf)).astype(F32)

    for idx in range(N_HEADS):
        rows = slice(idx * ROW_TILE, (idx + 1) * ROW_TILE)
        s_ref[rows, :] = _dot_nt(qs_ref[rows, :], kbuf[...]) + bias
    for idx in range(N_HEADS):
        rows = slice(idx * ROW_TILE, (idx + 1) * ROW_TILE)
        sink = sink_ref[_Q_HEAD_ORDER[idx]]
        m = jnp.maximum(jnp.max(s_ref[rows, :], axis=-1, keepdims=True), sink)
        p = jnp.exp(s_ref[rows, :] - m)
        l = jnp.sum(p, axis=-1, keepdims=True) + jnp.exp(sink - m)
        o_ref[rows, :] = _dot(p.astype(MXU_DTYPE), vbuf[...]) / l
    _unstack_heads(o_ref[...], out_ref)


def _attn_a(pa, sink, n_ctx, need_ctx):
    bsz, u, _ = pa.shape
    n_tok = u - n_ctx
    q_off = 0 if need_ctx else n_ctx // ROW_TILE
    nt = u // ROW_TILE - q_off
    last_blk = u // WINDOW - 1
    rpw = ROW_TILE // WINDOW
    nk = 2 * WINDOW + ROW_TILE + n_ctx

    def own(col):
        return lambda b, i: (b, i + q_off, col)

    def prev(col):
        return lambda b, i: (b, jnp.maximum((i + q_off) * rpw - 1, 0), col)

    def nxt(col):
        return lambda b, i: (b, jnp.minimum((i + q_off + 1) * rpw, last_blk), col)

    def ctx(col):
        return lambda b, i: (b, 0, col)

    kern = functools.partial(_attn_a_kernel, n_tok=n_tok, n_ctx=n_ctx, q_off=q_off)
    return pl.pallas_call(
        kern,
        grid=(bsz, nt),
        in_specs=[
            pl.BlockSpec(memory_space=pltpu.SMEM),
            pl.BlockSpec((None, ROW_TILE, 2 * LANES), own(0)),
            pl.BlockSpec((None, WINDOW, LANES), prev(2)),
            pl.BlockSpec((None, ROW_TILE, LANES), own(2)),
            pl.BlockSpec((None, WINDOW, LANES), nxt(2)),
            pl.BlockSpec((None, n_ctx, LANES), ctx(2)),
            pl.BlockSpec((None, WINDOW, LANES), prev(3)),
            pl.BlockSpec((None, ROW_TILE, LANES), own(3)),
            pl.BlockSpec((None, WINDOW, LANES), nxt(3)),
            pl.BlockSpec((None, n_ctx, LANES), ctx(3)),
        ],
        out_specs=pl.BlockSpec((None, ROW_TILE, 2 * LANES), lambda b, i: (b, i, 0)),
        out_shape=jax.ShapeDtypeStruct((bsz, nt * ROW_TILE, 2 * LANES), MXU_DTYPE),
        scratch_shapes=[pltpu.VMEM((N_HEADS * ROW_TILE, LANES), MXU_DTYPE),
                        pltpu.VMEM((nk, LANES), MXU_DTYPE), pltpu.VMEM((nk, LANES), MXU_DTYPE),
                        pltpu.VMEM((N_HEADS * ROW_TILE, nk), F32),
                        pltpu.VMEM((N_HEADS * ROW_TILE, LANES), F32)],
        compiler_params=_params("parallel", "parallel"),
        name="attn_window",
    )(sink, pa, pa, pa, pa, pa, pa, pa, pa, pa)


def _attn_b_kernel(q_ref, k_ref, v_ref, out_ref, qs_ref, vext_ref, s_ref, p_ref, m_ref, alpha_ref, acc_ref,
                   *, n_ctx, n_tok, q_off, tk):
    qi = pl.program_id(1) + q_off
    n_rows = N_HEADS * ROW_TILE

    @pl.when(pl.program_id(1) == 0)
    def _():
        vext_ref[:, 0:LANES] = v_ref[...]
        vext_ref[:, LANES:] = jnp.ones((vext_ref.shape[0], LANES), vext_ref.dtype)

    _stack_heads(q_ref, qs_ref)

    def scores(slot, start, size):
        s_ref[slot, :, 0:size] = _dot_nt(qs_ref[...], k_ref[start:start + size, :])

    def softmax(slot, size, first):
        m_new = jnp.max(s_ref[slot, :, 0:size], axis=-1, keepdims=True)
        if not first:
            m_old = m_ref[...]
            m_new = jnp.maximum(m_old, m_new)
            alpha_ref[slot] = jnp.exp2(m_old - m_new)
        m_ref[...] = m_new
        p_ref[slot, :, 0:size] = jnp.exp2(s_ref[slot, :, 0:size] - m_new).astype(p_ref.dtype)

    def accumulate(slot, start, size, first):
        pv = _dot(p_ref[slot, :, 0:size], vext_ref[start:start + size, :])
        if first:
            acc_ref[...] = pv
        else:
            acc_ref[...] = alpha_ref[slot] * acc_ref[...] + pv

    def attend(tiles):
        scores(0, *tiles[0])
        for t, (start, size) in enumerate(tiles):
            if t + 1 < len(tiles):
                scores((t + 1) % 2, *tiles[t + 1])
            softmax(t % 2, size, t == 0)
            accumulate(t % 2, start, size, t == 0)
        _unstack_heads(acc_ref[:, 0:LANES] / acc_ref[:, LANES:], out_ref)

    ctx_tiles = [(0, n_ctx)]
    all_tiles = ctx_tiles + [(n_ctx + t * tk, tk) for t in range(n_tok // tk)]
    if q_off == 0:
        pl.when(qi == 0)(functools.partial(attend, ctx_tiles))
        pl.when(qi > 0)(functools.partial(attend, all_tiles))
    else:
        attend(all_tiles)


def _attn_b(pb, n_ctx, need_ctx):
    bsz, u, _ = pb.shape
    n_tok = u - n_ctx
    q_off = 0 if need_ctx else n_ctx // ROW_TILE
    nt = u // ROW_TILE - q_off
    tk = min(KEY_TILE, n_tok)
    assert n_tok % tk == 0
    kern = functools.partial(_attn_b_kernel, n_ctx=n_ctx, n_tok=n_tok, q_off=q_off, tk=tk)
    return pl.pallas_call(
        kern,
        grid=(bsz, nt),
        in_specs=[
            pl.BlockSpec((None, ROW_TILE, 2 * LANES), lambda b, i: (b, i + q_off, 0)),
            pl.BlockSpec((None, u, LANES), lambda b, i: (b, 0, 2)),
            pl.BlockSpec((None, u, LANES), lambda b, i: (b, 0, 3)),
        ],
        out_specs=pl.BlockSpec((None, ROW_TILE, 2 * LANES), lambda b, i: (b, i, 0)),
        out_shape=jax.ShapeDtypeStruct((bsz, nt * ROW_TILE, 2 * LANES), MXU_DTYPE),
        scratch_shapes=[
            pltpu.VMEM((N_HEADS * ROW_TILE, LANES), MXU_DTYPE),
            pltpu.VMEM((u, 2 * LANES), MXU_DTYPE),
            pltpu.VMEM((2, N_HEADS * ROW_TILE, tk), F32),
            pltpu.VMEM((2, N_HEADS * ROW_TILE, tk), MXU_DTYPE),
            pltpu.VMEM((N_HEADS * ROW_TILE, 1), F32),
            pltpu.VMEM((2, N_HEADS * ROW_TILE, 1), F32),
            pltpu.VMEM((N_HEADS * ROW_TILE, 2 * LANES), F32),
        ],
        compiler_params=_params("parallel", "arbitrary"),
        name="attn_dense",
    )(pb, pb, pb)


def _scan_tile(j, direction, nt_ctx, nt):
    if direction == 0:
        return j
    return jnp.where(j < nt_ctx, nt_ctx - 1 - j, nt + nt_ctx - 1 - j)


def _causal_mask(direction):
    t = lax.broadcasted_iota(jnp.int32, (CHUNK, CHUNK), 0)
    s = lax.broadcasted_iota(jnp.int32, (CHUNK, CHUNK), 1)
    return (s <= t) if direction == 0 else (s >= t)


def _chunk_order(direction):
    order = range(CHUNKS_PER_TILE)
    return order if direction == 0 else reversed(order)


def _log_sigmoid(x):
    return jnp.minimum(x, 0.0) - jnp.log1p(jnp.exp(-jnp.abs(x)))


def _softplus(x):
    return jnp.maximum(x, 0.0) + jnp.log1p(jnp.exp(-jnp.abs(x)))


def _mlstm_gates(direction, g_rows, m_prev, mt3):
    end = CHUNK - 1 if direction == 0 else 0
    i_r = jnp.concatenate([g[16 * direction:16 * direction + 8] for g in g_rows], axis=0)
    f_r = jnp.concatenate([g[16 * direction + 8:16 * direction + 16] for g in g_rows], axis=0)
    b_r = _rows_cumsum(_log_sigmoid(f_r), mt3)
    r_r = i_r - b_r
    cmax = _cummax_lanes(r_r, direction)
    tot = jnp.broadcast_to(b_r[:, end:end + 1], b_r.shape)
    g2 = tot - b_r + i_r
    g2max = jnp.broadcast_to(jnp.max(g2, axis=-1, keepdims=True), g2.shape)
    per_chunk = []
    for idx in range(len(g_rows)):
        sl = slice(8 * idx, 8 * idx + 8)
        c_r = jnp.maximum(m_prev, cmax[sl])
        dp_r = jnp.exp(m_prev - c_r)
        em_r = jnp.exp(jnp.minimum(-(b_r[sl] + c_r), EXP_ARG_MAX))
        m_new = jnp.maximum(tot[sl] + m_prev, g2max[sl])
        wk_r = jnp.exp(g2[sl] - m_new)
        cd_r = jnp.exp(tot[sl] + m_prev - m_new)
        per_chunk.append((jnp.concatenate([c_r, dp_r, em_r, wk_r], axis=0), r_r[sl], cd_r))
        m_prev = m_new
    return per_chunk, m_prev


def _mlstm_scores(qkvs):
    low1 = _lane_is_low((1, LANES))
    ones = jnp.ones((CHUNK, LANES), MXU_DTYPE)
    pre = []
    for qkv in qkvs:
        per_pair = []
        for pair in range(2):
            q = qkv[:, pair * LANES:(pair + 1) * LANES]
            k = qkv[:, GROUP_WIDTH + pair * LANES:GROUP_WIDTH + (pair + 1) * LANES]
            v = qkv[:, 2 * GROUP_WIDTH + pair * LANES:2 * GROUP_WIDTH + (pair + 1) * LANES]
            qk = [_dot_nt(jnp.where(low1 if half == 0 else jnp.logical_not(low1), q, jnp.zeros_like(q)), k)
                  for half in range(2)]
            per_pair.append((q, k, jnp.concatenate([v, ones], axis=1), qk))
        pre.append(per_pair)
    return pre


def _mlstm_intra(items, pre):
    low1 = _lane_is_low((1, LANES))
    ones = jnp.ones((CHUNK, LANES), MXU_DTYPE)
    r2 = lax.broadcasted_iota(jnp.int32, (LANES, 2 * LANES), 0)
    c2 = lax.broadcasted_iota(jnp.int32, (LANES, 2 * LANES), 1)
    state_mask = (r2 < HEAD_DIM) == ((c2 % LANES) < HEAD_DIM)
    mid = []
    for (direction, qkv, r_r, cd_r, cols), per_pair in zip(items, pre):
        mask = _causal_mask(direction)
        res = []
        for pair, (q, k, vext, qk) in enumerate(per_pair):
            svs, dens = [], []
            for half in range(2):
                h = 2 * pair + half
                w = jnp.exp(jnp.where(mask, r_r[h:h + 1, :] - cols[:, h * LANES:(h + 1) * LANES], -jnp.inf))
                s = qk[half] * w
                s_hi = s.astype(MXU_DTYPE)
                s_lo = (s - s_hi.astype(F32)).astype(MXU_DTYPE)
                sv = _dot(s_hi, vext)
                svs.append(sv[:, :LANES])
                dens.append(sv[:, LANES:] + _dot(s_lo, ones))
            wk = cols[:, (8 + pair) * LANES:(9 + pair) * LANES]
            upd = _dot_tn((k.astype(F32) * wk).astype(MXU_DTYPE), vext)
            res.append((jnp.where(low1, svs[0], svs[1]), jnp.where(low1, dens[0], dens[1]),
                        jnp.where(state_mask, upd, 0.0)))
        mid.append(res)
    return mid


def _mlstm_recurrence(items, pre, mid, state):
    low2 = _lane_is_low((1, 2 * LANES))
    outs = []
    for (direction, qkv, r_r, cd_r, cols), per_pair, res in zip(items, pre, mid):
        out = []
        for pair in range(2):
            q = per_pair[pair][0]
            sv, den_intra, upd = res[pair]
            prior = _dot(q, state[direction][pair].astype(MXU_DTYPE))
            dp = cols[:, (4 + pair) * LANES:(5 + pair) * LANES]
            em = cols[:, (6 + pair) * LANES:(7 + pair) * LANES]
            num = sv + dp * prior[:, :LANES]
            den = den_intra + dp * prior[:, LANES:]
            out.append(num / jnp.maximum(jnp.abs(den), em))
            cd0 = jnp.concatenate([cd_r[2 * pair:2 * pair + 1]] * 2, axis=1)
            cd1 = jnp.concatenate([cd_r[2 * pair + 1:2 * pair + 2]] * 2, axis=1)
            state[direction][pair] = state[direction][pair] * jnp.where(low2, cd0, cd1) + upd
        outs.append(out)
    return outs


def _gate_rows(g_refs, gbias_ref):
    gbias = gbias_ref[...]
    return [[(g_ref[ci * CHUNK:(ci + 1) * CHUNK, :] + gbias).T for ci in _chunk_order(direction)]
            for direction, g_ref in enumerate(g_refs)]


def _mlstm_body(g_rows, qkvf_ref, qkvb_ref, mt3_ref, sel3_ref, hf_ref, hb_ref, st_ref, m_ref):
    dirs = ((qkvf_ref, None, hf_ref), (qkvb_ref, None, hb_ref))
    work = [(direction, idx) for idx in range(CHUNKS_PER_TILE) for direction in range(2)]
    row_slices = []
    for direction, idx in work:
        ci = list(_chunk_order(direction))[idx]
        row_slices.append(slice(ci * CHUNK, (ci + 1) * CHUNK))
    gates = []
    for direction in range(2):
        per_chunk, m_new = _mlstm_gates(direction, g_rows[direction], m_ref[direction], mt3_ref[direction])
        m_ref[direction] = m_new
        gates.append(per_chunk)
    yield
    rows_all = jnp.concatenate([gates[direction][idx][0] for direction, idx in work], axis=1)
    cols_all = _rows_to_columns(rows_all, sel3_ref[...])
    qkvs = [dirs[direction][0][rows, :] for (direction, _), rows in zip(work, row_slices)]
    pre = _mlstm_scores(qkvs)
    yield
    state = [[st_ref[direction, pair] for pair in range(2)] for direction in range(2)]
    items = []
    for n, (direction, idx) in enumerate(work):
        _, r_r, cd_r = gates[direction][idx]
        items.append((direction, qkvs[n], r_r, cd_r, cols_all[n * CHUNK:(n + 1) * CHUNK, :]))
    mid = _mlstm_intra(items, pre)
    yield
    outs = _mlstm_recurrence(items, pre, mid, state)
    for (direction, *_), rows, out in zip(items, row_slices, outs):
        for pair in range(2):
            dirs[direction][2][rows, pair * LANES:(pair + 1) * LANES] = out[pair]
    for direction in range(2):
        for pair in range(2):
            st_ref[direction, pair] = state[direction][pair]


def _scan_masks():
    s = np.arange(CHUNK)[:, None]
    t = np.arange(CHUNK)[None, :]
    mats = [np.tile((s <= t).astype(np.float32), (3, 1)), np.tile((s >= t).astype(np.float32), (3, 1))]
    return jnp.asarray(np.stack(mats), MXU_DTYPE)


def _column_selector(n_head_groups, n_pair_groups):
    n_groups = n_head_groups + n_pair_groups
    n_rows = 8 * n_groups + (-8 * n_groups) % 16
    n_cols = (N_HEADS * n_head_groups + 2 * n_pair_groups) * LANES
    sel = np.zeros((n_rows, n_cols), np.float32)
    col = 0
    for g in range(n_head_groups):
        for h in range(N_HEADS):
            sel[8 * g + h, col:col + LANES] = 1.0
            col += LANES
    for g in range(n_head_groups, n_groups):
        for pair in range(2):
            sel[8 * g + 2 * pair, col:col + HEAD_DIM] = 1.0
            sel[8 * g + 2 * pair + 1, col + HEAD_DIM:col + LANES] = 1.0
            col += LANES
    return jnp.asarray(np.tile(sel, (3, 1)), MXU_DTYPE)


def _ssd_conv(x_ref, xp_ref, xn_ref, cw_ref, cb_ref, tile, nt_ctx, nt):
    xin = x_ref[...]
    has_prev = (tile != 0) & (tile != nt_ctx)
    has_next = (tile != nt_ctx - 1) & (tile != nt - 1)
    prow = jnp.where(has_prev, xp_ref[HALO - 1:HALO, :], 0.0)
    nrow = jnp.where(has_next, xn_ref[0:1, :], 0.0)
    ridx = lax.broadcasted_iota(jnp.int32, (ROW_TILE, 1), 0)
    up = jnp.where(ridx == 0, prow, pltpu.roll(xin, 1, 0))
    dn = jnp.where(ridx == ROW_TILE - 1, nrow, pltpu.roll(xin, ROW_TILE - 1, 0))
    u = cw_ref[0:1, :] * up + cw_ref[1:2, :] * xin + cw_ref[2:3, :] * dn + cb_ref[...]
    return u * jax.nn.sigmoid(u)


def _ssd_gates(direction, g_rows, neg_a, mt3):
    end = CHUNK - 1 if direction == 0 else 0
    n = len(g_rows)
    dt_r = _softplus(jnp.concatenate([g[32 + 8 * direction:40 + 8 * direction] for g in g_rows], axis=0))
    cum_r = _rows_cumsum(dt_r * jnp.concatenate([neg_a] * n, axis=0), mt3)
    tot = jnp.broadcast_to(cum_r[:, end:end + 1], cum_r.shape)
    e_r = jnp.exp(cum_r)
    wend_r = jnp.exp(tot - cum_r) * dt_r
    d_r = jnp.exp(tot)
    shift_r = cum_r - jnp.log(dt_r)
    zero = jnp.zeros((8, CHUNK), F32)
    per_chunk = []
    for idx in range(n):
        sl = slice(8 * idx, 8 * idx + 8)
        per_chunk.append((jnp.concatenate([cum_r[sl], e_r[sl], wend_r[sl], zero], axis=0),
                          cum_r[sl], shift_r[sl], d_r[sl]))
    return per_chunk


def _ssd_scores(us):
    pre = []
    for u in us:
        per_grp = []
        for grp in range(2):
            bm = u[:, GROUP_WIDTH + grp * D_STATE:GROUP_WIDTH + (grp + 1) * D_STATE].astype(MXU_DTYPE)
            cm = u[:, 2 * GROUP_WIDTH + grp * D_STATE:2 * GROUP_WIDTH + (grp + 1) * D_STATE].astype(MXU_DTYPE)
            x_pair = u[:, grp * LANES:(grp + 1) * LANES]
            per_grp.append((bm, cm, x_pair, _dot_nt(cm, bm)))
        pre.append(per_grp)
    return pre


def _ssd_intra(items, pre):
    low1 = _lane_is_low((1, LANES))
    mid = []
    for (direction, u, cum_r, shift_r, d_r, cols), per_grp in zip(items, pre):
        mask = _causal_mask(direction)
        res = []
        for grp, (bm, cm, x_pair, gmat) in enumerate(per_grp):
            xb = x_pair.astype(MXU_DTYPE)
            ys = []
            for half in range(2):
                h = 2 * grp + half
                decay_dt = jnp.exp(jnp.where(mask, cols[:, h * LANES:(h + 1) * LANES] - shift_r[h:h + 1, :],
                                             -jnp.inf))
                ys.append(_dot((gmat * decay_dt).astype(MXU_DTYPE), xb))
            w_pair = cols[:, (6 + grp) * LANES:(7 + grp) * LANES]
            upd = _dot_tn(bm, (x_pair * w_pair).astype(MXU_DTYPE))
            res.append((jnp.where(low1, ys[0], ys[1]), upd))
        mid.append(res)
    return mid


def _ssd_recurrence(items, pre, mid, state):
    low1 = _lane_is_low((1, LANES))
    outs = []
    for (direction, u, cum_r, dt_r, d_r, cols), per_grp, res in zip(items, pre, mid):
        out = []
        for grp in range(2):
            cm = per_grp[grp][1]
            y_intra, upd = res[grp]
            ch = _dot(cm, state[direction][grp].astype(MXU_DTYPE))
            e_pair = cols[:, (4 + grp) * LANES:(5 + grp) * LANES]
            out.append(y_intra + e_pair * ch)
            d_pair = jnp.where(low1, d_r[2 * grp:2 * grp + 1], d_r[2 * grp + 1:2 * grp + 2])
            state[direction][grp] = state[direction][grp] * d_pair + upd
        outs.append(out)
    return outs


def _ssd_convs(xf_ref, xfp_ref, xfn_ref, xb_ref, xbp_ref, xbn_ref, cw_ref, cb_ref, *, nt_ctx, nt):
    j = pl.program_id(1)
    convs = []
    for direction, (x_ref, xp_ref, xn_ref) in enumerate(((xf_ref, xfp_ref, xfn_ref), (xb_ref, xbp_ref, xbn_ref))):
        tile = _scan_tile(j, direction, nt_ctx, nt)
        convs.append(_ssd_conv(x_ref, xp_ref, xn_ref, cw_ref, cb_ref, tile, nt_ctx, nt))
    return convs


def _ssd_body(g_rows, convs, alog_ref, dskip_ref, mt3_ref, sel3_ref, yf_ref, yb_ref, st_ref):
    dirs = ((None, None, None, None, yf_ref), (None, None, None, None, yb_ref))
    gates = []
    for direction in range(2):
        gates.append(_ssd_gates(direction, g_rows[direction], -jnp.exp(alog_ref[direction]), mt3_ref[direction]))
    work = [(direction, idx) for idx in range(CHUNKS_PER_TILE) for direction in range(2)]
    yield
    rows_all = jnp.concatenate([gates[direction][idx][0] for direction, idx in work], axis=1)
    cols_all = _rows_to_columns(rows_all, sel3_ref[...])
    items, row_slices = [], []
    for n, (direction, idx) in enumerate(work):
        ci = list(_chunk_order(direction))[idx]
        rows = slice(ci * CHUNK, (ci + 1) * CHUNK)
        _, cum_r, dt_r, d_r = gates[direction][idx]
        items.append((direction, convs[direction][rows, :], cum_r, dt_r, d_r,
                      cols_all[n * CHUNK:(n + 1) * CHUNK, :]))
        row_slices.append(rows)
    pre = _ssd_scores([item[1] for item in items])
    yield
    mid = _ssd_intra(items, pre)
    yield
    state = [[st_ref[direction, grp] for grp in range(2)] for direction in range(2)]
    outs = _ssd_recurrence(items, pre, mid, state)
    for (direction, u, *_), rows, out in zip(items, row_slices, outs):
        for grp in range(2):
            sl = slice(grp * LANES, (grp + 1) * LANES)
            y = out[grp]
            if direction == 0:
                y = y + dskip_ref[:, sl] * u[:, sl]
            dirs[direction][4][rows, sl] = y
    for direction in range(2):
        for grp in range(2):
            st_ref[direction, grp] = state[direction][grp]


def _scans_kernel(qkvf_ref, qkvb_ref, gf_ref, gb_ref, xf_ref, xfp_ref, xfn_ref, xb_ref, xbp_ref, xbn_ref,
                  gbias_ref, alog_ref, cw_ref, cb_ref, dskip_ref, mt3_ref, selm_ref, sels_ref,
                  hf_ref, hb_ref, yf_ref, yb_ref, stm_ref, m_ref, sts_ref, *, nt_ctx, nt):
    @pl.when(pl.program_id(1) == 0)
    def _():
        stm_ref[...] = jnp.zeros(stm_ref.shape, F32)
        m_ref[...] = jnp.zeros(m_ref.shape, F32)
        sts_ref[...] = jnp.zeros(sts_ref.shape, F32)

    convs = _ssd_convs(xf_ref, xfp_ref, xfn_ref, xb_ref, xbp_ref, xbn_ref, cw_ref, cb_ref, nt_ctx=nt_ctx, nt=nt)
    g_rows = _gate_rows((gf_ref, gb_ref), gbias_ref)
    bodies = [
        _mlstm_body(g_rows, qkvf_ref, qkvb_ref, mt3_ref, selm_ref, hf_ref, hb_ref, stm_ref, m_ref),
        _ssd_body(g_rows, convs, alog_ref, dskip_ref, mt3_ref, sels_ref, yf_ref, yb_ref, sts_ref),
    ]
    while bodies:
        bodies = [body for body in bodies if next(body, "done") != "done"]


def _scans(pc, pf, gbias, alog, conv_w, conv_b, d_skip, n_ctx):
    bsz, u, _ = pf.shape
    nt = u // ROW_TILE
    nt_ctx = n_ctx // ROW_TILE
    hpt = ROW_TILE // HALO
    last_halo = u // HALO - 1
    const = lambda b, j: (0, 0)
    const3 = lambda b, j: (0, 0, 0)
    tile = [lambda b, j, d=d: _scan_tile(j, d, nt_ctx, nt) for d in range(2)]

    def tile_spec(direction, width, col):
        return pl.BlockSpec((None, ROW_TILE, width), lambda b, j: (b, tile[direction](b, j), col))

    def conv_specs(direction):
        t = tile[direction]
        return [
            tile_spec(direction, 768, 0),
            pl.BlockSpec((None, HALO, 768), lambda b, j: (b, jnp.maximum(t(b, j) * hpt - 1, 0), 0)),
            pl.BlockSpec((None, HALO, 768), lambda b, j: (b, jnp.minimum((t(b, j) + 1) * hpt, last_halo), 0)),
        ]

    out = jax.ShapeDtypeStruct((bsz, u, GROUP_WIDTH), F32)
    mt3 = _scan_masks()
    sel_m = _column_selector(1, 3)
    sel_s = _column_selector(1, 2)
    return pl.pallas_call(
        functools.partial(_scans_kernel, nt_ctx=nt_ctx, nt=nt),
        grid=(bsz, nt),
        in_specs=[tile_spec(0, 768, 0), tile_spec(1, 768, 0),
                  tile_spec(0, LANES, _GATE_BLK), tile_spec(1, LANES, _GATE_BLK)]
        + conv_specs(0) + conv_specs(1) + [
            pl.BlockSpec((1, LANES), const),
            pl.BlockSpec((2, 8, LANES), const3),
            pl.BlockSpec((3, 768), const),
            pl.BlockSpec((1, 768), const),
            pl.BlockSpec((1, GROUP_WIDTH), const),
            pl.BlockSpec(mt3.shape, const3),
            pl.BlockSpec(sel_m.shape, const),
            pl.BlockSpec(sel_s.shape, const),
        ],
        out_specs=[tile_spec(0, GROUP_WIDTH, 0), tile_spec(1, GROUP_WIDTH, 0),
                   tile_spec(0, GROUP_WIDTH, 0), tile_spec(1, GROUP_WIDTH, 0)],
        out_shape=[out, out, out, out],
        scratch_shapes=[pltpu.VMEM((2, 2, LANES, 2 * LANES), F32), pltpu.VMEM((2, 8, LANES), F32),
                        pltpu.VMEM((2, 2, D_STATE, LANES), F32)],
        compiler_params=_params("parallel", "arbitrary"),
        name="scans",
    )(pc, pc, pf, pf, pf, pf, pf, pf, pf, pf, gbias, alog, conv_w, conv_b, d_skip, mt3, sel_m, sel_s)


def _out_ffn_kernel(xctx_ref, xlat_ref, ya_ref, yb_ref, hf_ref, hb_ref, o_ref, yf_ref, ybw_ref, z_ref,
                    gm_ref, gs_ref, mod_ref, g2_ref, wo_ref, w1_ref, w2_ref, gfin_ref, out_ref,
                    *, final, nt_ctx, q_off):
    if q_off >= nt_ctx:
        x_in = xlat_ref[...]
    else:
        x_in = jnp.where(pl.program_id(1) + q_off < nt_ctx, xctx_ref[...], xlat_ref[...])
    low = _lane_is_low((1, LANES))
    ym = []
    for pair in range(2):
        sl = slice(pair * LANES, (pair + 1) * LANES)
        h = hf_ref[:, sl] + hb_ref[:, sl]
        hn = h * lax.rsqrt(_head_mean_square(h, low) + EPS) * gm_ref[:, sl]
        ym.append((hn * jax.nn.sigmoid(o_ref[:, sl])).astype(MXU_DTYPE))
    ys = []
    for grp in range(2):
        sl = slice(grp * LANES, (grp + 1) * LANES)
        z = z_ref[:, sl]
        ys.append((yf_ref[:, sl] + ybw_ref[:, sl]) * (z * jax.nn.sigmoid(z)))
    ms = (jnp.sum(ys[0] * ys[0], axis=-1, keepdims=True)
          + jnp.sum(ys[1] * ys[1], axis=-1, keepdims=True)) * (1.0 / GROUP_WIDTH)
    rs = lax.rsqrt(ms + EPS)
    yd = [(ys[grp] * rs * gs_ref[:, grp * LANES:(grp + 1) * LANES]).astype(MXU_DTYPE) for grp in range(2)]

    y = (_dot(ya_ref[...], wo_ref[0:GROUP_WIDTH, :])
         + _dot(yb_ref[...], wo_ref[GROUP_WIDTH:2 * GROUP_WIDTH, :])
         + _dot(jnp.concatenate(ym, axis=1), wo_ref[2 * GROUP_WIDTH:3 * GROUP_WIDTH, :])
         + _dot(jnp.concatenate(yd, axis=1), wo_ref[3 * GROUP_WIDTH:, :]))
    x1 = x_in + mod_ref[2:3, :] * y
    h2 = _rmsnorm_mod(x1, g2_ref[...], mod_ref[3:4, :], mod_ref[4:5, :]).astype(MXU_DTYPE)
    acc = jnp.zeros(x1.shape, F32)
    for f in range(D_FF // FF_CHUNK):
        sl = slice(f * FF_CHUNK, (f + 1) * FF_CHUNK)
        hf = jnp.maximum(_dot(h2, w1_ref[:, sl]), 0.0)
        acc = acc + _dot((hf * hf).astype(MXU_DTYPE), w2_ref[sl, :])
    x2 = x1 + mod_ref[5:6, :] * acc
    if final:
        ms2 = jnp.mean(x2 * x2, axis=-1, keepdims=True)
        x2 = x2 * lax.rsqrt(ms2 + EPS) * gfin_ref[...]
    out_ref[...] = x2


def _out_ffn(x_ctx, x_lat, lat_off, ya, yb, hf, hb, yf, ybw, pf, gm, gs, modv, g2, wo, w1, w2, g_final,
             n_ctx, final):
    bsz, u, _ = pf.shape
    d = x_lat.shape[-1]
    nt_ctx = n_ctx // ROW_TILE
    q_off = nt_ctx if final else 0
    nt = u // ROW_TILE - q_off
    row = lambda b, i: (b, i + q_off, 0)
    const = lambda b, i: (0, 0)
    grp_tile = pl.BlockSpec((None, ROW_TILE, GROUP_WIDTH), row)
    attn_tile = pl.BlockSpec((None, ROW_TILE, GROUP_WIDTH), lambda b, i: (b, i, 0))
    return pl.pallas_call(
        functools.partial(_out_ffn_kernel, final=final, nt_ctx=nt_ctx, q_off=q_off),
        grid=(bsz, nt),
        in_specs=_residual_specs(nt_ctx, lat_off, q_off, d) + [
            attn_tile, attn_tile, grp_tile, grp_tile,
            pl.BlockSpec((None, ROW_TILE, GROUP_WIDTH), lambda b, i: (b, i + q_off, _O_BLK)),
            grp_tile, grp_tile,
            pl.BlockSpec((None, ROW_TILE, GROUP_WIDTH), lambda b, i: (b, i + q_off, _Z_BLK)),
            pl.BlockSpec((1, GROUP_WIDTH), const),
            pl.BlockSpec((1, GROUP_WIDTH), const),
            _mod_spec(nt_ctx, q_off, d),
            pl.BlockSpec((1, d), const),
            pl.BlockSpec((d, d), const, pipeline_mode=_RESIDENT),
            pl.BlockSpec((d, D_FF), const, pipeline_mode=_RESIDENT),
            pl.BlockSpec((D_FF, d), const, pipeline_mode=_RESIDENT),
            pl.BlockSpec((1, d), const),
        ],
        out_specs=pl.BlockSpec((None, ROW_TILE, d), lambda b, i: (b, i, 0)),
        out_shape=jax.ShapeDtypeStruct((bsz, nt * ROW_TILE, d), F32),
        compiler_params=_params("parallel", "parallel"),
        name="out_ffn_final" if final else "out_ffn",
    )(x_ctx, x_lat, ya, yb, hf, hb, pf, yf, ybw, pf, gm, gs, modv, g2, wo, w1, w2, g_final)


def _gate_starts():
    cg, ddt = _OFF[10], _OFF[15]
    return [cg, cg + N_HEADS, cg + 2 * N_HEADS, cg + 3 * N_HEADS, ddt, ddt + N_HEADS]


def _gate_lanes(groups):
    pad = jnp.zeros((8 - N_HEADS,), F32)
    parts = []
    for g in groups:
        parts += [g.astype(F32), pad]
    parts.append(jnp.zeros((LANES - 8 * len(groups),), F32))
    return jnp.concatenate(parts).reshape(1, LANES)


def _proj_weight(w_in):
    (aq, ak, av, bq, bk, bv, cq, ck, cv, co, cg, dx, dz, db, dc, ddt) = _OFF[:-1]
    qscale = HEAD_DIM ** -0.5
    segs = []
    for base, scale in ((aq, qscale), (bq, 1.0)):
        segs += [(base + h * HEAD_DIM, HEAD_DIM, scale) for h in _Q_HEAD_ORDER]
        segs += [(base + GROUP_WIDTH, 2 * LANES, 1.0)]
    segs += [(cq, GROUP_WIDTH, qscale), (ck, 2 * GROUP_WIDTH, 1.0)]
    segs += [(dx, GROUP_WIDTH, 1.0), (db, 2 * GROUP_WIDTH, 1.0), (co, GROUP_WIDTH, 1.0), (dz, GROUP_WIDTH, 1.0)]
    parts = [w_in[:, s:s + n] if scale == 1.0 else w_in[:, s:s + n] * scale for s, n, scale in segs]
    zeros4 = jnp.zeros((w_in.shape[0], 8 - N_HEADS), w_in.dtype)
    for start in _gate_starts():
        parts += [w_in[:, start:start + N_HEADS], zeros4]
    parts.append(jnp.zeros((w_in.shape[0], LANES - 8 * len(_gate_starts())), w_in.dtype))
    w = jnp.concatenate(parts, axis=1)
    assert w.shape[1] == _N_PROJ
    return w.astype(MXU_DTYPE)


def _out_weight(w_out):
    parts = []
    for base in (0, GROUP_WIDTH):
        parts += [w_out[base + h * HEAD_DIM:base + (h + 1) * HEAD_DIM] for h in _Q_HEAD_ORDER]
    parts.append(w_out[2 * GROUP_WIDTH:])
    return jnp.concatenate(parts, axis=0).astype(MXU_DTYPE)


def _rope_tables(n_tok, n_ctx):
    t = jnp.arange(n_tok)
    row = (t // GRID_W).astype(F32)
    col = (t % GRID_W).astype(F32)
    half = HEAD_DIM // 2
    inv_freq = ROPE_BASE ** (-jnp.arange(0, half, 2, dtype=F32) / half)
    lane = np.arange(LANES)
    hd = lane % HEAD_DIM
    use_col = (hd // half) == 1
    pos = jnp.where(use_col[None, :], col[:, None], row[:, None])
    ang = pos * inv_freq[hd % (half // 2)][None, :]
    first = (hd % half) < (half // 2)
    cos = jnp.cos(ang)
    sin = jnp.where(first[None, :], -jnp.sin(ang), jnp.sin(ang))
    cos = jnp.concatenate([jnp.ones((n_ctx, LANES), F32), cos], axis=0)
    sin = jnp.concatenate([jnp.zeros((n_ctx, LANES), F32), sin], axis=0)
    return cos, sin


def kernel(x, c, ctx, c_ctx, w_ada, b_ada, g_norm1, g_norm2, w_in, sink_a, g_q_b, g_k_b, b_igate, b_fgate,
           g_mlstm, conv_w, conv_b, a_log, dt_bias, d_skip, g_ssm, w_out, w_ff1, w_ff2, g_final):
    bsz, n_tok, d = x.shape
    n_ctx = ctx.shape[1]
    depth = w_in.shape[0]
    assert d == D_MODEL and n_ctx % ROW_TILE == 0 and n_tok % ROW_TILE == 0

    cos, sin = _rope_tables(n_tok, n_ctx)
    n_rows = 16
    cvec = jnp.concatenate([c, c_ctx[None, :], jnp.zeros((n_rows - bsz - 1, d), F32)], axis=0)
    mod_all = _ada_mod(cvec, w_ada, b_ada)

    u = n_ctx + n_tok
    x_ctx, x_lat, lat_off = ctx, x, 0
    for layer in range(depth):
        need_ctx = layer < depth - 1
        mod = mod_all[layer].reshape(n_rows, 6, d)
        pad = jnp.zeros((bsz, 2, d), F32)
        mod_lat = jnp.concatenate([mod[:bsz], pad], axis=1)
        mod_ctx = jnp.broadcast_to(jnp.concatenate([mod[bsz], pad[0]], axis=0), (bsz, 8, d))
        modv = jnp.stack([mod_ctx, mod_lat], axis=1)

        gq = jnp.tile(g_q_b[layer] * (HEAD_DIM ** -0.5 * LOG2E), 2).reshape(1, LANES)
        gk = jnp.tile(g_k_b[layer], 2).reshape(1, LANES)
        pa, pb, pc, pf = _in_proj(x_ctx, x_lat, lat_off, u, modv, g_norm1[layer].reshape(1, d),
                                  _proj_weight(w_in[layer]), cos, sin, gq, gk, n_ctx)

        ya = _attn_a(pa, sink_a[layer], n_ctx, need_ctx)
        yb = _attn_b(pb, n_ctx, need_ctx)

        gbias = _gate_lanes([b_igate[layer, 0], b_fgate[layer, 0], b_igate[layer, 1], b_fgate[layer, 1],
                             dt_bias[layer, 0], dt_bias[layer, 1]])
        alog = jnp.broadcast_to(jnp.pad(a_log[layer], ((0, 0), (0, 8 - N_HEADS)))[:, :, None], (2, 8, LANES))
        dsk = jnp.repeat(d_skip[layer], HEAD_DIM).reshape(1, GROUP_WIDTH)
        hf, hb, yf, ybw = _scans(pc, pf, gbias, alog, conv_w[layer], conv_b[layer].reshape(1, -1), dsk, n_ctx)

        xc = _out_ffn(x_ctx, x_lat, lat_off, ya, yb, hf, hb, yf, ybw, pf, g_mlstm[layer].reshape(1, GROUP_WIDTH),
                      g_ssm[layer].reshape(1, GROUP_WIDTH), modv, g_norm2[layer].reshape(1, d),
                      _out_weight(w_out[layer]), w_ff1[layer].astype(MXU_DTYPE), w_ff2[layer].astype(MXU_DTYPE),
                      g_final.reshape(1, d), n_ctx, final=not need_ctx)
        x_ctx, x_lat, lat_off = xc, xc, n_ctx // ROW_TILE
    return xc
```

```python
import functools
import math

import numpy as np
import jax
import jax.numpy as jnp
from jax import lax
from jax.experimental import pallas as pl
from jax.experimental.pallas import tpu as pltpu

F32 = jnp.float32
MXU_DTYPE = jnp.bfloat16

D_MODEL = 1024
HEAD_DIM = 64
LANES = 128
GRID_W = 64
WINDOW = 128
ROPE_BASE = 10000.0
EPS = 1e-6
N_HEADS = 4
GROUP_WIDTH = N_HEADS * HEAD_DIM
D_STATE = 128
CHUNK = 128
ROW_TILE = 256
CHUNKS_PER_TILE = ROW_TILE // CHUNK
KEY_TILE = 1024
D_FF = 4 * D_MODEL
FF_CHUNK = 1024
HALO = 8
VMEM_LIMIT = 56 * 1024 * 1024
LOG2E = math.log2(math.e)
EXP_ARG_MAX = 88.0

_SPLIT_SIZES = (256, 128, 128, 256, 128, 128, 256, 256, 256, 256, 16, 256, 256, 256, 256, 8)
_OFF = [int(o) for o in np.concatenate([[0], np.cumsum(_SPLIT_SIZES)])]
_Q_HEAD_ORDER = (0, 2, 1, 3)

_COL_A = 0
_COL_B = 512
_COL_C = 1024
_COL_F = 1792
_N_F = 768 + 256 + 256 + 128
_N_PROJ = _COL_F + _N_F
_GATE_BLK = (_N_F - LANES) // LANES
_O_BLK = 768 // GROUP_WIDTH
_Z_BLK = (768 + GROUP_WIDTH) // GROUP_WIDTH


def _dot(a, b):
    return jnp.dot(a, b, preferred_element_type=F32)


def _dot_nt(a, b):
    return lax.dot_general(a, b, (((1,), (1,)), ((), ())), preferred_element_type=F32)


def _dot_tn(a, b):
    return lax.dot_general(a, b, (((0,), (0,)), ((), ())), preferred_element_type=F32)


def _split3(x):
    hi = x.astype(MXU_DTYPE)
    r1 = x - hi.astype(F32)
    mid = r1.astype(MXU_DTYPE)
    lo = (r1 - mid.astype(F32)).astype(MXU_DTYPE)
    return [hi, mid, lo]


def _rows_cumsum(rows, mt3):
    return _dot(jnp.concatenate(_split3(rows), axis=1), mt3)


def _rows_to_columns(rows, sel3):
    return _dot_tn(jnp.concatenate(_split3(rows), axis=0), sel3)


def _cummax_lanes(x, direction):
    n = x.shape[-1]
    lane = lax.broadcasted_iota(jnp.int32, x.shape, 1)

    def shifted(v, shift):
        if direction == 0:
            return jnp.where(lane >= shift, pltpu.roll(v, shift, 1), -jnp.inf)
        return jnp.where(lane < n - shift, pltpu.roll(v, n - shift, 1), -jnp.inf)

    def tree_max(vals):
        while len(vals) > 1:
            vals = [jnp.maximum(a, b) for a, b in zip(vals[::2], vals[1::2])] + ([vals[-1]] if len(vals) % 2 else [])
        return vals[0]

    window = tree_max([x] + [shifted(x, s) for s in range(1, 8)])
    return tree_max([window] + [shifted(window, s) for s in range(8, n, 8)])


def _lane_is_low(shape):
    lane = lax.broadcasted_iota(jnp.int32, shape, len(shape) - 1)
    return (lane % LANES) < HEAD_DIM


def _head_mean_square(t, low):
    sq = t * t
    lo = jnp.sum(jnp.where(low, sq, 0.0), axis=-1, keepdims=True)
    hi = jnp.sum(jnp.where(low, 0.0, sq), axis=-1, keepdims=True)
    return jnp.where(low, lo, hi) * (1.0 / HEAD_DIM)


def _params(*sem):
    return pltpu.CompilerParams(dimension_semantics=sem, vmem_limit_bytes=VMEM_LIMIT)


_RESIDENT = pl.Buffered(1)


def _ada_kernel(c_ref, w_ref, b_ref, out_ref):
    cv = c_ref[...]
    cv = cv * jax.nn.sigmoid(cv)
    out_ref[...] = _dot(cv.astype(MXU_DTYPE), w_ref[...].astype(MXU_DTYPE)) + b_ref[...]


def _ada_mod(cvec, w_ada, b_ada):
    depth, d, n = w_ada.shape
    rows = cvec.shape[0]
    tn = 1536
    return pl.pallas_call(
        _ada_kernel,
        grid=(depth, n // tn),
        in_specs=[
            pl.BlockSpec((rows, d), lambda l, j: (0, 0)),
            pl.BlockSpec((None, d, tn), lambda l, j: (l, 0, j)),
            pl.BlockSpec((None, 1, tn), lambda l, j: (l, 0, j)),
        ],
        out_specs=pl.BlockSpec((None, rows, tn), lambda l, j: (l, 0, j)),
        out_shape=jax.ShapeDtypeStruct((depth, rows, n), F32),
        compiler_params=_params("parallel", "parallel"),
        name="ada_mod",
    )(cvec, w_ada, b_ada.reshape(depth, 1, n))


def _rmsnorm_mod(x, g, shift, scale):
    ms = jnp.mean(x * x, axis=-1, keepdims=True)
    return (x * lax.rsqrt(ms + EPS) * g) * (1.0 + scale) + shift


def _in_proj_kernel(xctx_ref, xlat_ref, mod_ref, g1_ref, w_ref, cos_ref, sin_ref, gq_ref, gk_ref,
                    a_ref, b_ref, c_ref, f_ref, *, nt_ctx):
    x = jnp.where(pl.program_id(1) < nt_ctx, xctx_ref[...], xlat_ref[...])
    hn = _rmsnorm_mod(x, g1_ref[...], mod_ref[0:1, :], mod_ref[1:2, :])
    hb = hn.astype(MXU_DTYPE)
    cos = cos_ref[...]
    sin = sin_ref[...]
    lane = lax.broadcasted_iota(jnp.int32, (1, LANES), 1)
    first = (lane % 32) < 16
    low = _lane_is_low((1, LANES))

    def rope(t):
        partner = jnp.where(first, pltpu.roll(t, LANES - 16, 1), pltpu.roll(t, 16, 1))
        return t * cos + partner * sin

    pa = _dot(hb, w_ref[:, _COL_A:_COL_A + 512])
    for j in range(3):
        a_ref[:, j * LANES:(j + 1) * LANES] = rope(pa[:, j * LANES:(j + 1) * LANES]).astype(a_ref.dtype)
    a_ref[:, 3 * LANES:] = pa[:, 3 * LANES:].astype(a_ref.dtype)

    pb = _dot(hb, w_ref[:, _COL_B:_COL_B + 512])
    for j in range(3):
        t = pb[:, j * LANES:(j + 1) * LANES]
        g = gq_ref[...] if j < 2 else gk_ref[...]
        t = t * lax.rsqrt(_head_mean_square(t, low) + EPS) * g
        b_ref[:, j * LANES:(j + 1) * LANES] = rope(t).astype(b_ref.dtype)
    b_ref[:, 3 * LANES:] = pb[:, 3 * LANES:].astype(b_ref.dtype)

    c_ref[...] = _dot(hb, w_ref[:, _COL_C:_COL_F]).astype(c_ref.dtype)
    f_ref[...] = _dot(hb, w_ref[:, _COL_F:])


def _residual_specs(nt_ctx, lat_off, q_off, d):
    ctx_spec = pl.BlockSpec((None, ROW_TILE, d), lambda b, i: (b, jnp.minimum(i + q_off, nt_ctx - 1), 0))
    lat_spec = pl.BlockSpec((None, ROW_TILE, d),
                            lambda b, i: (b, jnp.maximum(i + q_off - nt_ctx, 0) + lat_off, 0))
    return [ctx_spec, lat_spec]


def _mod_spec(nt_ctx, q_off, d):
    return pl.BlockSpec((None, None, 8, d), lambda b, i: (b, jnp.where(i + q_off < nt_ctx, 0, 1), 0, 0))


def _in_proj(x_ctx, x_lat, lat_off, u, modv, g1, w, cos, sin, gq, gk, n_ctx):
    bsz, _, d = x_lat.shape
    nt = u // ROW_TILE
    nt_ctx = n_ctx // ROW_TILE
    row = lambda b, i: (b, i, 0)
    const = lambda b, i: (0, 0)
    return pl.pallas_call(
        functools.partial(_in_proj_kernel, nt_ctx=nt_ctx),
        grid=(bsz, nt),
        in_specs=_residual_specs(nt_ctx, lat_off, 0, d) + [
            _mod_spec(nt_ctx, 0, d),
            pl.BlockSpec((1, d), const),
            pl.BlockSpec((d, _N_PROJ), const, pipeline_mode=_RESIDENT),
            pl.BlockSpec((ROW_TILE, LANES), lambda b, i: (i, 0)),
            pl.BlockSpec((ROW_TILE, LANES), lambda b, i: (i, 0)),
            pl.BlockSpec((1, LANES), const),
            pl.BlockSpec((1, LANES), const),
        ],
        out_specs=[
            pl.BlockSpec((None, ROW_TILE, 512), row),
            pl.BlockSpec((None, ROW_TILE, 512), row),
            pl.BlockSpec((None, ROW_TILE, 768), row),
            pl.BlockSpec((None, ROW_TILE, _N_F), row),
        ],
        out_shape=[
            jax.ShapeDtypeStruct((bsz, u, 512), MXU_DTYPE),
            jax.ShapeDtypeStruct((bsz, u, 512), MXU_DTYPE),
            jax.ShapeDtypeStruct((bsz, u, 768), MXU_DTYPE),
            jax.ShapeDtypeStruct((bsz, u, _N_F), F32),
        ],
        compiler_params=_params("parallel", "parallel"),
        name="in_proj",
    )(x_ctx, x_lat, modv, g1, w, cos, sin, gq, gk)


def _stack_heads(q_ref, qs_ref):
    low = _lane_is_low((1, LANES))
    for jb in range(2):
        q = q_ref[:, jb * LANES:(jb + 1) * LANES]
        for half in range(2):
            idx = 2 * jb + half
            keep = low if half == 0 else jnp.logical_not(low)
            qs_ref[idx * ROW_TILE:(idx + 1) * ROW_TILE, :] = jnp.where(keep, q, jnp.zeros_like(q))


def _unstack_heads(o, out_ref):
    low = _lane_is_low((1, LANES))
    for jb in range(2):
        lo = o[(2 * jb) * ROW_TILE:(2 * jb + 1) * ROW_TILE, :]
        hi = o[(2 * jb + 1) * ROW_TILE:(2 * jb + 2) * ROW_TILE, :]
        out_ref[:, jb * LANES:(jb + 1) * LANES] = jnp.where(low, lo, hi).astype(out_ref.dtype)


def _attn_a_kernel(sink_ref, q_ref, kp_ref, ko_ref, kn_ref, kc_ref, vp_ref, vo_ref, vn_ref, vc_ref,
                   out_ref, qs_ref, kbuf, vbuf, s_ref, o_ref, *, n_tok, n_ctx, q_off):
    i = pl.program_id(1) + q_off
    nb = 2 * WINDOW + ROW_TILE
    nk = nb + n_ctx
    kbuf[0:WINDOW] = kp_ref[...]
    kbuf[WINDOW:WINDOW + ROW_TILE] = ko_ref[...]
    kbuf[WINDOW + ROW_TILE:nb] = kn_ref[...]
    kbuf[nb:nk] = kc_ref[...]
    vbuf[0:WINDOW] = vp_ref[...]
    vbuf[WINDOW:WINDOW + ROW_TILE] = vo_ref[...]
    vbuf[WINDOW + ROW_TILE:nb] = vn_ref[...]
    vbuf[nb:nk] = vc_ref[...]
    _stack_heads(q_ref, qs_ref)

    r = lax.broadcasted_iota(jnp.int32, (ROW_TILE, nk), 0)
    c = lax.broadcasted_iota(jnp.int32, (ROW_TILE, nk), 1)
    kpos = (i - 1) * ROW_TILE - WINDOW + c
    dist = c - r
    band = jnp.where(dist >= 0, jnp.where(dist <= 2 * WINDOW, 1, 0), 0)
    band = jnp.where(kpos >= 0, jnp.where(kpos < n_tok, band, 0), 0)
    band = jnp.where(i >= 1, band, 0)
    bias = jnp.where(c >= nb, 0.0, jnp.where(band > 0, 0.0, -jnp.inf)).astype(F32)

    for idx in range(N_HEADS):
        rows = slice(idx * ROW_TILE, (idx + 1) * ROW_TILE)
        s_ref[rows, :] = _dot_nt(qs_ref[rows, :], kbuf[...]) + bias
    for idx in range(N_HEADS):
        rows = slice(idx * ROW_TILE, (idx + 1) * ROW_TILE)
        sink = sink_ref[_Q_HEAD_ORDER[idx]]
        m = jnp.maximum(jnp.max(s_ref[rows, :], axis=-1, keepdims=True), sink)
        p = jnp.exp(s_ref[rows, :] - m)
        l = jnp.sum(p, axis=-1, keepdims=True) + jnp.exp(sink - m)
        o_ref[rows, :] = _dot(p.astype(MXU_DTYPE), vbuf[...]) / l
    _unstack_heads(o_ref[...], out_ref)


def _attn_a(pa, sink, n_ctx, need_ctx):
    bsz, u, _ = pa.shape
    n_tok = u - n_ctx
    q_off = 0 if need_ctx else n_ctx // ROW_TILE
    nt = u // ROW_TILE - q_off
    last_blk = u // WINDOW - 1
    rpw = ROW_TILE // WINDOW
    nk = 2 * WINDOW + ROW_TILE + n_ctx

    def own(col):
        return lambda b, i: (b, i + q_off, col)

    def prev(col):
        return lambda b, i: (b, jnp.maximum((i + q_off) * rpw - 1, 0), col)

    def nxt(col):
        return lambda b, i: (b, jnp.minimum((i + q_off + 1) * rpw, last_blk), col)

    def ctx(col):
        return lambda b, i: (b, 0, col)

    kern = functools.partial(_attn_a_kernel, n_tok=n_tok, n_ctx=n_ctx, q_off=q_off)
    return pl.pallas_call(
        kern,
        grid=(bsz, nt),
        in_specs=[
            pl.BlockSpec(memory_space=pltpu.SMEM),
            pl.BlockSpec((None, ROW_TILE, 2 * LANES), own(0)),
            pl.BlockSpec((None, WINDOW, LANES), prev(2)),
            pl.BlockSpec((None, ROW_TILE, LANES), own(2)),
            pl.BlockSpec((None, WINDOW, LANES), nxt(2)),
            pl.BlockSpec((None, n_ctx, LANES), ctx(2)),
            pl.BlockSpec((None, WINDOW, LANES), prev(3)),
            pl.BlockSpec((None, ROW_TILE, LANES), own(3)),
            pl.BlockSpec((None, WINDOW, LANES), nxt(3)),
            pl.BlockSpec((None, n_ctx, LANES), ctx(3)),
        ],
        out_specs=pl.BlockSpec((None, ROW_TILE, 2 * LANES), lambda b, i: (b, i, 0)),
        out_shape=jax.ShapeDtypeStruct((bsz, nt * ROW_TILE, 2 * LANES), MXU_DTYPE),
        scratch_shapes=[pltpu.VMEM((N_HEADS * ROW_TILE, LANES), MXU_DTYPE),
                        pltpu.VMEM((nk, LANES), MXU_DTYPE), pltpu.VMEM((nk, LANES), MXU_DTYPE),
                        pltpu.VMEM((N_HEADS * ROW_TILE, nk), F32),
                        pltpu.VMEM((N_HEADS * ROW_TILE, LANES), F32)],
        compiler_params=_params("parallel", "parallel"),
        name="attn_window",
    )(sink, pa, pa, pa, pa, pa, pa, pa, pa, pa)


def _attn_b_kernel(q_ref, k_ref, v_ref, out_ref, qs_ref, vext_ref, s_ref, p_ref, m_ref, alpha_ref, acc_ref,
                   *, n_ctx, n_tok, q_off, tk):
    qi = pl.program_id(1) + q_off
    n_rows = N_HEADS * ROW_TILE

    @pl.when(pl.program_id(1) == 0)
    def _():
        vext_ref[:, 0:LANES] = v_ref[...]
        vext_ref[:, LANES:] = jnp.ones((vext_ref.shape[0], LANES), vext_ref.dtype)

    _stack_heads(q_ref, qs_ref)

    def scores(slot, start, size):
        s_ref[slot, :, 0:size] = _dot_nt(qs_ref[...], k_ref[start:start + size, :])

    def softmax(slot, size, first):
        m_new = jnp.max(s_ref[slot, :, 0:size], axis=-1, keepdims=True)
        if not first:
            m_old = m_ref[...]
            m_new = jnp.maximum(m_old, m_new)
            alpha_ref[slot] = jnp.exp2(m_old - m_new)
        m_ref[...] = m_new
        p_ref[slot, :, 0:size] = jnp.exp2(s_ref[slot, :, 0:size] - m_new).astype(p_ref.dtype)

    def accumulate(slot, start, size, first):
        pv = _dot(p_ref[slot, :, 0:size], vext_ref[start:start + size, :])
        if first:
            acc_ref[...] = pv
        else:
            acc_ref[...] = alpha_ref[slot] * acc_ref[...] + pv

    def attend(tiles):
        scores(0, *tiles[0])
        for t, (start, size) in enumerate(tiles):
            if t + 1 < len(tiles):
                scores((t + 1) % 2, *tiles[t + 1])
            softmax(t % 2, size, t == 0)
            accumulate(t % 2, start, size, t == 0)
        _unstack_heads(acc_ref[:, 0:LANES] / acc_ref[:, LANES:], out_ref)

    ctx_tiles = [(0, n_ctx)]
    all_tiles = ctx_tiles + [(n_ctx + t * tk, tk) for t in range(n_tok // tk)]
    if q_off == 0:
        pl.when(qi == 0)(functools.partial(attend, ctx_tiles))
        pl.when(qi > 0)(functools.partial(attend, all_tiles))
    else:
        attend(all_tiles)


def _attn_b(pb, n_ctx, need_ctx):
    bsz, u, _ = pb.shape
    n_tok = u - n_ctx
    q_off = 0 if need_ctx else n_ctx // ROW_TILE
    nt = u // ROW_TILE - q_off
    tk = min(KEY_TILE, n_tok)
    assert n_tok % tk == 0
    kern = functools.partial(_attn_b_kernel, n_ctx=n_ctx, n_tok=n_tok, q_off=q_off, tk=tk)
    return pl.pallas_call(
        kern,
        grid=(bsz, nt),
        in_specs=[
            pl.BlockSpec((None, ROW_TILE, 2 * LANES), lambda b, i: (b, i + q_off, 0)),
            pl.BlockSpec((None, u, LANES), lambda b, i: (b, 0, 2)),
            pl.BlockSpec((None, u, LANES), lambda b, i: (b, 0, 3)),
        ],
        out_specs=pl.BlockSpec((None, ROW_TILE, 2 * LANES), lambda b, i: (b, i, 0)),
        out_shape=jax.ShapeDtypeStruct((bsz, nt * ROW_TILE, 2 * LANES), MXU_DTYPE),
        scratch_shapes=[
            pltpu.VMEM((N_HEADS * ROW_TILE, LANES), MXU_DTYPE),
            pltpu.VMEM((u, 2 * LANES), MXU_DTYPE),
            pltpu.VMEM((2, N_HEADS * ROW_TILE, tk), F32),
            pltpu.VMEM((2, N_HEADS * ROW_TILE, tk), MXU_DTYPE),
            pltpu.VMEM((N_HEADS * ROW_TILE, 1), F32),
            pltpu.VMEM((2, N_HEADS * ROW_TILE, 1), F32),
            pltpu.VMEM((N_HEADS * ROW_TILE, 2 * LANES), F32),
        ],
        compiler_params=_params("parallel", "arbitrary"),
        name="attn_dense",
    )(pb, pb, pb)


def _scan_tile(j, direction, nt_ctx, nt):
    if direction == 0:
        return j
    return jnp.where(j < nt_ctx, nt_ctx - 1 - j, nt + nt_ctx - 1 - j)


def _causal_mask(direction):
    t = lax.broadcasted_iota(jnp.int32, (CHUNK, CHUNK), 0)
    s = lax.broadcasted_iota(jnp.int32, (CHUNK, CHUNK), 1)
    return (s <= t) if direction == 0 else (s >= t)


def _chunk_order(direction):
    order = range(CHUNKS_PER_TILE)
    return order if direction == 0 else reversed(order)


def _log_sigmoid(x):
    return jnp.minimum(x, 0.0) - jnp.log1p(jnp.exp(-jnp.abs(x)))


def _softplus(x):
    return jnp.maximum(x, 0.0) + jnp.log1p(jnp.exp(-jnp.abs(x)))


def _mlstm_gates(direction, g_rows, m_prev, mt3):
    end = CHUNK - 1 if direction == 0 else 0
    i_r = jnp.concatenate([g[16 * direction:16 * direction + 8] for g in g_rows], axis=0)
    f_r = jnp.concatenate([g[16 * direction + 8:16 * direction + 16] for g in g_rows], axis=0)
    b_r = _rows_cumsum(_log_sigmoid(f_r), mt3)
    r_r = i_r - b_r
    cmax = _cummax_lanes(r_r, direction)
    tot = jnp.broadcast_to(b_r[:, end:end + 1], b_r.shape)
    g2 = tot - b_r + i_r
    g2max = jnp.broadcast_to(jnp.max(g2, axis=-1, keepdims=True), g2.shape)
    per_chunk = []
    for idx in range(len(g_rows)):
        sl = slice(8 * idx, 8 * idx + 8)
        c_r = jnp.maximum(m_prev, cmax[sl])
        dp_r = jnp.exp(m_prev - c_r)
        em_r = jnp.exp(jnp.minimum(-(b_r[sl] + c_r), EXP_ARG_MAX))
        m_new = jnp.maximum(tot[sl] + m_prev, g2max[sl])
        wk_r = jnp.exp(g2[sl] - m_new)
        cd_r = jnp.exp(tot[sl] + m_prev - m_new)
        per_chunk.append((jnp.concatenate([c_r * LOG2E, dp_r, em_r, wk_r], axis=0), r_r[sl] * LOG2E, cd_r))
        m_prev = m_new
    return per_chunk, m_prev


def _mlstm_scores(qkvs):
    low1 = _lane_is_low((1, LANES))
    ones = jnp.ones((CHUNK, LANES), MXU_DTYPE)
    pre = []
    for qkv in qkvs:
        per_pair = []
        for pair in range(2):
            q = qkv[:, pair * LANES:(pair + 1) * LANES]
            k = qkv[:, GROUP_WIDTH + pair * LANES:GROUP_WIDTH + (pair + 1) * LANES]
            v = qkv[:, 2 * GROUP_WIDTH + pair * LANES:2 * GROUP_WIDTH + (pair + 1) * LANES]
            qk = [_dot_nt(jnp.where(low1 if half == 0 else jnp.logical_not(low1), q, jnp.zeros_like(q)), k)
                  for half in range(2)]
            per_pair.append((q, k, jnp.concatenate([v, ones], axis=1), qk))
        pre.append(per_pair)
    return pre


def _mlstm_intra(items, pre):
    low1 = _lane_is_low((1, LANES))
    r2 = lax.broadcasted_iota(jnp.int32, (LANES, 2 * LANES), 0)
    c2 = lax.broadcasted_iota(jnp.int32, (LANES, 2 * LANES), 1)
    state_mask = (r2 < HEAD_DIM) == ((c2 % LANES) < HEAD_DIM)
    mid = []
    for (direction, qkv, r_r, cd_r, cols), per_pair in zip(items, pre):
        mask = _causal_mask(direction)
        res = []
        for pair, (q, k, vext, qk) in enumerate(per_pair):
            svs, dens = [], []
            for half in range(2):
                h = 2 * pair + half
                w = jnp.exp2(jnp.where(mask, r_r[h:h + 1, :] - cols[:, h * LANES:(h + 1) * LANES], -jnp.inf))
                s = qk[half] * w
                svs.append(_dot(s.astype(MXU_DTYPE), vext[:, :LANES]))
                dens.append(jnp.sum(s, axis=-1, keepdims=True))
            wk = cols[:, (8 + pair) * LANES:(9 + pair) * LANES]
            upd = _dot_tn((k.astype(F32) * wk).astype(MXU_DTYPE), vext)
            res.append((jnp.where(low1, svs[0], svs[1]), jnp.where(low1, dens[0], dens[1]),
                        jnp.where(state_mask, upd, 0.0)))
        mid.append(res)
    return mid


def _mlstm_recurrence(items, pre, mid, state):
    low2 = _lane_is_low((1, 2 * LANES))
    outs = []
    for (direction, qkv, r_r, cd_r, cols), per_pair, res in zip(items, pre, mid):
        out = []
        for pair in range(2):
            q = per_pair[pair][0]
            sv, den_intra, upd = res[pair]
            prior = _dot(q, state[direction][pair].astype(MXU_DTYPE))
            dp = cols[:, (4 + pair) * LANES:(5 + pair) * LANES]
            em = cols[:, (6 + pair) * LANES:(7 + pair) * LANES]
            num = sv + dp * prior[:, :LANES]
            den = den_intra + dp * prior[:, LANES:]
            out.append(num / jnp.maximum(jnp.abs(den), em))
            cd0 = jnp.concatenate([cd_r[2 * pair:2 * pair + 1]] * 2, axis=1)
            cd1 = jnp.concatenate([cd_r[2 * pair + 1:2 * pair + 2]] * 2, axis=1)
            state[direction][pair] = state[direction][pair] * jnp.where(low2, cd0, cd1) + upd
        outs.append(out)
    return outs


def _gate_rows(g_refs, gbias_ref):
    gbias = gbias_ref[...]
    return [[(g_ref[ci * CHUNK:(ci + 1) * CHUNK, :] + gbias).T for ci in _chunk_order(direction)]
            for direction, g_ref in enumerate(g_refs)]


def _mlstm_body(g_rows, qkvf_ref, qkvb_ref, mt3_ref, sel3_ref, hf_ref, hb_ref, st_ref, m_ref):
    dirs = ((qkvf_ref, None, hf_ref), (qkvb_ref, None, hb_ref))
    work = [(direction, idx) for idx in range(CHUNKS_PER_TILE) for direction in range(2)]
    row_slices = []
    for direction, idx in work:
        ci = list(_chunk_order(direction))[idx]
        row_slices.append(slice(ci * CHUNK, (ci + 1) * CHUNK))
    gates = []
    for direction in range(2):
        per_chunk, m_new = _mlstm_gates(direction, g_rows[direction], m_ref[direction], mt3_ref[direction])
        m_ref[direction] = m_new
        gates.append(per_chunk)
    yield
    rows_all = jnp.concatenate([gates[direction][idx][0] for direction, idx in work], axis=1)
    cols_all = _rows_to_columns(rows_all, sel3_ref[...])
    qkvs = [dirs[direction][0][rows, :] for (direction, _), rows in zip(work, row_slices)]
    pre = _mlstm_scores(qkvs)
    yield
    state = [[st_ref[direction, pair] for pair in range(2)] for direction in range(2)]
    items = []
    for n, (direction, idx) in enumerate(work):
        _, r_r, cd_r = gates[direction][idx]
        items.append((direction, qkvs[n], r_r, cd_r, cols_all[n * CHUNK:(n + 1) * CHUNK, :]))
    mid = _mlstm_intra(items, pre)
    yield
    outs = _mlstm_recurrence(items, pre, mid, state)
    for (direction, *_), rows, out in zip(items, row_slices, outs):
        for pair in range(2):
            dirs[direction][2][rows, pair * LANES:(pair + 1) * LANES] = out[pair]
    for direction in range(2):
        for pair in range(2):
            st_ref[direction, pair] = state[direction][pair]


def _scan_masks():
    s = np.arange(CHUNK)[:, None]
    t = np.arange(CHUNK)[None, :]
    mats = [np.tile((s <= t).astype(np.float32), (3, 1)), np.tile((s >= t).astype(np.float32), (3, 1))]
    return jnp.asarray(np.stack(mats), MXU_DTYPE)


def _column_selector(n_head_groups, n_pair_groups):
    n_groups = n_head_groups + n_pair_groups
    n_rows = 8 * n_groups + (-8 * n_groups) % 16
    n_cols = (N_HEADS * n_head_groups + 2 * n_pair_groups) * LANES
    sel = np.zeros((n_rows, n_cols), np.float32)
    col = 0
    for g in range(n_head_groups):
        for h in range(N_HEADS):
            sel[8 * g + h, col:col + LANES] = 1.0
            col += LANES
    for g in range(n_head_groups, n_groups):
        for pair in range(2):
            sel[8 * g + 2 * pair, col:col + HEAD_DIM] = 1.0
            sel[8 * g + 2 * pair + 1, col + HEAD_DIM:col + LANES] = 1.0
            col += LANES
    return jnp.asarray(np.tile(sel, (3, 1)), MXU_DTYPE)


def _ssd_conv(x_ref, xp_ref, xn_ref, cw_ref, cb_ref, tile, nt_ctx, nt):
    xin = x_ref[...]
    has_prev = (tile != 0) & (tile != nt_ctx)
    has_next = (tile != nt_ctx - 1) & (tile != nt - 1)
    prow = jnp.where(has_prev, xp_ref[HALO - 1:HALO, :], 0.0)
    nrow = jnp.where(has_next, xn_ref[0:1, :], 0.0)
    ridx = lax.broadcasted_iota(jnp.int32, (ROW_TILE, 1), 0)
    up = jnp.where(ridx == 0, prow, pltpu.roll(xin, 1, 0))
    dn = jnp.where(ridx == ROW_TILE - 1, nrow, pltpu.roll(xin, ROW_TILE - 1, 0))
    u = cw_ref[0:1, :] * up + cw_ref[1:2, :] * xin + cw_ref[2:3, :] * dn + cb_ref[...]
    return u * jax.nn.sigmoid(u)


def _ssd_gates(direction, g_rows, neg_a, mt3):
    end = CHUNK - 1 if direction == 0 else 0
    n = len(g_rows)
    dt_r = _softplus(jnp.concatenate([g[32 + 8 * direction:40 + 8 * direction] for g in g_rows], axis=0))
    cum_r = _rows_cumsum(dt_r * jnp.concatenate([neg_a] * n, axis=0), mt3)
    tot = jnp.broadcast_to(cum_r[:, end:end + 1], cum_r.shape)
    e_r = jnp.exp(cum_r)
    wend_r = jnp.exp(tot - cum_r) * dt_r
    d_r = jnp.exp(tot)
    zero = jnp.zeros((8, CHUNK), F32)
    per_chunk = []
    for idx in range(n):
        sl = slice(8 * idx, 8 * idx + 8)
        per_chunk.append((jnp.concatenate([cum_r[sl] * LOG2E, e_r[sl], wend_r[sl], zero], axis=0),
                          cum_r[sl] * LOG2E, dt_r[sl], d_r[sl]))
    return per_chunk


def _ssd_scores(us):
    pre = []
    for u in us:
        per_grp = []
        for grp in range(2):
            bm = u[:, GROUP_WIDTH + grp * D_STATE:GROUP_WIDTH + (grp + 1) * D_STATE].astype(MXU_DTYPE)
            cm = u[:, 2 * GROUP_WIDTH + grp * D_STATE:2 * GROUP_WIDTH + (grp + 1) * D_STATE].astype(MXU_DTYPE)
            x_pair = u[:, grp * LANES:(grp + 1) * LANES]
            per_grp.append((bm, cm, x_pair, _dot_nt(cm, bm)))
        pre.append(per_grp)
    return pre


def _ssd_intra(items, pre):
    low1 = _lane_is_low((1, LANES))
    mid = []
    for (direction, u, cum_r, dt_r, d_r, cols), per_grp in zip(items, pre):
        mask = _causal_mask(direction)
        res = []
        for grp, (bm, cm, x_pair, gmat) in enumerate(per_grp):
            xb = x_pair.astype(MXU_DTYPE)
            ys = []
            for half in range(2):
                h = 2 * grp + half
                decay = jnp.exp2(jnp.where(mask, cols[:, h * LANES:(h + 1) * LANES] - cum_r[h:h + 1, :], -jnp.inf))
                s = gmat * decay * dt_r[h:h + 1, :]
                ys.append(_dot(s.astype(MXU_DTYPE), xb))
            w_pair = cols[:, (6 + grp) * LANES:(7 + grp) * LANES]
            upd = _dot_tn(bm, (x_pair * w_pair).astype(MXU_DTYPE))
            res.append((jnp.where(low1, ys[0], ys[1]), upd))
        mid.append(res)
    return mid


def _ssd_recurrence(items, pre, mid, state):
    low1 = _lane_is_low((1, LANES))
    outs = []
    for (direction, u, cum_r, dt_r, d_r, cols), per_grp, res in zip(items, pre, mid):
        out = []
        for grp in range(2):
            cm = per_grp[grp][1]
            y_intra, upd = res[grp]
            ch = _dot(cm, state[direction][grp].astype(MXU_DTYPE))
            e_pair = cols[:, (4 + grp) * LANES:(5 + grp) * LANES]
            out.append(y_intra + e_pair * ch)
            d_pair = jnp.where(low1, d_r[2 * grp:2 * grp + 1], d_r[2 * grp + 1:2 * grp + 2])
            state[direction][grp] = state[direction][grp] * d_pair + upd
        outs.append(out)
    return outs


def _ssd_convs(xf_ref, xfp_ref, xfn_ref, xb_ref, xbp_ref, xbn_ref, cw_ref, cb_ref, *, nt_ctx, nt):
    j = pl.program_id(1)
    convs = []
    for direction, (x_ref, xp_ref, xn_ref) in enumerate(((xf_ref, xfp_ref, xfn_ref), (xb_ref, xbp_ref, xbn_ref))):
        tile = _scan_tile(j, direction, nt_ctx, nt)
        convs.append(_ssd_conv(x_ref, xp_ref, xn_ref, cw_ref, cb_ref, tile, nt_ctx, nt))
    return convs


def _ssd_body(g_rows, convs, alog_ref, dskip_ref, mt3_ref, sel3_ref, yf_ref, yb_ref, st_ref):
    dirs = ((None, None, None, None, yf_ref), (None, None, None, None, yb_ref))
    gates = []
    for direction in range(2):
        gates.append(_ssd_gates(direction, g_rows[direction], -jnp.exp(alog_ref[direction]), mt3_ref[direction]))
    work = [(direction, idx) for idx in range(CHUNKS_PER_TILE) for direction in range(2)]
    yield
    rows_all = jnp.concatenate([gates[direction][idx][0] for direction, idx in work], axis=1)
    cols_all = _rows_to_columns(rows_all, sel3_ref[...])
    items, row_slices = [], []
    for n, (direction, idx) in enumerate(work):
        ci = list(_chunk_order(direction))[idx]
        rows = slice(ci * CHUNK, (ci + 1) * CHUNK)
        _, cum_r, dt_r, d_r = gates[direction][idx]
        items.append((direction, convs[direction][rows, :], cum_r, dt_r, d_r,
                      cols_all[n * CHUNK:(n + 1) * CHUNK, :]))
        row_slices.append(rows)
    pre = _ssd_scores([item[1] for item in items])
    yield
    mid = _ssd_intra(items, pre)
    yield
    state = [[st_ref[direction, grp] for grp in range(2)] for direction in range(2)]
    outs = _ssd_recurrence(items, pre, mid, state)
    for (direction, u, *_), rows, out in zip(items, row_slices, outs):
        for grp in range(2):
            sl = slice(grp * LANES, (grp + 1) * LANES)
            y = out[grp]
            if direction == 0:
                y = y + dskip_ref[:, sl] * u[:, sl]
            dirs[direction][4][rows, sl] = y
    for direction in range(2):
        for grp in range(2):
            st_ref[direction, grp] = state[direction][grp]


def _scans_kernel(qkvf_ref, qkvb_ref, gf_ref, gb_ref, xf_ref, xfp_ref, xfn_ref, xb_ref, xbp_ref, xbn_ref,
                  gbias_ref, alog_ref, cw_ref, cb_ref, dskip_ref, mt3_ref, selm_ref, sels_ref,
                  hf_ref, hb_ref, yf_ref, yb_ref, stm_ref, m_ref, sts_ref, *, nt_ctx, nt):
    @pl.when(pl.program_id(1) == 0)
    def _():
        stm_ref[...] = jnp.zeros(stm_ref.shape, F32)
        m_ref[...] = jnp.zeros(m_ref.shape, F32)
        sts_ref[...] = jnp.zeros(sts_ref.shape, F32)

    convs = _ssd_convs(xf_ref, xfp_ref, xfn_ref, xb_ref, xbp_ref, xbn_ref, cw_ref, cb_ref, nt_ctx=nt_ctx, nt=nt)
    g_rows = _gate_rows((gf_ref, gb_ref), gbias_ref)
    bodies = [
        _mlstm_body(g_rows, qkvf_ref, qkvb_ref, mt3_ref, selm_ref, hf_ref, hb_ref, stm_ref, m_ref),
        _ssd_body(g_rows, convs, alog_ref, dskip_ref, mt3_ref, sels_ref, yf_ref, yb_ref, sts_ref),
    ]
    while bodies:
        bodies = [body for body in bodies if next(body, "done") != "done"]


def _scans(pc, pf, gbias, alog, conv_w, conv_b, d_skip, n_ctx):
    bsz, u, _ = pf.shape
    nt = u // ROW_TILE
    nt_ctx = n_ctx // ROW_TILE
    hpt = ROW_TILE // HALO
    last_halo = u // HALO - 1
    const = lambda b, j: (0, 0)
    const3 = lambda b, j: (0, 0, 0)
    tile = [lambda b, j, d=d: _scan_tile(j, d, nt_ctx, nt) for d in range(2)]

    def tile_spec(direction, width, col):
        return pl.BlockSpec((None, ROW_TILE, width), lambda b, j: (b, tile[direction](b, j), col))

    def conv_specs(direction):
        t = tile[direction]
        return [
            tile_spec(direction, 768, 0),
            pl.BlockSpec((None, HALO, 768), lambda b, j: (b, jnp.maximum(t(b, j) * hpt - 1, 0), 0)),
            pl.BlockSpec((None, HALO, 768), lambda b, j: (b, jnp.minimum((t(b, j) + 1) * hpt, last_halo), 0)),
        ]

    out = jax.ShapeDtypeStruct((bsz, u, GROUP_WIDTH), F32)
    mt3 = _scan_masks()
    sel_m = _column_selector(1, 3)
    sel_s = _column_selector(1, 2)
    return pl.pallas_call(
        functools.partial(_scans_kernel, nt_ctx=nt_ctx, nt=nt),
        grid=(bsz, nt),
        in_specs=[tile_spec(0, 768, 0), tile_spec(1, 768, 0),
                  tile_spec(0, LANES, _GATE_BLK), tile_spec(1, LANES, _GATE_BLK)]
        + conv_specs(0) + conv_specs(1) + [
            pl.BlockSpec((1, LANES), const),
            pl.BlockSpec((2, 8, LANES), const3),
            pl.BlockSpec((3, 768), const),
            pl.BlockSpec((1, 768), const),
            pl.BlockSpec((1, GROUP_WIDTH), const),
            pl.BlockSpec(mt3.shape, const3),
            pl.BlockSpec(sel_m.shape, const),
            pl.BlockSpec(sel_s.shape, const),
        ],
        out_specs=[tile_spec(0, GROUP_WIDTH, 0), tile_spec(1, GROUP_WIDTH, 0),
                   tile_spec(0, GROUP_WIDTH, 0), tile_spec(1, GROUP_WIDTH, 0)],
        out_shape=[out, out, out, out],
        scratch_shapes=[pltpu.VMEM((2, 2, LANES, 2 * LANES), F32), pltpu.VMEM((2, 8, LANES), F32),
                        pltpu.VMEM((2, 2, D_STATE, LANES), F32)],
        compiler_params=_params("parallel", "arbitrary"),
        name="scans",
    )(pc, pc, pf, pf, pf, pf, pf, pf, pf, pf, gbias, alog, conv_w, conv_b, d_skip, mt3, sel_m, sel_s)


def _out_ffn_kernel(xctx_ref, xlat_ref, ya_ref, yb_ref, hf_ref, hb_ref, o_ref, yf_ref, ybw_ref, z_ref,
                    gm_ref, gs_ref, mod_ref, g2_ref, wo_ref, w1_ref, w2_ref, gfin_ref, out_ref,
                    *, final, nt_ctx, q_off):
    if q_off >= nt_ctx:
        x_in = xlat_ref[...]
    else:
        x_in = jnp.where(pl.program_id(1) + q_off < nt_ctx, xctx_ref[...], xlat_ref[...])
    low = _lane_is_low((1, LANES))
    ym = []
    for pair in range(2):
        sl = slice(pair * LANES, (pair + 1) * LANES)
        h = hf_ref[:, sl] + hb_ref[:, sl]
        hn = h * lax.rsqrt(_head_mean_square(h, low) + EPS) * gm_ref[:, sl]
        ym.append((hn * jax.nn.sigmoid(o_ref[:, sl])).astype(MXU_DTYPE))
    ys = []
    for grp in range(2):
        sl = slice(grp * LANES, (grp + 1) * LANES)
        z = z_ref[:, sl]
        ys.append((yf_ref[:, sl] + ybw_ref[:, sl]) * (z * jax.nn.sigmoid(z)))
    ms = (jnp.sum(ys[0] * ys[0], axis=-1, keepdims=True)
          + jnp.sum(ys[1] * ys[1], axis=-1, keepdims=True)) * (1.0 / GROUP_WIDTH)
    rs = lax.rsqrt(ms + EPS)
    yd = [(ys[grp] * rs * gs_ref[:, grp * LANES:(grp + 1) * LANES]).astype(MXU_DTYPE) for grp in range(2)]

    y = (_dot(ya_ref[...], wo_ref[0:GROUP_WIDTH, :])
         + _dot(yb_ref[...], wo_ref[GROUP_WIDTH:2 * GROUP_WIDTH, :])
         + _dot(jnp.concatenate(ym, axis=1), wo_ref[2 * GROUP_WIDTH:3 * GROUP_WIDTH, :])
         + _dot(jnp.concatenate(yd, axis=1), wo_ref[3 * GROUP_WIDTH:, :]))
    x1 = x_in + mod_ref[2:3, :] * y
    h2 = _rmsnorm_mod(x1, g2_ref[...], mod_ref[3:4, :], mod_ref[4:5, :]).astype(MXU_DTYPE)
    acc = jnp.zeros(x1.shape, F32)
    for f in range(D_FF // FF_CHUNK):
        sl = slice(f * FF_CHUNK, (f + 1) * FF_CHUNK)
        hf = jnp.maximum(_dot(h2, w1_ref[:, sl]), 0.0)
        acc = acc + _dot((hf * hf).astype(MXU_DTYPE), w2_ref[sl, :])
    x2 = x1 + mod_ref[5:6, :] * acc
    if final:
        ms2 = jnp.mean(x2 * x2, axis=-1, keepdims=True)
        x2 = x2 * lax.rsqrt(ms2 + EPS) * gfin_ref[...]
    out_ref[...] = x2


def _out_ffn(x_ctx, x_lat, lat_off, ya, yb, hf, hb, yf, ybw, pf, gm, gs, modv, g2, wo, w1, w2, g_final,
             n_ctx, final):
    bsz, u, _ = pf.shape
    d = x_lat.shape[-1]
    nt_ctx = n_ctx // ROW_TILE
    q_off = nt_ctx if final else 0
    nt = u // ROW_TILE - q_off
    row = lambda b, i: (b, i + q_off, 0)
    const = lambda b, i: (0, 0)
    grp_tile = pl.BlockSpec((None, ROW_TILE, GROUP_WIDTH), row)
    attn_tile = pl.BlockSpec((None, ROW_TILE, GROUP_WIDTH), lambda b, i: (b, i, 0))
    return pl.pallas_call(
        functools.partial(_out_ffn_kernel, final=final, nt_ctx=nt_ctx, q_off=q_off),
        grid=(bsz, nt),
        in_specs=_residual_specs(nt_ctx, lat_off, q_off, d) + [
            attn_tile, attn_tile, grp_tile, grp_tile,
            pl.BlockSpec((None, ROW_TILE, GROUP_WIDTH), lambda b, i: (b, i + q_off, _O_BLK)),
            grp_tile, grp_tile,
            pl.BlockSpec((None, ROW_TILE, GROUP_WIDTH), lambda b, i: (b, i + q_off, _Z_BLK)),
            pl.BlockSpec((1, GROUP_WIDTH), const),
            pl.BlockSpec((1, GROUP_WIDTH), const),
            _mod_spec(nt_ctx, q_off, d),
            pl.BlockSpec((1, d), const),
            pl.BlockSpec((d, d), const, pipeline_mode=_RESIDENT),
            pl.BlockSpec((d, D_FF), const, pipeline_mode=_RESIDENT),
            pl.BlockSpec((D_FF, d), const, pipeline_mode=_RESIDENT),
            pl.BlockSpec((1, d), const),
        ],
        out_specs=pl.BlockSpec((None, ROW_TILE, d), lambda b, i: (b, i, 0)),
        out_shape=jax.ShapeDtypeStruct((bsz, nt * ROW_TILE, d), F32),
        compiler_params=_params("parallel", "parallel"),
        name="out_ffn_final" if final else "out_ffn",
    )(x_ctx, x_lat, ya, yb, hf, hb, pf, yf, ybw, pf, gm, gs, modv, g2, wo, w1, w2, g_final)


def _gate_starts():
    cg, ddt = _OFF[10], _OFF[15]
    return [cg, cg + N_HEADS, cg + 2 * N_HEADS, cg + 3 * N_HEADS, ddt, ddt + N_HEADS]


def _gate_lanes(groups):
    pad = jnp.zeros((8 - N_HEADS,), F32)
    parts = []
    for g in groups:
        parts += [g.astype(F32), pad]
    parts.append(jnp.zeros((LANES - 8 * len(groups),), F32))
    return jnp.concatenate(parts).reshape(1, LANES)


def _proj_weight(w_in):
    (aq, ak, av, bq, bk, bv, cq, ck, cv, co, cg, dx, dz, db, dc, ddt) = _OFF[:-1]
    qscale = HEAD_DIM ** -0.5
    segs = []
    for base, scale in ((aq, qscale), (bq, 1.0)):
        segs += [(base + h * HEAD_DIM, HEAD_DIM, scale) for h in _Q_HEAD_ORDER]
        segs += [(base + GROUP_WIDTH, 2 * LANES, 1.0)]
    segs += [(cq, GROUP_WIDTH, qscale), (ck, 2 * GROUP_WIDTH, 1.0)]
    segs += [(dx, GROUP_WIDTH, 1.0), (db, 2 * GROUP_WIDTH, 1.0), (co, GROUP_WIDTH, 1.0), (dz, GROUP_WIDTH, 1.0)]
    parts = [(w_in[:, s:s + n] if scale == 1.0 else w_in[:, s:s + n] * scale).astype(MXU_DTYPE)
             for s, n, scale in segs]
    zeros4 = jnp.zeros((w_in.shape[0], 8 - N_HEADS), MXU_DTYPE)
    for start in _gate_starts():
        parts += [w_in[:, start:start + N_HEADS].astype(MXU_DTYPE), zeros4]
    parts.append(jnp.zeros((w_in.shape[0], LANES - 8 * len(_gate_starts())), MXU_DTYPE))
    w = jnp.concatenate(parts, axis=1)
    assert w.shape[1] == _N_PROJ
    return w


def _out_weight(w_out):
    parts = []
    for base in (0, GROUP_WIDTH):
        parts += [w_out[base + h * HEAD_DIM:base + (h + 1) * HEAD_DIM].astype(MXU_DTYPE) for h in _Q_HEAD_ORDER]
    parts.append(w_out[2 * GROUP_WIDTH:].astype(MXU_DTYPE))
    return jnp.concatenate(parts, axis=0)


def _rope_tables(n_tok, n_ctx):
    rows = n_tok // GRID_W
    half = HEAD_DIM // 2
    n_freq = half // 2
    inv_freq = ROPE_BASE ** (-jnp.arange(0, half, 2, dtype=F32) / half)
    pos = jnp.arange(max(rows, GRID_W), dtype=F32)
    ang = pos[:, None] * inv_freq[None, :]
    cos_u = jnp.tile(jnp.cos(ang), (1, LANES // n_freq))
    sin_u = jnp.tile(jnp.sin(ang), (1, LANES // n_freq))
    lane = np.arange(LANES)
    hd = lane % HEAD_DIM
    use_col = ((hd // half) == 1)[None, :]
    first = ((hd % half) < n_freq)[None, :]

    def per_token(tab):
        by_row = jnp.repeat(tab[:rows], GRID_W, axis=0)
        by_col = jnp.tile(tab[:GRID_W], (rows, 1))
        return jnp.where(use_col, by_col, by_row)

    cos = per_token(cos_u)
    sin = per_token(sin_u)
    sin = jnp.where(first, -sin, sin)
    cos = jnp.concatenate([jnp.ones((n_ctx, LANES), F32), cos], axis=0)
    sin = jnp.concatenate([jnp.zeros((n_ctx, LANES), F32), sin], axis=0)
    return cos, sin


def kernel(x, c, ctx, c_ctx, w_ada, b_ada, g_norm1, g_norm2, w_in, sink_a, g_q_b, g_k_b, b_igate, b_fgate,
           g_mlstm, conv_w, conv_b, a_log, dt_bias, d_skip, g_ssm, w_out, w_ff1, w_ff2, g_final):
    bsz, n_tok, d = x.shape
    n_ctx = ctx.shape[1]
    depth = w_in.shape[0]
    assert d == D_MODEL and n_ctx % ROW_TILE == 0 and n_tok % ROW_TILE == 0

    cos, sin = _rope_tables(n_tok, n_ctx)
    n_rows = 16
    cvec = jnp.concatenate([c, c_ctx[None, :], jnp.zeros((n_rows - bsz - 1, d), F32)], axis=0)
    mod_all = _ada_mod(cvec, w_ada, b_ada)

    u = n_ctx + n_tok
    x_ctx, x_lat, lat_off = ctx, x, 0
    for layer in range(depth):
        need_ctx = layer < depth - 1
        mod = mod_all[layer].reshape(n_rows, 6, d)
        pad = jnp.zeros((bsz, 2, d), F32)
        mod_lat = jnp.concatenate([mod[:bsz], pad], axis=1)
        mod_ctx = jnp.broadcast_to(jnp.concatenate([mod[bsz], pad[0]], axis=0), (bsz, 8, d))
        modv = jnp.stack([mod_ctx, mod_lat], axis=1)

        gq = jnp.tile(g_q_b[layer] * (HEAD_DIM ** -0.5 * LOG2E), 2).reshape(1, LANES)
        gk = jnp.tile(g_k_b[layer], 2).reshape(1, LANES)
        pa, pb, pc, pf = _in_proj(x_ctx, x_lat, lat_off, u, modv, g_norm1[layer].reshape(1, d),
                                  _proj_weight(w_in[layer]), cos, sin, gq, gk, n_ctx)

        ya = _attn_a(pa, sink_a[layer], n_ctx, need_ctx)
        yb = _attn_b(pb, n_ctx, need_ctx)

        gbias = _gate_lanes([b_igate[layer, 0], b_fgate[layer, 0], b_igate[layer, 1], b_fgate[layer, 1],
                             dt_bias[layer, 0], dt_bias[layer, 1]])
        alog = jnp.broadcast_to(jnp.pad(a_log[layer], ((0, 0), (0, 8 - N_HEADS)))[:, :, None], (2, 8, LANES))
        dsk = jnp.repeat(d_skip[layer], HEAD_DIM).reshape(1, GROUP_WIDTH)
        hf, hb, yf, ybw = _scans(pc, pf, gbias, alog, conv_w[layer], conv_b[layer].reshape(1, -1), dsk, n_ctx)

        xc = _out_ffn(x_ctx, x_lat, lat_off, ya, yb, hf, hb, yf, ybw, pf, g_mlstm[layer].reshape(1, GROUP_WIDTH),
                      g_ssm[layer].reshape(1, GROUP_WIDTH), modv, g_norm2[layer].reshape(1, d),
                      _out_weight(w_out[layer]), w_ff1[layer].astype(MXU_DTYPE), w_ff2[layer].astype(MXU_DTYPE),
                      g_final.reshape(1, d), n_ctx, final=not need_ctx)
        x_ctx, x_lat, lat_off = xc, xc, n_ctx // ROW_TILE
    return xc
```

```python
import functools
import math

import numpy as np
import jax
import jax.numpy as jnp
from jax import lax
from jax.experimental import pallas as pl
from jax.experimental.pallas import tpu as pltpu

F32 = jnp.float32
MXU_DTYPE = jnp.bfloat16

D_MODEL = 1024
HEAD_DIM = 64
LANES = 128
GRID_W = 64
WINDOW = 128
ROPE_BASE = 10000.0
EPS = 1e-6
N_HEADS = 4
GROUP_WIDTH = N_HEADS * HEAD_DIM
D_STATE = 128
CHUNK = 128
ROW_TILE = 256
CHUNKS_PER_TILE = ROW_TILE // CHUNK
KEY_TILE = 1024
D_FF = 4 * D_MODEL
FF_CHUNK = 1024
HALO = 8
VMEM_LIMIT = 56 * 1024 * 1024
LOG2E = math.log2(math.e)
EXP_ARG_MAX = 88.0

_SPLIT_SIZES = (256, 128, 128, 256, 128, 128, 256, 256, 256, 256, 16, 256, 256, 256, 256, 8)
_OFF = [int(o) for o in np.concatenate([[0], np.cumsum(_SPLIT_SIZES)])]
_Q_HEAD_ORDER = (0, 2, 1, 3)

_COL_A = 0
_COL_B = 512
_COL_C = 1024
_COL_F = 1792
_N_F = 768 + 256 + 256 + 128
_N_PROJ = _COL_F + _N_F
_GATE_BLK = (_N_F - LANES) // LANES
_O_BLK = 768 // GROUP_WIDTH
_Z_BLK = (768 + GROUP_WIDTH) // GROUP_WIDTH


def _dot(a, b):
    return jnp.dot(a, b, preferred_element_type=F32)


def _dot_nt(a, b):
    return lax.dot_general(a, b, (((1,), (1,)), ((), ())), preferred_element_type=F32)


def _dot_tn(a, b):
    return lax.dot_general(a, b, (((0,), (0,)), ((), ())), preferred_element_type=F32)


def _split3(x):
    hi = x.astype(MXU_DTYPE)
    r1 = x - hi.astype(F32)
    mid = r1.astype(MXU_DTYPE)
    lo = (r1 - mid.astype(F32)).astype(MXU_DTYPE)
    return [hi, mid, lo]


def _rows_cumsum(rows, mt3):
    return _dot(jnp.concatenate(_split3(rows), axis=1), mt3)


def _rows_to_columns(rows, sel3):
    return _dot_tn(jnp.concatenate(_split3(rows), axis=0), sel3)


def _cummax_lanes(x, direction):
    n = x.shape[-1]
    lane = lax.broadcasted_iota(jnp.int32, x.shape, 1)

    def shifted(v, shift):
        if direction == 0:
            return jnp.where(lane >= shift, pltpu.roll(v, shift, 1), -jnp.inf)
        return jnp.where(lane < n - shift, pltpu.roll(v, n - shift, 1), -jnp.inf)

    def tree_max(vals):
        while len(vals) > 1:
            vals = [jnp.maximum(a, b) for a, b in zip(vals[::2], vals[1::2])] + ([vals[-1]] if len(vals) % 2 else [])
        return vals[0]

    window = tree_max([x] + [shifted(x, s) for s in range(1, 8)])
    return tree_max([window] + [shifted(window, s) for s in range(8, n, 8)])


def _lane_is_low(shape):
    lane = lax.broadcasted_iota(jnp.int32, shape, len(shape) - 1)
    return (lane % LANES) < HEAD_DIM


def _head_mean_square(t, low):
    sq = t * t
    lo = jnp.sum(jnp.where(low, sq, 0.0), axis=-1, keepdims=True)
    hi = jnp.sum(jnp.where(low, 0.0, sq), axis=-1, keepdims=True)
    return jnp.where(low, lo, hi) * (1.0 / HEAD_DIM)


def _params(*sem):
    return pltpu.CompilerParams(dimension_semantics=sem, vmem_limit_bytes=VMEM_LIMIT)


_RESIDENT = pl.Buffered(1)


def _ada_kernel(c_ref, w_ref, b_ref, out_ref):
    cv = c_ref[...]
    cv = cv * jax.nn.sigmoid(cv)
    out_ref[...] = _dot(cv.astype(MXU_DTYPE), w_ref[...].astype(MXU_DTYPE)) + b_ref[...]


def _ada_mod(cvec, w_ada, b_ada):
    depth, d, n = w_ada.shape
    rows = cvec.shape[0]
    tn = 1536
    return pl.pallas_call(
        _ada_kernel,
        grid=(depth, n // tn),
        in_specs=[
            pl.BlockSpec((rows, d), lambda l, j: (0, 0)),
            pl.BlockSpec((None, d, tn), lambda l, j: (l, 0, j)),
            pl.BlockSpec((None, 1, tn), lambda l, j: (l, 0, j)),
        ],
        out_specs=pl.BlockSpec((None, rows, tn), lambda l, j: (l, 0, j)),
        out_shape=jax.ShapeDtypeStruct((depth, rows, n), F32),
        compiler_params=_params("parallel", "parallel"),
        name="ada_mod",
    )(cvec, w_ada, b_ada.reshape(depth, 1, n))


def _rmsnorm_mod(x, g, shift, scale):
    ms = jnp.mean(x * x, axis=-1, keepdims=True)
    return (x * lax.rsqrt(ms + EPS) * g) * (1.0 + scale) + shift


def _in_proj_kernel(xctx_ref, xlat_ref, mod_ref, g1_ref, w_ref, cos_ref, sin_ref, gq_ref, gk_ref,
                    a_ref, b_ref, c_ref, f_ref, *, nt_ctx):
    x = jnp.where(pl.program_id(1) < nt_ctx, xctx_ref[...], xlat_ref[...])
    hn = _rmsnorm_mod(x, g1_ref[...], mod_ref[0:1, :], mod_ref[1:2, :])
    hb = hn.astype(MXU_DTYPE)
    cos = cos_ref[...]
    sin = sin_ref[...]
    lane = lax.broadcasted_iota(jnp.int32, (1, LANES), 1)
    first = (lane % 32) < 16
    low = _lane_is_low((1, LANES))

    def rope(t):
        partner = jnp.where(first, pltpu.roll(t, LANES - 16, 1), pltpu.roll(t, 16, 1))
        return t * cos + partner * sin

    pa = _dot(hb, w_ref[:, _COL_A:_COL_A + 512])
    for j in range(3):
        a_ref[:, j * LANES:(j + 1) * LANES] = rope(pa[:, j * LANES:(j + 1) * LANES]).astype(a_ref.dtype)
    a_ref[:, 3 * LANES:] = pa[:, 3 * LANES:].astype(a_ref.dtype)

    pb = _dot(hb, w_ref[:, _COL_B:_COL_B + 512])
    for j in range(3):
        t = pb[:, j * LANES:(j + 1) * LANES]
        g = gq_ref[...] if j < 2 else gk_ref[...]
        t = t * lax.rsqrt(_head_mean_square(t, low) + EPS) * g
        b_ref[:, j * LANES:(j + 1) * LANES] = rope(t).astype(b_ref.dtype)
    b_ref[:, 3 * LANES:] = pb[:, 3 * LANES:].astype(b_ref.dtype)

    c_ref[...] = _dot(hb, w_ref[:, _COL_C:_COL_F]).astype(c_ref.dtype)
    f_ref[...] = _dot(hb, w_ref[:, _COL_F:])


def _residual_specs(nt_ctx, lat_off, q_off, d):
    ctx_spec = pl.BlockSpec((None, ROW_TILE, d), lambda b, i: (b, jnp.minimum(i + q_off, nt_ctx - 1), 0))
    lat_spec = pl.BlockSpec((None, ROW_TILE, d),
                            lambda b, i: (b, jnp.maximum(i + q_off - nt_ctx, 0) + lat_off, 0))
    return [ctx_spec, lat_spec]


def _mod_spec(nt_ctx, q_off, d):
    return pl.BlockSpec((None, None, 8, d), lambda b, i: (b, jnp.where(i + q_off < nt_ctx, 0, 1), 0, 0))


def _in_proj(x_ctx, x_lat, lat_off, u, modv, g1, w, cos, sin, gq, gk, n_ctx):
    bsz, _, d = x_lat.shape
    nt = u // ROW_TILE
    nt_ctx = n_ctx // ROW_TILE
    row = lambda b, i: (b, i, 0)
    const = lambda b, i: (0, 0)
    return pl.pallas_call(
        functools.partial(_in_proj_kernel, nt_ctx=nt_ctx),
        grid=(bsz, nt),
        in_specs=_residual_specs(nt_ctx, lat_off, 0, d) + [
            _mod_spec(nt_ctx, 0, d),
            pl.BlockSpec((1, d), const),
            pl.BlockSpec((d, _N_PROJ), const, pipeline_mode=_RESIDENT),
            pl.BlockSpec((ROW_TILE, LANES), lambda b, i: (i, 0)),
            pl.BlockSpec((ROW_TILE, LANES), lambda b, i: (i, 0)),
            pl.BlockSpec((1, LANES), const),
            pl.BlockSpec((1, LANES), const),
        ],
        out_specs=[
            pl.BlockSpec((None, ROW_TILE, 512), row),
            pl.BlockSpec((None, ROW_TILE, 512), row),
            pl.BlockSpec((None, ROW_TILE, 768), row),
            pl.BlockSpec((None, ROW_TILE, _N_F), row),
        ],
        out_shape=[
            jax.ShapeDtypeStruct((bsz, u, 512), MXU_DTYPE),
            jax.ShapeDtypeStruct((bsz, u, 512), MXU_DTYPE),
            jax.ShapeDtypeStruct((bsz, u, 768), MXU_DTYPE),
            jax.ShapeDtypeStruct((bsz, u, _N_F), F32),
        ],
        compiler_params=_params("parallel", "parallel"),
        name="in_proj",
    )(x_ctx, x_lat, modv, g1, w, cos, sin, gq, gk)


def _stack_heads(q_ref, qs_ref):
    low = _lane_is_low((1, LANES))
    for jb in range(2):
        q = q_ref[:, jb * LANES:(jb + 1) * LANES]
        for half in range(2):
            idx = 2 * jb + half
            keep = low if half == 0 else jnp.logical_not(low)
            qs_ref[idx * ROW_TILE:(idx + 1) * ROW_TILE, :] = jnp.where(keep, q, jnp.zeros_like(q))


def _unstack_heads(o, out_ref):
    low = _lane_is_low((1, LANES))
    for jb in range(2):
        lo = o[(2 * jb) * ROW_TILE:(2 * jb + 1) * ROW_TILE, :]
        hi = o[(2 * jb + 1) * ROW_TILE:(2 * jb + 2) * ROW_TILE, :]
        out_ref[:, jb * LANES:(jb + 1) * LANES] = jnp.where(low, lo, hi).astype(out_ref.dtype)


def _attn_a_kernel(sink_ref, q_ref, kp_ref, ko_ref, kn_ref, kc_ref, vp_ref, vo_ref, vn_ref, vc_ref,
                   out_ref, qs_ref, kbuf, vbuf, s_ref, o_ref, *, n_tok, n_ctx, q_off):
    i = pl.program_id(1) + q_off
    nb = 2 * WINDOW + ROW_TILE
    nk = nb + n_ctx
    kbuf[0:WINDOW] = kp_ref[...]
    kbuf[WINDOW:WINDOW + ROW_TILE] = ko_ref[...]
    kbuf[WINDOW + ROW_TILE:nb] = kn_ref[...]
    kbuf[nb:nk] = kc_ref[...]
    vbuf[0:WINDOW] = vp_ref[...]
    vbuf[WINDOW:WINDOW + ROW_TILE] = vo_ref[...]
    vbuf[WINDOW + ROW_TILE:nb] = vn_ref[...]
    vbuf[nb:nk] = vc_ref[...]
    _stack_heads(q_ref, qs_ref)

    r = lax.broadcasted_iota(jnp.int32, (ROW_TILE, nk), 0)
    c = lax.broadcasted_iota(jnp.int32, (ROW_TILE, nk), 1)
    kpos = (i - 1) * ROW_TILE - WINDOW + c
    dist = c - r
    band = jnp.where(dist >= 0, jnp.where(dist <= 2 * WINDOW, 1, 0), 0)
    band = jnp.where(kpos >= 0, jnp.where(kpos < n_tok, band, 0), 0)
    band = jnp.where(i >= 1, band, 0)
    bias = jnp.where(c >= nb, 0.0, jnp.where(band > 0, 0.0, -jnp.inf)).astype(F32)

    for idx in range(N_HEADS):
        rows = slice(idx * ROW_TILE, (idx + 1) * ROW_TILE)
        s_ref[rows, :] = _dot_nt(qs_ref[rows, :], kbuf[...]) + bias
    for idx in range(N_HEADS):
        rows = slice(idx * ROW_TILE, (idx + 1) * ROW_TILE)
        sink = sink_ref[_Q_HEAD_ORDER[idx]]
        m = jnp.maximum(jnp.max(s_ref[rows, :], axis=-1, keepdims=True), sink)
        p = jnp.exp(s_ref[rows, :] - m)
        l = jnp.sum(p, axis=-1, keepdims=True) + jnp.exp(sink - m)
        o_ref[rows, :] = _dot(p.astype(MXU_DTYPE), vbuf[...]) / l
    _unstack_heads(o_ref[...], out_ref)


def _attn_a(pa, sink, n_ctx, need_ctx):
    bsz, u, _ = pa.shape
    n_tok = u - n_ctx
    q_off = 0 if need_ctx else n_ctx // ROW_TILE
    nt = u // ROW_TILE - q_off
    last_blk = u // WINDOW - 1
    rpw = ROW_TILE // WINDOW
    nk = 2 * WINDOW + ROW_TILE + n_ctx

    def own(col):
        return lambda b, i: (b, i + q_off, col)

    def prev(col):
        return lambda b, i: (b, jnp.maximum((i + q_off) * rpw - 1, 0), col)

    def nxt(col):
        return lambda b, i: (b, jnp.minimum((i + q_off + 1) * rpw, last_blk), col)

    def ctx(col):
        return lambda b, i: (b, 0, col)

    kern = functools.partial(_attn_a_kernel, n_tok=n_tok, n_ctx=n_ctx, q_off=q_off)
    return pl.pallas_call(
        kern,
        grid=(bsz, nt),
        in_specs=[
            pl.BlockSpec(memory_space=pltpu.SMEM),
            pl.BlockSpec((None, ROW_TILE, 2 * LANES), own(0)),
            pl.BlockSpec((None, WINDOW, LANES), prev(2)),
            pl.BlockSpec((None, ROW_TILE, LANES), own(2)),
            pl.BlockSpec((None, WINDOW, LANES), nxt(2)),
            pl.BlockSpec((None, n_ctx, LANES), ctx(2)),
            pl.BlockSpec((None, WINDOW, LANES), prev(3)),
            pl.BlockSpec((None, ROW_TILE, LANES), own(3)),
            pl.BlockSpec((None, WINDOW, LANES), nxt(3)),
            pl.BlockSpec((None, n_ctx, LANES), ctx(3)),
        ],
        out_specs=pl.BlockSpec((None, ROW_TILE, 2 * LANES), lambda b, i: (b, i, 0)),
        out_shape=jax.ShapeDtypeStruct((bsz, nt * ROW_TILE, 2 * LANES), MXU_DTYPE),
        scratch_shapes=[pltpu.VMEM((N_HEADS * ROW_TILE, LANES), MXU_DTYPE),
                        pltpu.VMEM((nk, LANES), MXU_DTYPE), pltpu.VMEM((nk, LANES), MXU_DTYPE),
                        pltpu.VMEM((N_HEADS * ROW_TILE, nk), F32),
                        pltpu.VMEM((N_HEADS * ROW_TILE, LANES), F32)],
        compiler_params=_params("parallel", "parallel"),
        name="attn_window",
    )(sink, pa, pa, pa, pa, pa, pa, pa, pa, pa)


def _attn_b_kernel(q_ref, k_ref, v_ref, out_ref, qs_ref, vext_ref, s_ref, p_ref, m_ref, alpha_ref, acc_ref,
                   *, n_ctx, n_tok, q_off, tk):
    qi = pl.program_id(1) + q_off
    n_rows = N_HEADS * ROW_TILE

    @pl.when(pl.program_id(1) == 0)
    def _():
        vext_ref[:, 0:LANES] = v_ref[...]
        vext_ref[:, LANES:] = jnp.ones((vext_ref.shape[0], LANES), vext_ref.dtype)

    _stack_heads(q_ref, qs_ref)

    def scores(slot, start, size):
        s_ref[slot, :, 0:size] = _dot_nt(qs_ref[...], k_ref[start:start + size, :])

    def softmax(slot, size, first):
        m_new = jnp.max(s_ref[slot, :, 0:size], axis=-1, keepdims=True)
        if not first:
            m_old = m_ref[...]
            m_new = jnp.maximum(m_old, m_new)
            alpha_ref[slot] = jnp.exp2(m_old - m_new)
        m_ref[...] = m_new
        p_ref[slot, :, 0:size] = jnp.exp2(s_ref[slot, :, 0:size] - m_new).astype(p_ref.dtype)

    def accumulate(slot, start, size, first):
        pv = _dot(p_ref[slot, :, 0:size], vext_ref[start:start + size, :])
        if first:
            acc_ref[...] = pv
        else:
            acc_ref[...] = alpha_ref[slot] * acc_ref[...] + pv

    def attend(tiles):
        scores(0, *tiles[0])
        for t, (start, size) in enumerate(tiles):
            if t + 1 < len(tiles):
                scores((t + 1) % 2, *tiles[t + 1])
            softmax(t % 2, size, t == 0)
            accumulate(t % 2, start, size, t == 0)
        _unstack_heads(acc_ref[:, 0:LANES] / acc_ref[:, LANES:], out_ref)

    ctx_tiles = [(0, n_ctx)]
    all_tiles = ctx_tiles + [(n_ctx + t * tk, tk) for t in range(n_tok // tk)]
    if q_off == 0:
        pl.when(qi == 0)(functools.partial(attend, ctx_tiles))
        pl.when(qi > 0)(functools.partial(attend, all_tiles))
    else:
        attend(all_tiles)


def _attn_b(pb, n_ctx, need_ctx):
    bsz, u, _ = pb.shape
    n_tok = u - n_ctx
    q_off = 0 if need_ctx else n_ctx // ROW_TILE
    nt = u // ROW_TILE - q_off
    tk = min(KEY_TILE, n_tok)
    assert n_tok % tk == 0
    kern = functools.partial(_attn_b_kernel, n_ctx=n_ctx, n_tok=n_tok, q_off=q_off, tk=tk)
    return pl.pallas_call(
        kern,
        grid=(bsz, nt),
        in_specs=[
            pl.BlockSpec((None, ROW_TILE, 2 * LANES), lambda b, i: (b, i + q_off, 0)),
            pl.BlockSpec((None, u, LANES), lambda b, i: (b, 0, 2)),
            pl.BlockSpec((None, u, LANES), lambda b, i: (b, 0, 3)),
        ],
        out_specs=pl.BlockSpec((None, ROW_TILE, 2 * LANES), lambda b, i: (b, i, 0)),
        out_shape=jax.ShapeDtypeStruct((bsz, nt * ROW_TILE, 2 * LANES), MXU_DTYPE),
        scratch_shapes=[
            pltpu.VMEM((N_HEADS * ROW_TILE, LANES), MXU_DTYPE),
            pltpu.VMEM((u, 2 * LANES), MXU_DTYPE),
            pltpu.VMEM((2, N_HEADS * ROW_TILE, tk), F32),
            pltpu.VMEM((2, N_HEADS * ROW_TILE, tk), MXU_DTYPE),
            pltpu.VMEM((N_HEADS * ROW_TILE, 1), F32),
            pltpu.VMEM((2, N_HEADS * ROW_TILE, 1), F32),
            pltpu.VMEM((N_HEADS * ROW_TILE, 2 * LANES), F32),
        ],
        compiler_params=_params("parallel", "arbitrary"),
        name="attn_dense",
    )(pb, pb, pb)


def _scan_tile(j, direction, nt_ctx, nt):
    if direction == 0:
        return j
    return jnp.where(j < nt_ctx, nt_ctx - 1 - j, nt + nt_ctx - 1 - j)


def _causal_mask(direction):
    t = lax.broadcasted_iota(jnp.int32, (CHUNK, CHUNK), 0)
    s = lax.broadcasted_iota(jnp.int32, (CHUNK, CHUNK), 1)
    return (s <= t) if direction == 0 else (s >= t)


def _chunk_order(direction):
    order = range(CHUNKS_PER_TILE)
    return order if direction == 0 else reversed(order)


def _log_sigmoid(x):
    return jnp.minimum(x, 0.0) - jnp.log1p(jnp.exp(-jnp.abs(x)))


def _softplus(x):
    return jnp.maximum(x, 0.0) + jnp.log1p(jnp.exp(-jnp.abs(x)))


def _mlstm_gates(direction, g_rows, m_prev, mt3):
    end = CHUNK - 1 if direction == 0 else 0
    i_r = jnp.concatenate([g[16 * direction:16 * direction + 8] for g in g_rows], axis=0)
    f_r = jnp.concatenate([g[16 * direction + 8:16 * direction + 16] for g in g_rows], axis=0)
    b_r = _rows_cumsum(_log_sigmoid(f_r), mt3)
    r_r = i_r - b_r
    cmax = _cummax_lanes(r_r, direction)
    tot = jnp.broadcast_to(b_r[:, end:end + 1], b_r.shape)
    g2 = tot - b_r + i_r
    g2max = jnp.broadcast_to(jnp.max(g2, axis=-1, keepdims=True), g2.shape)
    per_chunk = []
    for idx in range(len(g_rows)):
        sl = slice(8 * idx, 8 * idx + 8)
        c_r = jnp.maximum(m_prev, cmax[sl])
        dp_r = jnp.exp(m_prev - c_r)
        em_r = jnp.exp(jnp.minimum(-(b_r[sl] + c_r), EXP_ARG_MAX))
        m_new = jnp.maximum(tot[sl] + m_prev, g2max[sl])
        wk_r = jnp.exp(g2[sl] - m_new)
        cd_r = jnp.exp(tot[sl] + m_prev - m_new)
        per_chunk.append((jnp.concatenate([c_r * LOG2E, dp_r, em_r, wk_r], axis=0), r_r[sl] * LOG2E, cd_r))
        m_prev = m_new
    return per_chunk, m_prev


def _mlstm_scores(qkvs):
    low1 = _lane_is_low((1, LANES))
    ones = jnp.ones((CHUNK, LANES), MXU_DTYPE)
    pre = []
    for qkv in qkvs:
        per_pair = []
        for pair in range(2):
            q = qkv[:, pair * LANES:(pair + 1) * LANES]
            k = qkv[:, GROUP_WIDTH + pair * LANES:GROUP_WIDTH + (pair + 1) * LANES]
            v = qkv[:, 2 * GROUP_WIDTH + pair * LANES:2 * GROUP_WIDTH + (pair + 1) * LANES]
            qk = [_dot_nt(jnp.where(low1 if half == 0 else jnp.logical_not(low1), q, jnp.zeros_like(q)), k)
                  for half in range(2)]
            per_pair.append((q, k, jnp.concatenate([v, ones], axis=1), qk))
        pre.append(per_pair)
    return pre


def _mlstm_intra(items, pre):
    low1 = _lane_is_low((1, LANES))
    r2 = lax.broadcasted_iota(jnp.int32, (LANES, 2 * LANES), 0)
    c2 = lax.broadcasted_iota(jnp.int32, (LANES, 2 * LANES), 1)
    state_mask = (r2 < HEAD_DIM) == ((c2 % LANES) < HEAD_DIM)
    mid = []
    for (direction, qkv, r_r, cd_r, cols), per_pair in zip(items, pre):
        mask = _causal_mask(direction)
        res = []
        for pair, (q, k, vext, qk) in enumerate(per_pair):
            svs, dens = [], []
            for half in range(2):
                h = 2 * pair + half
                w = jnp.exp2(jnp.where(mask, r_r[h:h + 1, :] - cols[:, h * LANES:(h + 1) * LANES], -jnp.inf))
                s = qk[half] * w
                svs.append(_dot(s.astype(MXU_DTYPE), vext[:, :LANES]))
                dens.append(jnp.sum(s, axis=-1, keepdims=True))
            wk = cols[:, (8 + pair) * LANES:(9 + pair) * LANES]
            upd = _dot_tn((k.astype(F32) * wk).astype(MXU_DTYPE), vext)
            res.append((jnp.where(low1, svs[0], svs[1]), jnp.where(low1, dens[0], dens[1]),
                        jnp.where(state_mask, upd, 0.0)))
        mid.append(res)
    return mid


def _mlstm_recurrence(items, pre, mid, state):
    low2 = _lane_is_low((1, 2 * LANES))
    outs = []
    for (direction, qkv, r_r, cd_r, cols), per_pair, res in zip(items, pre, mid):
        out = []
        for pair in range(2):
            q = per_pair[pair][0]
            sv, den_intra, upd = res[pair]
            prior = _dot(q, state[direction][pair].astype(MXU_DTYPE))
            dp = cols[:, (4 + pair) * LANES:(5 + pair) * LANES]
            em = cols[:, (6 + pair) * LANES:(7 + pair) * LANES]
            num = sv + dp * prior[:, :LANES]
            den = den_intra + dp * prior[:, LANES:]
            out.append(num / jnp.maximum(jnp.abs(den), em))
            cd0 = jnp.concatenate([cd_r[2 * pair:2 * pair + 1]] * 2, axis=1)
            cd1 = jnp.concatenate([cd_r[2 * pair + 1:2 * pair + 2]] * 2, axis=1)
            state[direction][pair] = state[direction][pair] * jnp.where(low2, cd0, cd1) + upd
        outs.append(out)
    return outs


def _gate_rows(g_refs, gbias_ref):
    gbias = gbias_ref[...]
    return [[(g_ref[ci * CHUNK:(ci + 1) * CHUNK, :] + gbias).T for ci in _chunk_order(direction)]
            for direction, g_ref in enumerate(g_refs)]


def _mlstm_body(g_rows, qkvf_ref, qkvb_ref, mt3_ref, sel3_ref, hf_ref, hb_ref, st_ref, m_ref):
    dirs = ((qkvf_ref, None, hf_ref), (qkvb_ref, None, hb_ref))
    work = [(direction, idx) for idx in range(CHUNKS_PER_TILE) for direction in range(2)]
    row_slices = []
    for direction, idx in work:
        ci = list(_chunk_order(direction))[idx]
        row_slices.append(slice(ci * CHUNK, (ci + 1) * CHUNK))
    gates = []
    for direction in range(2):
        per_chunk, m_new = _mlstm_gates(direction, g_rows[direction], m_ref[direction], mt3_ref[direction])
        m_ref[direction] = m_new
        gates.append(per_chunk)
    yield
    rows_all = jnp.concatenate([gates[direction][idx][0] for direction, idx in work], axis=1)
    cols_all = _rows_to_columns(rows_all, sel3_ref[...])
    qkvs = [dirs[direction][0][rows, :] for (direction, _), rows in zip(work, row_slices)]
    pre = _mlstm_scores(qkvs)
    yield
    state = [[st_ref[direction, pair] for pair in range(2)] for direction in range(2)]
    items = []
    for n, (direction, idx) in enumerate(work):
        _, r_r, cd_r = gates[direction][idx]
        items.append((direction, qkvs[n], r_r, cd_r, cols_all[n * CHUNK:(n + 1) * CHUNK, :]))
    mid = _mlstm_intra(items, pre)
    yield
    outs = _mlstm_recurrence(items, pre, mid, state)
    for (direction, *_), rows, out in zip(items, row_slices, outs):
        for pair in range(2):
            dirs[direction][2][rows, pair * LANES:(pair + 1) * LANES] = out[pair]
    for direction in range(2):
        for pair in range(2):
            st_ref[direction, pair] = state[direction][pair]


def _scan_masks():
    s = np.arange(CHUNK)[:, None]
    t = np.arange(CHUNK)[None, :]
    mats = [np.tile((s <= t).astype(np.float32), (3, 1)), np.tile((s >= t).astype(np.float32), (3, 1))]
    return jnp.asarray(np.stack(mats), MXU_DTYPE)


def _column_selector(n_head_groups, n_pair_groups):
    n_groups = n_head_groups + n_pair_groups
    n_rows = 8 * n_groups + (-8 * n_groups) % 16
    n_cols = (N_HEADS * n_head_groups + 2 * n_pair_groups) * LANES
    sel = np.zeros((n_rows, n_cols), np.float32)
    col = 0
    for g in range(n_head_groups):
        for h in range(N_HEADS):
            sel[8 * g + h, col:col + LANES] = 1.0
            col += LANES
    for g in range(n_head_groups, n_groups):
        for pair in range(2):
            sel[8 * g + 2 * pair, col:col + HEAD_DIM] = 1.0
            sel[8 * g + 2 * pair + 1, col + HEAD_DIM:col + LANES] = 1.0
            col += LANES
    return jnp.asarray(np.tile(sel, (3, 1)), MXU_DTYPE)


def _ssd_conv(x_ref, xp_ref, xn_ref, cw_ref, cb_ref, tile, nt_ctx, nt):
    xin = x_ref[...]
    has_prev = (tile != 0) & (tile != nt_ctx)
    has_next = (tile != nt_ctx - 1) & (tile != nt - 1)
    prow = jnp.where(has_prev, xp_ref[HALO - 1:HALO, :], 0.0)
    nrow = jnp.where(has_next, xn_ref[0:1, :], 0.0)
    ridx = lax.broadcasted_iota(jnp.int32, (ROW_TILE, 1), 0)
    up = jnp.where(ridx == 0, prow, pltpu.roll(xin, 1, 0))
    dn = jnp.where(ridx == ROW_TILE - 1, nrow, pltpu.roll(xin, ROW_TILE - 1, 0))
    u = cw_ref[0:1, :] * up + cw_ref[1:2, :] * xin + cw_ref[2:3, :] * dn + cb_ref[...]
    return u * jax.nn.sigmoid(u)


def _ssd_gates(direction, g_rows, neg_a, mt3):
    end = CHUNK - 1 if direction == 0 else 0
    n = len(g_rows)
    dt_r = _softplus(jnp.concatenate([g[32 + 8 * direction:40 + 8 * direction] for g in g_rows], axis=0))
    cum_r = _rows_cumsum(dt_r * jnp.concatenate([neg_a] * n, axis=0), mt3)
    tot = jnp.broadcast_to(cum_r[:, end:end + 1], cum_r.shape)
    e_r = jnp.exp(cum_r)
    wend_r = jnp.exp(tot - cum_r) * dt_r
    d_r = jnp.exp(tot)
    zero = jnp.zeros((8, CHUNK), F32)
    per_chunk = []
    for idx in range(n):
        sl = slice(8 * idx, 8 * idx + 8)
        per_chunk.append((jnp.concatenate([cum_r[sl] * LOG2E, e_r[sl], wend_r[sl], zero], axis=0),
                          cum_r[sl] * LOG2E, dt_r[sl], d_r[sl]))
    return per_chunk


def _ssd_scores(us):
    pre = []
    for u in us:
        per_grp = []
        for grp in range(2):
            bm = u[:, GROUP_WIDTH + grp * D_STATE:GROUP_WIDTH + (grp + 1) * D_STATE].astype(MXU_DTYPE)
            cm = u[:, 2 * GROUP_WIDTH + grp * D_STATE:2 * GROUP_WIDTH + (grp + 1) * D_STATE].astype(MXU_DTYPE)
            x_pair = u[:, grp * LANES:(grp + 1) * LANES]
            per_grp.append((bm, cm, x_pair, _dot_nt(cm, bm)))
        pre.append(per_grp)
    return pre


def _ssd_intra(items, pre):
    low1 = _lane_is_low((1, LANES))
    mid = []
    for (direction, u, cum_r, dt_r, d_r, cols), per_grp in zip(items, pre):
        mask = _causal_mask(direction)
        res = []
        for grp, (bm, cm, x_pair, gmat) in enumerate(per_grp):
            xb = x_pair.astype(MXU_DTYPE)
            ys = []
            for half in range(2):
                h = 2 * grp + half
                decay = jnp.exp2(jnp.where(mask, cols[:, h * LANES:(h + 1) * LANES] - cum_r[h:h + 1, :], -jnp.inf))
                s = gmat * decay * dt_r[h:h + 1, :]
                ys.append(_dot(s.astype(MXU_DTYPE), xb))
            w_pair = cols[:, (6 + grp) * LANES:(7 + grp) * LANES]
            upd = _dot_tn(bm, (x_pair * w_pair).astype(MXU_DTYPE))
            res.append((jnp.where(low1, ys[0], ys[1]), upd))
        mid.append(res)
    return mid


def _ssd_recurrence(items, pre, mid, state):
    low1 = _lane_is_low((1, LANES))
    outs = []
    for (direction, u, cum_r, dt_r, d_r, cols), per_grp, res in zip(items, pre, mid):
        out = []
        for grp in range(2):
            cm = per_grp[grp][1]
            y_intra, upd = res[grp]
            ch = _dot(cm, state[direction][grp].astype(MXU_DTYPE))
            e_pair = cols[:, (4 + grp) * LANES:(5 + grp) * LANES]
            out.append(y_intra + e_pair * ch)
            d_pair = jnp.where(low1, d_r[2 * grp:2 * grp + 1], d_r[2 * grp + 1:2 * grp + 2])
            state[direction][grp] = state[direction][grp] * d_pair + upd
        outs.append(out)
    return outs


def _ssd_convs(xf_ref, xfp_ref, xfn_ref, xb_ref, xbp_ref, xbn_ref, cw_ref, cb_ref, u_ref, *, nt_ctx, nt):
    j = pl.program_id(1)
    tile_f = _scan_tile(j, 0, nt_ctx, nt)
    tile_b = _scan_tile(j, 1, nt_ctx, nt)
    step_b_visits_f = jnp.where(tile_f < nt_ctx, nt_ctx - 1 - tile_f, nt + nt_ctx - 1 - tile_f)
    rows_f = pl.ds(pl.multiple_of(tile_f * ROW_TILE, ROW_TILE), ROW_TILE)
    rows_b = pl.ds(pl.multiple_of(tile_b * ROW_TILE, ROW_TILE), ROW_TILE)

    @pl.when(step_b_visits_f >= j)
    def _():
        u_ref[rows_f, :] = _ssd_conv(xf_ref, xfp_ref, xfn_ref, cw_ref, cb_ref, tile_f, nt_ctx, nt)

    @pl.when(tile_b > j)
    def _():
        u_ref[rows_b, :] = _ssd_conv(xb_ref, xbp_ref, xbn_ref, cw_ref, cb_ref, tile_b, nt_ctx, nt)

    return [u_ref[rows_f, :], u_ref[rows_b, :]]


def _ssd_body(g_rows, convs, alog_ref, dskip_ref, mt3_ref, sel3_ref, yf_ref, yb_ref, st_ref):
    dirs = ((None, None, None, None, yf_ref), (None, None, None, None, yb_ref))
    gates = []
    for direction in range(2):
        gates.append(_ssd_gates(direction, g_rows[direction], -jnp.exp(alog_ref[direction]), mt3_ref[direction]))
    work = [(direction, idx) for idx in range(CHUNKS_PER_TILE) for direction in range(2)]
    yield
    rows_all = jnp.concatenate([gates[direction][idx][0] for direction, idx in work], axis=1)
    cols_all = _rows_to_columns(rows_all, sel3_ref[...])
    items, row_slices = [], []
    for n, (direction, idx) in enumerate(work):
        ci = list(_chunk_order(direction))[idx]
        rows = slice(ci * CHUNK, (ci + 1) * CHUNK)
        _, cum_r, dt_r, d_r = gates[direction][idx]
        items.append((direction, convs[direction][rows, :], cum_r, dt_r, d_r,
                      cols_all[n * CHUNK:(n + 1) * CHUNK, :]))
        row_slices.append(rows)
    pre = _ssd_scores([item[1] for item in items])
    yield
    mid = _ssd_intra(items, pre)
    yield
    state = [[st_ref[direction, grp] for grp in range(2)] for direction in range(2)]
    outs = _ssd_recurrence(items, pre, mid, state)
    for (direction, u, *_), rows, out in zip(items, row_slices, outs):
        for grp in range(2):
            sl = slice(grp * LANES, (grp + 1) * LANES)
            y = out[grp]
            if direction == 0:
                y = y + dskip_ref[:, sl] * u[:, sl]
            dirs[direction][4][rows, sl] = y
    for direction in range(2):
        for grp in range(2):
            st_ref[direction, grp] = state[direction][grp]


def _scans_kernel(qkvf_ref, qkvb_ref, gf_ref, gb_ref, xf_ref, xfp_ref, xfn_ref, xb_ref, xbp_ref, xbn_ref,
                  gbias_ref, alog_ref, cw_ref, cb_ref, dskip_ref, mt3_ref, selm_ref, sels_ref,
                  hf_ref, hb_ref, yf_ref, yb_ref, stm_ref, m_ref, sts_ref, u_ref, *, nt_ctx, nt):
    @pl.when(pl.program_id(1) == 0)
    def _():
        stm_ref[...] = jnp.zeros(stm_ref.shape, F32)
        m_ref[...] = jnp.zeros(m_ref.shape, F32)
        sts_ref[...] = jnp.zeros(sts_ref.shape, F32)

    convs = _ssd_convs(xf_ref, xfp_ref, xfn_ref, xb_ref, xbp_ref, xbn_ref, cw_ref, cb_ref, u_ref,
                       nt_ctx=nt_ctx, nt=nt)
    g_rows = _gate_rows((gf_ref, gb_ref), gbias_ref)
    bodies = [
        _mlstm_body(g_rows, qkvf_ref, qkvb_ref, mt3_ref, selm_ref, hf_ref, hb_ref, stm_ref, m_ref),
        _ssd_body(g_rows, convs, alog_ref, dskip_ref, mt3_ref, sels_ref, yf_ref, yb_ref, sts_ref),
    ]
    while bodies:
        bodies = [body for body in bodies if next(body, "done") != "done"]


def _scans(pc, pf, gbias, alog, conv_w, conv_b, d_skip, n_ctx):
    bsz, u, _ = pf.shape
    nt = u // ROW_TILE
    nt_ctx = n_ctx // ROW_TILE
    hpt = ROW_TILE // HALO
    last_halo = u // HALO - 1
    const = lambda b, j: (0, 0)
    const3 = lambda b, j: (0, 0, 0)
    tile = [lambda b, j, d=d: _scan_tile(j, d, nt_ctx, nt) for d in range(2)]

    def tile_spec(direction, width, col):
        return pl.BlockSpec((None, ROW_TILE, width), lambda b, j: (b, tile[direction](b, j), col))

    def conv_specs(direction):
        t = tile[direction]
        return [
            tile_spec(direction, 768, 0),
            pl.BlockSpec((None, HALO, 768), lambda b, j: (b, jnp.maximum(t(b, j) * hpt - 1, 0), 0)),
            pl.BlockSpec((None, HALO, 768), lambda b, j: (b, jnp.minimum((t(b, j) + 1) * hpt, last_halo), 0)),
        ]

    out = jax.ShapeDtypeStruct((bsz, u, GROUP_WIDTH), F32)
    mt3 = _scan_masks()
    sel_m = _column_selector(1, 3)
    sel_s = _column_selector(1, 2)
    return pl.pallas_call(
        functools.partial(_scans_kernel, nt_ctx=nt_ctx, nt=nt),
        grid=(bsz, nt),
        in_specs=[tile_spec(0, 768, 0), tile_spec(1, 768, 0),
                  tile_spec(0, LANES, _GATE_BLK), tile_spec(1, LANES, _GATE_BLK)]
        + conv_specs(0) + conv_specs(1) + [
            pl.BlockSpec((1, LANES), const),
            pl.BlockSpec((2, 8, LANES), const3),
            pl.BlockSpec((3, 768), const),
            pl.BlockSpec((1, 768), const),
            pl.BlockSpec((1, GROUP_WIDTH), const),
            pl.BlockSpec(mt3.shape, const3),
            pl.BlockSpec(sel_m.shape, const),
            pl.BlockSpec(sel_s.shape, const),
        ],
        out_specs=[tile_spec(0, GROUP_WIDTH, 0), tile_spec(1, GROUP_WIDTH, 0),
                   tile_spec(0, GROUP_WIDTH, 0), tile_spec(1, GROUP_WIDTH, 0)],
        out_shape=[out, out, out, out],
        scratch_shapes=[pltpu.VMEM((2, 2, LANES, 2 * LANES), F32), pltpu.VMEM((2, 8, LANES), F32),
                        pltpu.VMEM((2, 2, D_STATE, LANES), F32),
                        pltpu.VMEM((u, 768), F32)],
        compiler_params=_params("parallel", "arbitrary"),
        name="scans",
    )(pc, pc, pf, pf, pf, pf, pf, pf, pf, pf, gbias, alog, conv_w, conv_b, d_skip, mt3, sel_m, sel_s)


def _out_ffn_kernel(xctx_ref, xlat_ref, ya_ref, yb_ref, hf_ref, hb_ref, o_ref, yf_ref, ybw_ref, z_ref,
                    gm_ref, gs_ref, mod_ref, g2_ref, wo_ref, w1_ref, w2_ref, gfin_ref, out_ref,
                    *, final, nt_ctx, q_off):
    if q_off >= nt_ctx:
        x_in = xlat_ref[...]
    else:
        x_in = jnp.where(pl.program_id(1) + q_off < nt_ctx, xctx_ref[...], xlat_ref[...])
    low = _lane_is_low((1, LANES))
    ym = []
    for pair in range(2):
        sl = slice(pair * LANES, (pair + 1) * LANES)
        h = hf_ref[:, sl] + hb_ref[:, sl]
        hn = h * lax.rsqrt(_head_mean_square(h, low) + EPS) * gm_ref[:, sl]
        ym.append((hn * jax.nn.sigmoid(o_ref[:, sl])).astype(MXU_DTYPE))
    ys = []
    for grp in range(2):
        sl = slice(grp * LANES, (grp + 1) * LANES)
        z = z_ref[:, sl]
        ys.append((yf_ref[:, sl] + ybw_ref[:, sl]) * (z * jax.nn.sigmoid(z)))
    ms = (jnp.sum(ys[0] * ys[0], axis=-1, keepdims=True)
          + jnp.sum(ys[1] * ys[1], axis=-1, keepdims=True)) * (1.0 / GROUP_WIDTH)
    rs = lax.rsqrt(ms + EPS)
    yd = [(ys[grp] * rs * gs_ref[:, grp * LANES:(grp + 1) * LANES]).astype(MXU_DTYPE) for grp in range(2)]

    y = (_dot(ya_ref[...], wo_ref[0:GROUP_WIDTH, :])
         + _dot(yb_ref[...], wo_ref[GROUP_WIDTH:2 * GROUP_WIDTH, :])
         + _dot(jnp.concatenate(ym, axis=1), wo_ref[2 * GROUP_WIDTH:3 * GROUP_WIDTH, :])
         + _dot(jnp.concatenate(yd, axis=1), wo_ref[3 * GROUP_WIDTH:, :]))
    x1 = x_in + mod_ref[2:3, :] * y
    h2 = _rmsnorm_mod(x1, g2_ref[...], mod_ref[3:4, :], mod_ref[4:5, :]).astype(MXU_DTYPE)
    acc = jnp.zeros(x1.shape, F32)
    for f in range(D_FF // FF_CHUNK):
        sl = slice(f * FF_CHUNK, (f + 1) * FF_CHUNK)
        hf = jnp.maximum(_dot(h2, w1_ref[:, sl]), 0.0)
        acc = acc + _dot((hf * hf).astype(MXU_DTYPE), w2_ref[sl, :])
    x2 = x1 + mod_ref[5:6, :] * acc
    if final:
        ms2 = jnp.mean(x2 * x2, axis=-1, keepdims=True)
        x2 = x2 * lax.rsqrt(ms2 + EPS) * gfin_ref[...]
    out_ref[...] = x2


def _out_ffn(x_ctx, x_lat, lat_off, ya, yb, hf, hb, yf, ybw, pf, gm, gs, modv, g2, wo, w1, w2, g_final,
             n_ctx, final):
    bsz, u, _ = pf.shape
    d = x_lat.shape[-1]
    nt_ctx = n_ctx // ROW_TILE
    q_off = nt_ctx if final else 0
    nt = u // ROW_TILE - q_off
    row = lambda b, i: (b, i + q_off, 0)
    const = lambda b, i: (0, 0)
    grp_tile = pl.BlockSpec((None, ROW_TILE, GROUP_WIDTH), row)
    attn_tile = pl.BlockSpec((None, ROW_TILE, GROUP_WIDTH), lambda b, i: (b, i, 0))
    return pl.pallas_call(
        functools.partial(_out_ffn_kernel, final=final, nt_ctx=nt_ctx, q_off=q_off),
        grid=(bsz, nt),
        in_specs=_residual_specs(nt_ctx, lat_off, q_off, d) + [
            attn_tile, attn_tile, grp_tile, grp_tile,
            pl.BlockSpec((None, ROW_TILE, GROUP_WIDTH), lambda b, i: (b, i + q_off, _O_BLK)),
            grp_tile, grp_tile,
            pl.BlockSpec((None, ROW_TILE, GROUP_WIDTH), lambda b, i: (b, i + q_off, _Z_BLK)),
            pl.BlockSpec((1, GROUP_WIDTH), const),
            pl.BlockSpec((1, GROUP_WIDTH), const),
            _mod_spec(nt_ctx, q_off, d),
            pl.BlockSpec((1, d), const),
            pl.BlockSpec((d, d), const, pipeline_mode=_RESIDENT),
            pl.BlockSpec((d, D_FF), const, pipeline_mode=_RESIDENT),
            pl.BlockSpec((D_FF, d), const, pipeline_mode=_RESIDENT),
            pl.BlockSpec((1, d), const),
        ],
        out_specs=pl.BlockSpec((None, ROW_TILE, d), lambda b, i: (b, i, 0)),
        out_shape=jax.ShapeDtypeStruct((bsz, nt * ROW_TILE, d), F32),
        compiler_params=_params("parallel", "parallel"),
        name="out_ffn_final" if final else "out_ffn",
    )(x_ctx, x_lat, ya, yb, hf, hb, pf, yf, ybw, pf, gm, gs, modv, g2, wo, w1, w2, g_final)


def _gate_starts():
    cg, ddt = _OFF[10], _OFF[15]
    return [cg, cg + N_HEADS, cg + 2 * N_HEADS, cg + 3 * N_HEADS, ddt, ddt + N_HEADS]


def _gate_lanes(groups):
    pad = jnp.zeros((8 - N_HEADS,), F32)
    parts = []
    for g in groups:
        parts += [g.astype(F32), pad]
    parts.append(jnp.zeros((LANES - 8 * len(groups),), F32))
    return jnp.concatenate(parts).reshape(1, LANES)


def _proj_weight(w_in):
    (aq, ak, av, bq, bk, bv, cq, ck, cv, co, cg, dx, dz, db, dc, ddt) = _OFF[:-1]
    qscale = HEAD_DIM ** -0.5
    segs = []
    for base, scale in ((aq, qscale), (bq, 1.0)):
        segs += [(base + h * HEAD_DIM, HEAD_DIM, scale) for h in _Q_HEAD_ORDER]
        segs += [(base + GROUP_WIDTH, 2 * LANES, 1.0)]
    segs += [(cq, GROUP_WIDTH, qscale), (ck, 2 * GROUP_WIDTH, 1.0)]
    segs += [(dx, GROUP_WIDTH, 1.0), (db, 2 * GROUP_WIDTH, 1.0), (co, GROUP_WIDTH, 1.0), (dz, GROUP_WIDTH, 1.0)]
    parts = [(w_in[:, s:s + n] if scale == 1.0 else w_in[:, s:s + n] * scale).astype(MXU_DTYPE)
             for s, n, scale in segs]
    zeros4 = jnp.zeros((w_in.shape[0], 8 - N_HEADS), MXU_DTYPE)
    for start in _gate_starts():
        parts += [w_in[:, start:start + N_HEADS].astype(MXU_DTYPE), zeros4]
    parts.append(jnp.zeros((w_in.shape[0], LANES - 8 * len(_gate_starts())), MXU_DTYPE))
    w = jnp.concatenate(parts, axis=1)
    assert w.shape[1] == _N_PROJ
    return w


def _out_weight(w_out):
    parts = []
    for base in (0, GROUP_WIDTH):
        parts += [w_out[base + h * HEAD_DIM:base + (h + 1) * HEAD_DIM].astype(MXU_DTYPE) for h in _Q_HEAD_ORDER]
    parts.append(w_out[2 * GROUP_WIDTH:].astype(MXU_DTYPE))
    return jnp.concatenate(parts, axis=0)


def _rope_tables(n_tok, n_ctx):
    rows = n_tok // GRID_W
    half = HEAD_DIM // 2
    n_freq = half // 2
    inv_freq = ROPE_BASE ** (-jnp.arange(0, half, 2, dtype=F32) / half)
    pos = jnp.arange(max(rows, GRID_W), dtype=F32)
    ang = pos[:, None] * inv_freq[None, :]
    cos_u = jnp.tile(jnp.cos(ang), (1, LANES // n_freq))
    sin_u = jnp.tile(jnp.sin(ang), (1, LANES // n_freq))
    lane = np.arange(LANES)
    hd = lane % HEAD_DIM
    use_col = ((hd // half) == 1)[None, :]
    first = ((hd % half) < n_freq)[None, :]

    def per_token(tab):
        by_row = jnp.repeat(tab[:rows], GRID_W, axis=0)
        by_col = jnp.tile(tab[:GRID_W], (rows, 1))
        return jnp.where(use_col, by_col, by_row)

    cos = per_token(cos_u)
    sin = per_token(sin_u)
    sin = jnp.where(first, -sin, sin)
    cos = jnp.concatenate([jnp.ones((n_ctx, LANES), F32), cos], axis=0)
    sin = jnp.concatenate([jnp.zeros((n_ctx, LANES), F32), sin], axis=0)
    return cos, sin


def kernel(x, c, ctx, c_ctx, w_ada, b_ada, g_norm1, g_norm2, w_in, sink_a, g_q_b, g_k_b, b_igate, b_fgate,
           g_mlstm, conv_w, conv_b, a_log, dt_bias, d_skip, g_ssm, w_out, w_ff1, w_ff2, g_final):
    bsz, n_tok, d = x.shape
    n_ctx = ctx.shape[1]
    depth = w_in.shape[0]
    assert d == D_MODEL and n_ctx % ROW_TILE == 0 and n_tok % ROW_TILE == 0

    cos, sin = _rope_tables(n_tok, n_ctx)
    n_rows = 16
    cvec = jnp.concatenate([c, c_ctx[None, :], jnp.zeros((n_rows - bsz - 1, d), F32)], axis=0)
    mod_all = _ada_mod(cvec, w_ada, b_ada)

    u = n_ctx + n_tok
    x_ctx, x_lat, lat_off = ctx, x, 0
    for layer in range(depth):
        need_ctx = layer < depth - 1
        mod = mod_all[layer].reshape(n_rows, 6, d)
        pad = jnp.zeros((bsz, 2, d), F32)
        mod_lat = jnp.concatenate([mod[:bsz], pad], axis=1)
        mod_ctx = jnp.broadcast_to(jnp.concatenate([mod[bsz], pad[0]], axis=0), (bsz, 8, d))
        modv = jnp.stack([mod_ctx, mod_lat], axis=1)

        gq = jnp.tile(g_q_b[layer] * (HEAD_DIM ** -0.5 * LOG2E), 2).reshape(1, LANES)
        gk = jnp.tile(g_k_b[layer], 2).reshape(1, LANES)
        pa, pb, pc, pf = _in_proj(x_ctx, x_lat, lat_off, u, modv, g_norm1[layer].reshape(1, d),
                                  _proj_weight(w_in[layer]), cos, sin, gq, gk, n_ctx)

        ya = _attn_a(pa, sink_a[layer], n_ctx, need_ctx)
        yb = _attn_b(pb, n_ctx, need_ctx)

        gbias = _gate_lanes([b_igate[layer, 0], b_fgate[layer, 0], b_igate[layer, 1], b_fgate[layer, 1],
                             dt_bias[layer, 0], dt_bias[layer, 1]])
        alog = jnp.broadcast_to(jnp.pad(a_log[layer], ((0, 0), (0, 8 - N_HEADS)))[:, :, None], (2, 8, LANES))
        dsk = jnp.repeat(d_skip[layer], HEAD_DIM).reshape(1, GROUP_WIDTH)
        hf, hb, yf, ybw = _scans(pc, pf, gbias, alog, conv_w[layer], conv_b[layer].reshape(1, -1), dsk, n_ctx)

        xc = _out_ffn(x_ctx, x_lat, lat_off, ya, yb, hf, hb, yf, ybw, pf, g_mlstm[layer].reshape(1, GROUP_WIDTH),
                      g_ssm[layer].reshape(1, GROUP_WIDTH), modv, g_norm2[layer].reshape(1, d),
                      _out_weight(w_out[layer]), w_ff1[layer].astype(MXU_DTYPE), w_ff2[layer].astype(MXU_DTYPE),
                      g_final.reshape(1, d), n_ctx, final=not need_ctx)
        x_ctx, x_lat, lat_off = xc, xc, n_ctx // ROW_TILE
    return xc
```

```python
import functools
import math

import numpy as np
import jax
import jax.numpy as jnp
from jax import lax
from jax.experimental import pallas as pl
from jax.experimental.pallas import tpu as pltpu

F32 = jnp.float32
MXU_DTYPE = jnp.bfloat16

D_MODEL = 1024
HEAD_DIM = 64
LANES = 128
GRID_W = 64
WINDOW = 128
ROPE_BASE = 10000.0
EPS = 1e-6
N_HEADS = 4
GROUP_WIDTH = N_HEADS * HEAD_DIM
D_STATE = 128
CHUNK = 128
ROW_TILE = 256
CHUNKS_PER_TILE = ROW_TILE // CHUNK
KEY_TILE = 1024
D_FF = 4 * D_MODEL
FF_CHUNK = 1024
HALO = 8
VMEM_LIMIT = 56 * 1024 * 1024
LOG2E = math.log2(math.e)
EXP_ARG_MAX = 88.0

_SPLIT_SIZES = (256, 128, 128, 256, 128, 128, 256, 256, 256, 256, 16, 256, 256, 256, 256, 8)
_OFF = [int(o) for o in np.concatenate([[0], np.cumsum(_SPLIT_SIZES)])]
_Q_HEAD_ORDER = (0, 2, 1, 3)

_COL_A = 0
_COL_B = 512
_COL_C = 1024
_COL_F = 1792
_N_F = 768 + 256 + 256 + 128
_N_PROJ = _COL_F + _N_F
_GATE_BLK = (_N_F - LANES) // LANES
_O_BLK = 768 // GROUP_WIDTH
_Z_BLK = (768 + GROUP_WIDTH) // GROUP_WIDTH


def _dot(a, b):
    return jnp.dot(a, b, preferred_element_type=F32)


def _dot_nt(a, b):
    return lax.dot_general(a, b, (((1,), (1,)), ((), ())), preferred_element_type=F32)


def _dot_tn(a, b):
    return lax.dot_general(a, b, (((0,), (0,)), ((), ())), preferred_element_type=F32)


def _split3(x):
    hi = x.astype(MXU_DTYPE)
    r1 = x - hi.astype(F32)
    mid = r1.astype(MXU_DTYPE)
    lo = (r1 - mid.astype(F32)).astype(MXU_DTYPE)
    return [hi, mid, lo]


def _rows_cumsum(rows, mt3):
    return _dot(jnp.concatenate(_split3(rows), axis=1), mt3)


def _rows_to_columns(rows, sel3):
    return _dot_tn(jnp.concatenate(_split3(rows), axis=0), sel3)


def _cummax_lanes(x, direction):
    n = x.shape[-1]
    lane = lax.broadcasted_iota(jnp.int32, x.shape, 1)

    def shifted(v, shift):
        if direction == 0:
            return jnp.where(lane >= shift, pltpu.roll(v, shift, 1), -jnp.inf)
        return jnp.where(lane < n - shift, pltpu.roll(v, n - shift, 1), -jnp.inf)

    def tree_max(vals):
        while len(vals) > 1:
            vals = [jnp.maximum(a, b) for a, b in zip(vals[::2], vals[1::2])] + ([vals[-1]] if len(vals) % 2 else [])
        return vals[0]

    window = tree_max([x] + [shifted(x, s) for s in range(1, 8)])
    return tree_max([window] + [shifted(window, s) for s in range(8, n, 8)])


def _lane_is_low(shape):
    lane = lax.broadcasted_iota(jnp.int32, shape, len(shape) - 1)
    return (lane % LANES) < HEAD_DIM


def _head_mean_square(t, low):
    sq = t * t
    lo = jnp.sum(jnp.where(low, sq, 0.0), axis=-1, keepdims=True)
    hi = jnp.sum(jnp.where(low, 0.0, sq), axis=-1, keepdims=True)
    return jnp.where(low, lo, hi) * (1.0 / HEAD_DIM)


def _params(*sem):
    return pltpu.CompilerParams(dimension_semantics=sem, vmem_limit_bytes=VMEM_LIMIT)


_RESIDENT = pl.Buffered(1)


def _ada_kernel(c_ref, w_ref, b_ref, out_ref):
    cv = c_ref[...]
    cv = cv * jax.nn.sigmoid(cv)
    out_ref[...] = _dot(cv.astype(MXU_DTYPE), w_ref[...].astype(MXU_DTYPE)) + b_ref[...]


def _ada_mod(cvec, w_ada, b_ada):
    depth, d, n = w_ada.shape
    rows = cvec.shape[0]
    tn = 1536
    return pl.pallas_call(
        _ada_kernel,
        grid=(depth, n // tn),
        in_specs=[
            pl.BlockSpec((rows, d), lambda l, j: (0, 0)),
            pl.BlockSpec((None, d, tn), lambda l, j: (l, 0, j)),
            pl.BlockSpec((None, 1, tn), lambda l, j: (l, 0, j)),
        ],
        out_specs=pl.BlockSpec((None, rows, tn), lambda l, j: (l, 0, j)),
        out_shape=jax.ShapeDtypeStruct((depth, rows, n), F32),
        compiler_params=_params("parallel", "parallel"),
        name="ada_mod",
    )(cvec, w_ada, b_ada.reshape(depth, 1, n))


def _rmsnorm_mod(x, g, shift, scale):
    ms = jnp.mean(x * x, axis=-1, keepdims=True)
    return (x * lax.rsqrt(ms + EPS) * g) * (1.0 + scale) + shift


def _in_proj_kernel(xctx_ref, xlat_ref, mod_ref, g1_ref, w_ref, cos_ref, sin_ref, gq_ref, gk_ref,
                    a_ref, b_ref, c_ref, f_ref, *, nt_ctx):
    x = jnp.where(pl.program_id(1) < nt_ctx, xctx_ref[...], xlat_ref[...])
    hn = _rmsnorm_mod(x, g1_ref[...], mod_ref[0:1, :], mod_ref[1:2, :])
    hb = hn.astype(MXU_DTYPE)
    cos = cos_ref[...]
    sin = sin_ref[...]
    lane = lax.broadcasted_iota(jnp.int32, (1, LANES), 1)
    first = (lane % 32) < 16
    low = _lane_is_low((1, LANES))

    def rope(t):
        partner = jnp.where(first, pltpu.roll(t, LANES - 16, 1), pltpu.roll(t, 16, 1))
        return t * cos + partner * sin

    pa = _dot(hb, w_ref[:, _COL_A:_COL_A + 512])
    for j in range(3):
        a_ref[:, j * LANES:(j + 1) * LANES] = rope(pa[:, j * LANES:(j + 1) * LANES]).astype(a_ref.dtype)
    a_ref[:, 3 * LANES:] = pa[:, 3 * LANES:].astype(a_ref.dtype)

    pb = _dot(hb, w_ref[:, _COL_B:_COL_B + 512])
    for j in range(3):
        t = pb[:, j * LANES:(j + 1) * LANES]
        g = gq_ref[...] if j < 2 else gk_ref[...]
        t = t * lax.rsqrt(_head_mean_square(t, low) + EPS) * g
        b_ref[:, j * LANES:(j + 1) * LANES] = rope(t).astype(b_ref.dtype)
    b_ref[:, 3 * LANES:] = pb[:, 3 * LANES:].astype(b_ref.dtype)

    c_ref[...] = _dot(hb, w_ref[:, _COL_C:_COL_F]).astype(c_ref.dtype)
    f_ref[...] = _dot(hb, w_ref[:, _COL_F:])


def _residual_specs(nt_ctx, lat_off, q_off, d):
    ctx_spec = pl.BlockSpec((None, ROW_TILE, d), lambda b, i: (b, jnp.minimum(i + q_off, nt_ctx - 1), 0))
    lat_spec = pl.BlockSpec((None, ROW_TILE, d),
                            lambda b, i: (b, jnp.maximum(i + q_off - nt_ctx, 0) + lat_off, 0))
    return [ctx_spec, lat_spec]


def _mod_spec(nt_ctx, q_off, d):
    return pl.BlockSpec((None, None, 8, d), lambda b, i: (b, jnp.where(i + q_off < nt_ctx, 0, 1), 0, 0))


def _in_proj(x_ctx, x_lat, lat_off, u, modv, g1, w, cos, sin, gq, gk, n_ctx):
    bsz, _, d = x_lat.shape
    nt = u // ROW_TILE
    nt_ctx = n_ctx // ROW_TILE
    row = lambda b, i: (b, i, 0)
    const = lambda b, i: (0, 0)
    return pl.pallas_call(
        functools.partial(_in_proj_kernel, nt_ctx=nt_ctx),
        grid=(bsz, nt),
        in_specs=_residual_specs(nt_ctx, lat_off, 0, d) + [
            _mod_spec(nt_ctx, 0, d),
            pl.BlockSpec((1, d), const),
            pl.BlockSpec((d, _N_PROJ), const, pipeline_mode=_RESIDENT),
            pl.BlockSpec((ROW_TILE, LANES), lambda b, i: (i, 0)),
            pl.BlockSpec((ROW_TILE, LANES), lambda b, i: (i, 0)),
            pl.BlockSpec((1, LANES), const),
            pl.BlockSpec((1, LANES), const),
        ],
        out_specs=[
            pl.BlockSpec((None, ROW_TILE, 512), row),
            pl.BlockSpec((None, ROW_TILE, 512), row),
            pl.BlockSpec((None, ROW_TILE, 768), row),
            pl.BlockSpec((None, ROW_TILE, _N_F), row),
        ],
        out_shape=[
            jax.ShapeDtypeStruct((bsz, u, 512), MXU_DTYPE),
            jax.ShapeDtypeStruct((bsz, u, 512), MXU_DTYPE),
            jax.ShapeDtypeStruct((bsz, u, 768), MXU_DTYPE),
            jax.ShapeDtypeStruct((bsz, u, _N_F), F32),
        ],
        compiler_params=_params("parallel", "parallel"),
        name="in_proj",
    )(x_ctx, x_lat, modv, g1, w, cos, sin, gq, gk)


def _stack_heads(q_ref, qs_ref):
    low = _lane_is_low((1, LANES))
    for jb in range(2):
        q = q_ref[:, jb * LANES:(jb + 1) * LANES]
        for half in range(2):
            idx = 2 * jb + half
            keep = low if half == 0 else jnp.logical_not(low)
            qs_ref[idx * ROW_TILE:(idx + 1) * ROW_TILE, :] = jnp.where(keep, q, jnp.zeros_like(q))


def _unstack_heads(o, out_ref):
    low = _lane_is_low((1, LANES))
    for jb in range(2):
        lo = o[(2 * jb) * ROW_TILE:(2 * jb + 1) * ROW_TILE, :]
        hi = o[(2 * jb + 1) * ROW_TILE:(2 * jb + 2) * ROW_TILE, :]
        out_ref[:, jb * LANES:(jb + 1) * LANES] = jnp.where(low, lo, hi).astype(out_ref.dtype)


def _attn_a_kernel(sink_ref, q_ref, kp_ref, ko_ref, kn_ref, kc_ref, vp_ref, vo_ref, vn_ref, vc_ref,
                   out_ref, qs_ref, kbuf, vbuf, s_ref, o_ref, *, n_tok, n_ctx, q_off):
    i = pl.program_id(1) + q_off
    nb = 2 * WINDOW + ROW_TILE
    nk = nb + n_ctx
    kbuf[0:WINDOW] = kp_ref[...]
    kbuf[WINDOW:WINDOW + ROW_TILE] = ko_ref[...]
    kbuf[WINDOW + ROW_TILE:nb] = kn_ref[...]
    kbuf[nb:nk] = kc_ref[...]
    vbuf[0:WINDOW, 0:LANES] = vp_ref[...]
    vbuf[WINDOW:WINDOW + ROW_TILE, 0:LANES] = vo_ref[...]
    vbuf[WINDOW + ROW_TILE:nb, 0:LANES] = vn_ref[...]
    vbuf[nb:nk, 0:LANES] = vc_ref[...]
    vbuf[:, LANES:] = jnp.ones((nk, LANES), vbuf.dtype)
    _stack_heads(q_ref, qs_ref)

    r = lax.broadcasted_iota(jnp.int32, (ROW_TILE, nk), 0)
    c = lax.broadcasted_iota(jnp.int32, (ROW_TILE, nk), 1)
    kpos = (i - 1) * ROW_TILE - WINDOW + c
    dist = c - r
    band = jnp.where(dist >= 0, jnp.where(dist <= 2 * WINDOW, 1, 0), 0)
    band = jnp.where(kpos >= 0, jnp.where(kpos < n_tok, band, 0), 0)
    band = jnp.where(i >= 1, band, 0)
    bias = jnp.where(c >= nb, 0.0, jnp.where(band > 0, 0.0, -jnp.inf)).astype(F32)

    for idx in range(N_HEADS):
        rows = slice(idx * ROW_TILE, (idx + 1) * ROW_TILE)
        s_ref[rows, :] = _dot_nt(qs_ref[rows, :], kbuf[...]) + bias
    for idx in range(N_HEADS):
        rows = slice(idx * ROW_TILE, (idx + 1) * ROW_TILE)
        sink = sink_ref[_Q_HEAD_ORDER[idx]] * LOG2E
        m = jnp.maximum(jnp.max(s_ref[rows, :], axis=-1, keepdims=True), sink)
        p = jnp.exp2(s_ref[rows, :] - m)
        pv = _dot(p.astype(MXU_DTYPE), vbuf[...])
        o_ref[rows, :] = pv[:, :LANES] / (pv[:, LANES:] + jnp.exp2(sink - m))
    _unstack_heads(o_ref[...], out_ref)


def _attn_a(pa, sink, n_ctx, need_ctx):
    bsz, u, _ = pa.shape
    n_tok = u - n_ctx
    q_off = 0 if need_ctx else n_ctx // ROW_TILE
    nt = u // ROW_TILE - q_off
    last_blk = u // WINDOW - 1
    rpw = ROW_TILE // WINDOW
    nk = 2 * WINDOW + ROW_TILE + n_ctx

    def own(col):
        return lambda b, i: (b, i + q_off, col)

    def prev(col):
        return lambda b, i: (b, jnp.maximum((i + q_off) * rpw - 1, 0), col)

    def nxt(col):
        return lambda b, i: (b, jnp.minimum((i + q_off + 1) * rpw, last_blk), col)

    def ctx(col):
        return lambda b, i: (b, 0, col)

    kern = functools.partial(_attn_a_kernel, n_tok=n_tok, n_ctx=n_ctx, q_off=q_off)
    return pl.pallas_call(
        kern,
        grid=(bsz, nt),
        in_specs=[
            pl.BlockSpec(memory_space=pltpu.SMEM),
            pl.BlockSpec((None, ROW_TILE, 2 * LANES), own(0)),
            pl.BlockSpec((None, WINDOW, LANES), prev(2)),
            pl.BlockSpec((None, ROW_TILE, LANES), own(2)),
            pl.BlockSpec((None, WINDOW, LANES), nxt(2)),
            pl.BlockSpec((None, n_ctx, LANES), ctx(2)),
            pl.BlockSpec((None, WINDOW, LANES), prev(3)),
            pl.BlockSpec((None, ROW_TILE, LANES), own(3)),
            pl.BlockSpec((None, WINDOW, LANES), nxt(3)),
            pl.BlockSpec((None, n_ctx, LANES), ctx(3)),
        ],
        out_specs=pl.BlockSpec((None, ROW_TILE, 2 * LANES), lambda b, i: (b, i, 0)),
        out_shape=jax.ShapeDtypeStruct((bsz, nt * ROW_TILE, 2 * LANES), MXU_DTYPE),
        scratch_shapes=[pltpu.VMEM((N_HEADS * ROW_TILE, LANES), MXU_DTYPE),
                        pltpu.VMEM((nk, LANES), MXU_DTYPE), pltpu.VMEM((nk, 2 * LANES), MXU_DTYPE),
                        pltpu.VMEM((N_HEADS * ROW_TILE, nk), F32),
                        pltpu.VMEM((N_HEADS * ROW_TILE, LANES), F32)],
        compiler_params=_params("parallel", "parallel"),
        name="attn_window",
    )(sink, pa, pa, pa, pa, pa, pa, pa, pa, pa)


def _attn_b_kernel(q_ref, k_ref, v_ref, out_ref, qs_ref, vext_ref, s_ref, p_ref, m_ref, alpha_ref, acc_ref,
                   *, n_ctx, n_tok, q_off, tk):
    qi = pl.program_id(1) + q_off
    n_rows = N_HEADS * ROW_TILE

    @pl.when(pl.program_id(1) == 0)
    def _():
        vext_ref[:, 0:LANES] = v_ref[...]
        vext_ref[:, LANES:] = jnp.ones((vext_ref.shape[0], LANES), vext_ref.dtype)

    _stack_heads(q_ref, qs_ref)

    def scores(slot, start, size):
        s_ref[slot, :, 0:size] = _dot_nt(qs_ref[...], k_ref[start:start + size, :])

    def softmax(slot, size, first):
        m_new = jnp.max(s_ref[slot, :, 0:size], axis=-1, keepdims=True)
        if not first:
            m_old = m_ref[...]
            m_new = jnp.maximum(m_old, m_new)
            alpha_ref[slot] = jnp.exp2(m_old - m_new)
        m_ref[...] = m_new
        p_ref[slot, :, 0:size] = jnp.exp2(s_ref[slot, :, 0:size] - m_new).astype(p_ref.dtype)

    def accumulate(slot, start, size, first):
        pv = _dot(p_ref[slot, :, 0:size], vext_ref[start:start + size, :])
        if first:
            acc_ref[...] = pv
        else:
            acc_ref[...] = alpha_ref[slot] * acc_ref[...] + pv

    def attend(tiles):
        scores(0, *tiles[0])
        for t, (start, size) in enumerate(tiles):
            if t + 1 < len(tiles):
                scores((t + 1) % 2, *tiles[t + 1])
            softmax(t % 2, size, t == 0)
            accumulate(t % 2, start, size, t == 0)
        _unstack_heads(acc_ref[:, 0:LANES] / acc_ref[:, LANES:], out_ref)

    ctx_tiles = [(0, n_ctx)]
    all_tiles = ctx_tiles + [(n_ctx + t * tk, tk) for t in range(n_tok // tk)]
    if q_off == 0:
        pl.when(qi == 0)(functools.partial(attend, ctx_tiles))
        pl.when(qi > 0)(functools.partial(attend, all_tiles))
    else:
        attend(all_tiles)


def _attn_b(pb, n_ctx, need_ctx):
    bsz, u, _ = pb.shape
    n_tok = u - n_ctx
    q_off = 0 if need_ctx else n_ctx // ROW_TILE
    nt = u // ROW_TILE - q_off
    tk = min(KEY_TILE, n_tok)
    assert n_tok % tk == 0
    kern = functools.partial(_attn_b_kernel, n_ctx=n_ctx, n_tok=n_tok, q_off=q_off, tk=tk)
    return pl.pallas_call(
        kern,
        grid=(bsz, nt),
        in_specs=[
            pl.BlockSpec((None, ROW_TILE, 2 * LANES), lambda b, i: (b, i + q_off, 0)),
            pl.BlockSpec((None, u, LANES), lambda b, i: (b, 0, 2)),
            pl.BlockSpec((None, u, LANES), lambda b, i: (b, 0, 3)),
        ],
        out_specs=pl.BlockSpec((None, ROW_TILE, 2 * LANES), lambda b, i: (b, i, 0)),
        out_shape=jax.ShapeDtypeStruct((bsz, nt * ROW_TILE, 2 * LANES), MXU_DTYPE),
        scratch_shapes=[
            pltpu.VMEM((N_HEADS * ROW_TILE, LANES), MXU_DTYPE),
            pltpu.VMEM((u, 2 * LANES), MXU_DTYPE),
            pltpu.VMEM((2, N_HEADS * ROW_TILE, tk), F32),
            pltpu.VMEM((2, N_HEADS * ROW_TILE, tk), MXU_DTYPE),
            pltpu.VMEM((N_HEADS * ROW_TILE, 1), F32),
            pltpu.VMEM((2, N_HEADS * ROW_TILE, 1), F32),
            pltpu.VMEM((N_HEADS * ROW_TILE, 2 * LANES), F32),
        ],
        compiler_params=_params("parallel", "arbitrary"),
        name="attn_dense",
    )(pb, pb, pb)


def _scan_tile(j, direction, nt_ctx, nt):
    if direction == 0:
        return j
    return jnp.where(j < nt_ctx, nt_ctx - 1 - j, nt + nt_ctx - 1 - j)


def _causal_mask(direction):
    t = lax.broadcasted_iota(jnp.int32, (CHUNK, CHUNK), 0)
    s = lax.broadcasted_iota(jnp.int32, (CHUNK, CHUNK), 1)
    return (s <= t) if direction == 0 else (s >= t)


def _chunk_order(direction):
    order = range(CHUNKS_PER_TILE)
    return order if direction == 0 else reversed(order)


def _log_sigmoid(x):
    return jnp.minimum(x, 0.0) - jnp.log1p(jnp.exp(-jnp.abs(x)))


def _softplus(x):
    return jnp.maximum(x, 0.0) + jnp.log1p(jnp.exp(-jnp.abs(x)))


def _mlstm_gates(direction, g_rows, m_prev, mt3):
    end = CHUNK - 1 if direction == 0 else 0
    i_r = jnp.concatenate([g[16 * direction:16 * direction + 8] for g in g_rows], axis=0)
    f_r = jnp.concatenate([g[16 * direction + 8:16 * direction + 16] for g in g_rows], axis=0)
    b_r = _rows_cumsum(_log_sigmoid(f_r), mt3)
    r_r = i_r - b_r
    cmax = _cummax_lanes(r_r, direction)
    tot = jnp.broadcast_to(b_r[:, end:end + 1], b_r.shape)
    g2 = tot - b_r + i_r
    g2max = jnp.broadcast_to(jnp.max(g2, axis=-1, keepdims=True), g2.shape)
    per_chunk = []
    for idx in range(len(g_rows)):
        sl = slice(8 * idx, 8 * idx + 8)
        c_r = jnp.maximum(m_prev, cmax[sl])
        dp_r = jnp.exp(m_prev - c_r)
        em_r = jnp.exp(jnp.minimum(-(b_r[sl] + c_r), EXP_ARG_MAX))
        m_new = jnp.maximum(tot[sl] + m_prev, g2max[sl])
        wk_r = jnp.exp(g2[sl] - m_new)
        cd_r = jnp.exp(tot[sl] + m_prev - m_new)
        per_chunk.append((jnp.concatenate([c_r * LOG2E, dp_r, em_r, wk_r], axis=0), r_r[sl] * LOG2E, cd_r))
        m_prev = m_new
    return per_chunk, m_prev


def _mlstm_scores(qkvs):
    low1 = _lane_is_low((1, LANES))
    ones = jnp.ones((CHUNK, LANES), MXU_DTYPE)
    pre = []
    for qkv in qkvs:
        per_pair = []
        for pair in range(2):
            q = qkv[:, pair * LANES:(pair + 1) * LANES]
            k = qkv[:, GROUP_WIDTH + pair * LANES:GROUP_WIDTH + (pair + 1) * LANES]
            v = qkv[:, 2 * GROUP_WIDTH + pair * LANES:2 * GROUP_WIDTH + (pair + 1) * LANES]
            qk = [_dot_nt(jnp.where(low1 if half == 0 else jnp.logical_not(low1), q, jnp.zeros_like(q)), k)
                  for half in range(2)]
            per_pair.append((q, k, jnp.concatenate([v, ones], axis=1), qk))
        pre.append(per_pair)
    return pre


def _mlstm_intra(items, pre):
    low1 = _lane_is_low((1, LANES))
    r2 = lax.broadcasted_iota(jnp.int32, (LANES, 2 * LANES), 0)
    c2 = lax.broadcasted_iota(jnp.int32, (LANES, 2 * LANES), 1)
    state_mask = (r2 < HEAD_DIM) == ((c2 % LANES) < HEAD_DIM)
    mid = []
    for (direction, qkv, r_r, cd_r, cols), per_pair in zip(items, pre):
        mask = _causal_mask(direction)
        res = []
        for pair, (q, k, vext, qk) in enumerate(per_pair):
            svs, dens = [], []
            for half in range(2):
                h = 2 * pair + half
                w = jnp.exp2(jnp.where(mask, r_r[h:h + 1, :] - cols[:, h * LANES:(h + 1) * LANES], -jnp.inf))
                s = qk[half] * w
                svs.append(_dot(s.astype(MXU_DTYPE), vext[:, :LANES]))
                dens.append(jnp.sum(s, axis=-1, keepdims=True))
            wk = cols[:, (8 + pair) * LANES:(9 + pair) * LANES]
            upd = _dot_tn((k.astype(F32) * wk).astype(MXU_DTYPE), vext)
            res.append((jnp.where(low1, svs[0], svs[1]), jnp.where(low1, dens[0], dens[1]),
                        jnp.where(state_mask, upd, 0.0)))
        mid.append(res)
    return mid


def _mlstm_recurrence(items, pre, mid, state):
    low2 = _lane_is_low((1, 2 * LANES))
    outs = []
    for (direction, qkv, r_r, cd_r, cols), per_pair, res in zip(items, pre, mid):
        out = []
        for pair in range(2):
            q = per_pair[pair][0]
            sv, den_intra, upd = res[pair]
            prior = _dot(q, state[direction][pair].astype(MXU_DTYPE))
            dp = cols[:, (4 + pair) * LANES:(5 + pair) * LANES]
            em = cols[:, (6 + pair) * LANES:(7 + pair) * LANES]
            num = sv + dp * prior[:, :LANES]
            den = den_intra + dp * prior[:, LANES:]
            out.append(num / jnp.maximum(jnp.abs(den), em))
            cd0 = jnp.concatenate([cd_r[2 * pair:2 * pair + 1]] * 2, axis=1)
            cd1 = jnp.concatenate([cd_r[2 * pair + 1:2 * pair + 2]] * 2, axis=1)
            state[direction][pair] = state[direction][pair] * jnp.where(low2, cd0, cd1) + upd
        outs.append(out)
    return outs


def _gate_rows(g_refs, gbias_ref):
    gbias = gbias_ref[...]
    return [[(g_ref[ci * CHUNK:(ci + 1) * CHUNK, :] + gbias).T for ci in _chunk_order(direction)]
            for direction, g_ref in enumerate(g_refs)]


def _mlstm_body(g_rows, qkvf_ref, qkvb_ref, mt3_ref, sel3_ref, hf_ref, hb_ref, st_ref, m_ref):
    dirs = ((qkvf_ref, None, hf_ref), (qkvb_ref, None, hb_ref))
    work = [(direction, idx) for idx in range(CHUNKS_PER_TILE) for direction in range(2)]
    row_slices = []
    for direction, idx in work:
        ci = list(_chunk_order(direction))[idx]
        row_slices.append(slice(ci * CHUNK, (ci + 1) * CHUNK))
    gates = []
    for direction in range(2):
        per_chunk, m_new = _mlstm_gates(direction, g_rows[direction], m_ref[direction], mt3_ref[direction])
        m_ref[direction] = m_new
        gates.append(per_chunk)
    yield
    rows_all = jnp.concatenate([gates[direction][idx][0] for direction, idx in work], axis=1)
    cols_all = _rows_to_columns(rows_all, sel3_ref[...])
    qkvs = [dirs[direction][0][rows, :] for (direction, _), rows in zip(work, row_slices)]
    pre = _mlstm_scores(qkvs)
    yield
    state = [[st_ref[direction, pair] for pair in range(2)] for direction in range(2)]
    items = []
    for n, (direction, idx) in enumerate(work):
        _, r_r, cd_r = gates[direction][idx]
        items.append((direction, qkvs[n], r_r, cd_r, cols_all[n * CHUNK:(n + 1) * CHUNK, :]))
    mid = _mlstm_intra(items, pre)
    yield
    outs = _mlstm_recurrence(items, pre, mid, state)
    for (direction, *_), rows, out in zip(items, row_slices, outs):
        for pair in range(2):
            dirs[direction][2][rows, pair * LANES:(pair + 1) * LANES] = out[pair]
    for direction in range(2):
        for pair in range(2):
            st_ref[direction, pair] = state[direction][pair]


def _scan_masks():
    s = np.arange(CHUNK)[:, None]
    t = np.arange(CHUNK)[None, :]
    mats = [np.tile((s <= t).astype(np.float32), (3, 1)), np.tile((s >= t).astype(np.float32), (3, 1))]
    return jnp.asarray(np.stack(mats), MXU_DTYPE)


def _column_selector(n_head_groups, n_pair_groups):
    n_groups = n_head_groups + n_pair_groups
    n_rows = 8 * n_groups + (-8 * n_groups) % 16
    n_cols = (N_HEADS * n_head_groups + 2 * n_pair_groups) * LANES
    sel = np.zeros((n_rows, n_cols), np.float32)
    col = 0
    for g in range(n_head_groups):
        for h in range(N_HEADS):
            sel[8 * g + h, col:col + LANES] = 1.0
            col += LANES
    for g in range(n_head_groups, n_groups):
        for pair in range(2):
            sel[8 * g + 2 * pair, col:col + HEAD_DIM] = 1.0
            sel[8 * g + 2 * pair + 1, col + HEAD_DIM:col + LANES] = 1.0
            col += LANES
    return jnp.asarray(np.tile(sel, (3, 1)), MXU_DTYPE)


def _ssd_conv(x_ref, xp_ref, xn_ref, cw_ref, cb_ref, tile, nt_ctx, nt):
    xin = x_ref[...]
    has_prev = (tile != 0) & (tile != nt_ctx)
    has_next = (tile != nt_ctx - 1) & (tile != nt - 1)
    prow = jnp.where(has_prev, xp_ref[HALO - 1:HALO, :], 0.0)
    nrow = jnp.where(has_next, xn_ref[0:1, :], 0.0)
    ridx = lax.broadcasted_iota(jnp.int32, (ROW_TILE, 1), 0)
    up = jnp.where(ridx == 0, prow, pltpu.roll(xin, 1, 0))
    dn = jnp.where(ridx == ROW_TILE - 1, nrow, pltpu.roll(xin, ROW_TILE - 1, 0))
    u = cw_ref[0:1, :] * up + cw_ref[1:2, :] * xin + cw_ref[2:3, :] * dn + cb_ref[...]
    return u * jax.nn.sigmoid(u)


def _ssd_gates(direction, g_rows, neg_a, mt3):
    end = CHUNK - 1 if direction == 0 else 0
    n = len(g_rows)
    dt_r = _softplus(jnp.concatenate([g[32 + 8 * direction:40 + 8 * direction] for g in g_rows], axis=0))
    cum_r = _rows_cumsum(dt_r * jnp.concatenate([neg_a] * n, axis=0), mt3)
    tot = jnp.broadcast_to(cum_r[:, end:end + 1], cum_r.shape)
    e_r = jnp.exp(cum_r)
    wend_r = jnp.exp(tot - cum_r) * dt_r
    d_r = jnp.exp(tot)
    zero = jnp.zeros((8, CHUNK), F32)
    per_chunk = []
    for idx in range(n):
        sl = slice(8 * idx, 8 * idx + 8)
        per_chunk.append((jnp.concatenate([cum_r[sl] * LOG2E, e_r[sl], wend_r[sl], zero], axis=0),
                          cum_r[sl] * LOG2E, dt_r[sl], d_r[sl]))
    return per_chunk


def _ssd_scores(us):
    pre = []
    for u in us:
        per_grp = []
        for grp in range(2):
            bm = u[:, GROUP_WIDTH + grp * D_STATE:GROUP_WIDTH + (grp + 1) * D_STATE].astype(MXU_DTYPE)
            cm = u[:, 2 * GROUP_WIDTH + grp * D_STATE:2 * GROUP_WIDTH + (grp + 1) * D_STATE].astype(MXU_DTYPE)
            x_pair = u[:, grp * LANES:(grp + 1) * LANES]
            per_grp.append((bm, cm, x_pair, _dot_nt(cm, bm)))
        pre.append(per_grp)
    return pre


def _ssd_intra(items, pre):
    low1 = _lane_is_low((1, LANES))
    mid = []
    for (direction, u, cum_r, dt_r, d_r, cols), per_grp in zip(items, pre):
        mask = _causal_mask(direction)
        res = []
        for grp, (bm, cm, x_pair, gmat) in enumerate(per_grp):
            xb = x_pair.astype(MXU_DTYPE)
            ys = []
            for half in range(2):
                h = 2 * grp + half
                decay = jnp.exp2(jnp.where(mask, cols[:, h * LANES:(h + 1) * LANES] - cum_r[h:h + 1, :], -jnp.inf))
                s = gmat * decay * dt_r[h:h + 1, :]
                ys.append(_dot(s.astype(MXU_DTYPE), xb))
            w_pair = cols[:, (6 + grp) * LANES:(7 + grp) * LANES]
            upd = _dot_tn(bm, (x_pair * w_pair).astype(MXU_DTYPE))
            res.append((jnp.where(low1, ys[0], ys[1]), upd))
        mid.append(res)
    return mid


def _ssd_recurrence(items, pre, mid, state):
    low1 = _lane_is_low((1, LANES))
    outs = []
    for (direction, u, cum_r, dt_r, d_r, cols), per_grp, res in zip(items, pre, mid):
        out = []
        for grp in range(2):
            cm = per_grp[grp][1]
            y_intra, upd = res[grp]
            ch = _dot(cm, state[direction][grp].astype(MXU_DTYPE))
            e_pair = cols[:, (4 + grp) * LANES:(5 + grp) * LANES]
            out.append(y_intra + e_pair * ch)
            d_pair = jnp.where(low1, d_r[2 * grp:2 * grp + 1], d_r[2 * grp + 1:2 * grp + 2])
            state[direction][grp] = state[direction][grp] * d_pair + upd
        outs.append(out)
    return outs


def _ssd_convs(xf_ref, xfp_ref, xfn_ref, xb_ref, xbp_ref, xbn_ref, cw_ref, cb_ref, u_ref, *, nt_ctx, nt):
    j = pl.program_id(1)
    tile_f = _scan_tile(j, 0, nt_ctx, nt)
    tile_b = _scan_tile(j, 1, nt_ctx, nt)
    step_b_visits_f = jnp.where(tile_f < nt_ctx, nt_ctx - 1 - tile_f, nt + nt_ctx - 1 - tile_f)
    rows_f = pl.ds(pl.multiple_of(tile_f * ROW_TILE, ROW_TILE), ROW_TILE)
    rows_b = pl.ds(pl.multiple_of(tile_b * ROW_TILE, ROW_TILE), ROW_TILE)

    @pl.when(step_b_visits_f >= j)
    def _():
        u_ref[rows_f, :] = _ssd_conv(xf_ref, xfp_ref, xfn_ref, cw_ref, cb_ref, tile_f, nt_ctx, nt)

    @pl.when(tile_b > j)
    def _():
        u_ref[rows_b, :] = _ssd_conv(xb_ref, xbp_ref, xbn_ref, cw_ref, cb_ref, tile_b, nt_ctx, nt)

    return [u_ref[rows_f, :], u_ref[rows_b, :]]


def _ssd_body(g_rows, convs, alog_ref, dskip_ref, mt3_ref, sel3_ref, yf_ref, yb_ref, st_ref):
    dirs = ((None, None, None, None, yf_ref), (None, None, None, None, yb_ref))
    gates = []
    for direction in range(2):
        gates.append(_ssd_gates(direction, g_rows[direction], -jnp.exp(alog_ref[direction]), mt3_ref[direction]))
    work = [(direction, idx) for idx in range(CHUNKS_PER_TILE) for direction in range(2)]
    yield
    rows_all = jnp.concatenate([gates[direction][idx][0] for direction, idx in work], axis=1)
    cols_all = _rows_to_columns(rows_all, sel3_ref[...])
    items, row_slices = [], []
    for n, (direction, idx) in enumerate(work):
        ci = list(_chunk_order(direction))[idx]
        rows = slice(ci * CHUNK, (ci + 1) * CHUNK)
        _, cum_r, dt_r, d_r = gates[direction][idx]
        items.append((direction, convs[direction][rows, :], cum_r, dt_r, d_r,
                      cols_all[n * CHUNK:(n + 1) * CHUNK, :]))
        row_slices.append(rows)
    pre = _ssd_scores([item[1] for item in items])
    yield
    mid = _ssd_intra(items, pre)
    yield
    state = [[st_ref[direction, grp] for grp in range(2)] for direction in range(2)]
    outs = _ssd_recurrence(items, pre, mid, state)
    for (direction, u, *_), rows, out in zip(items, row_slices, outs):
        for grp in range(2):
            sl = slice(grp * LANES, (grp + 1) * LANES)
            y = out[grp]
            if direction == 0:
                y = y + dskip_ref[:, sl] * u[:, sl]
            dirs[direction][4][rows, sl] = y
    for direction in range(2):
        for grp in range(2):
            st_ref[direction, grp] = state[direction][grp]


def _scans_kernel(qkvf_ref, qkvb_ref, gf_ref, gb_ref, xf_ref, xfp_ref, xfn_ref, xb_ref, xbp_ref, xbn_ref,
                  gbias_ref, alog_ref, cw_ref, cb_ref, dskip_ref, mt3_ref, selm_ref, sels_ref,
                  hf_ref, hb_ref, yf_ref, yb_ref, stm_ref, m_ref, sts_ref, u_ref, *, nt_ctx, nt):
    @pl.when(pl.program_id(1) == 0)
    def _():
        stm_ref[...] = jnp.zeros(stm_ref.shape, F32)
        m_ref[...] = jnp.zeros(m_ref.shape, F32)
        sts_ref[...] = jnp.zeros(sts_ref.shape, F32)

    convs = _ssd_convs(xf_ref, xfp_ref, xfn_ref, xb_ref, xbp_ref, xbn_ref, cw_ref, cb_ref, u_ref,
                       nt_ctx=nt_ctx, nt=nt)
    g_rows = _gate_rows((gf_ref, gb_ref), gbias_ref)
    bodies = [
        _mlstm_body(g_rows, qkvf_ref, qkvb_ref, mt3_ref, selm_ref, hf_ref, hb_ref, stm_ref, m_ref),
        _ssd_body(g_rows, convs, alog_ref, dskip_ref, mt3_ref, sels_ref, yf_ref, yb_ref, sts_ref),
    ]
    while bodies:
        bodies = [body for body in bodies if next(body, "done") != "done"]


def _scans(pc, pf, gbias, alog, conv_w, conv_b, d_skip, n_ctx):
    bsz, u, _ = pf.shape
    nt = u // ROW_TILE
    nt_ctx = n_ctx // ROW_TILE
    hpt = ROW_TILE // HALO
    last_halo = u // HALO - 1
    const = lambda b, j: (0, 0)
    const3 = lambda b, j: (0, 0, 0)
    tile = [lambda b, j, d=d: _scan_tile(j, d, nt_ctx, nt) for d in range(2)]

    def tile_spec(direction, width, col):
        return pl.BlockSpec((None, ROW_TILE, width), lambda b, j: (b, tile[direction](b, j), col))

    def conv_specs(direction):
        t = tile[direction]
        return [
            tile_spec(direction, 768, 0),
            pl.BlockSpec((None, HALO, 768), lambda b, j: (b, jnp.maximum(t(b, j) * hpt - 1, 0), 0)),
            pl.BlockSpec((None, HALO, 768), lambda b, j: (b, jnp.minimum((t(b, j) + 1) * hpt, last_halo), 0)),
        ]

    out = jax.ShapeDtypeStruct((bsz, u, GROUP_WIDTH), F32)
    mt3 = _scan_masks()
    sel_m = _column_selector(1, 3)
    sel_s = _column_selector(1, 2)
    return pl.pallas_call(
        functools.partial(_scans_kernel, nt_ctx=nt_ctx, nt=nt),
        grid=(bsz, nt),
        in_specs=[tile_spec(0, 768, 0), tile_spec(1, 768, 0),
                  tile_spec(0, LANES, _GATE_BLK), tile_spec(1, LANES, _GATE_BLK)]
        + conv_specs(0) + conv_specs(1) + [
            pl.BlockSpec((1, LANES), const),
            pl.BlockSpec((2, 8, LANES), const3),
            pl.BlockSpec((3, 768), const),
            pl.BlockSpec((1, 768), const),
            pl.BlockSpec((1, GROUP_WIDTH), const),
            pl.BlockSpec(mt3.shape, const3),
            pl.BlockSpec(sel_m.shape, const),
            pl.BlockSpec(sel_s.shape, const),
        ],
        out_specs=[tile_spec(0, GROUP_WIDTH, 0), tile_spec(1, GROUP_WIDTH, 0),
                   tile_spec(0, GROUP_WIDTH, 0), tile_spec(1, GROUP_WIDTH, 0)],
        out_shape=[out, out, out, out],
        scratch_shapes=[pltpu.VMEM((2, 2, LANES, 2 * LANES), F32), pltpu.VMEM((2, 8, LANES), F32),
                        pltpu.VMEM((2, 2, D_STATE, LANES), F32),
                        pltpu.VMEM((u, 768), F32)],
        compiler_params=_params("parallel", "arbitrary"),
        name="scans",
    )(pc, pc, pf, pf, pf, pf, pf, pf, pf, pf, gbias, alog, conv_w, conv_b, d_skip, mt3, sel_m, sel_s)


def _out_ffn_kernel(xctx_ref, xlat_ref, ya_ref, yb_ref, hf_ref, hb_ref, o_ref, yf_ref, ybw_ref, z_ref,
                    gm_ref, gs_ref, mod_ref, g2_ref, wo_ref, w1_ref, w2_ref, gfin_ref, out_ref,
                    *, final, nt_ctx, q_off):
    if q_off >= nt_ctx:
        x_in = xlat_ref[...]
    else:
        x_in = jnp.where(pl.program_id(1) + q_off < nt_ctx, xctx_ref[...], xlat_ref[...])
    low = _lane_is_low((1, LANES))
    ym = []
    for pair in range(2):
        sl = slice(pair * LANES, (pair + 1) * LANES)
        h = hf_ref[:, sl] + hb_ref[:, sl]
        hn = h * lax.rsqrt(_head_mean_square(h, low) + EPS) * gm_ref[:, sl]
        ym.append((hn * jax.nn.sigmoid(o_ref[:, sl])).astype(MXU_DTYPE))
    ys = []
    for grp in range(2):
        sl = slice(grp * LANES, (grp + 1) * LANES)
        z = z_ref[:, sl]
        ys.append((yf_ref[:, sl] + ybw_ref[:, sl]) * (z * jax.nn.sigmoid(z)))
    ms = (jnp.sum(ys[0] * ys[0], axis=-1, keepdims=True)
          + jnp.sum(ys[1] * ys[1], axis=-1, keepdims=True)) * (1.0 / GROUP_WIDTH)
    rs = lax.rsqrt(ms + EPS)
    yd = [(ys[grp] * rs * gs_ref[:, grp * LANES:(grp + 1) * LANES]).astype(MXU_DTYPE) for grp in range(2)]

    y = (_dot(ya_ref[...], wo_ref[0:GROUP_WIDTH, :])
         + _dot(yb_ref[...], wo_ref[GROUP_WIDTH:2 * GROUP_WIDTH, :])
         + _dot(jnp.concatenate(ym, axis=1), wo_ref[2 * GROUP_WIDTH:3 * GROUP_WIDTH, :])
         + _dot(jnp.concatenate(yd, axis=1), wo_ref[3 * GROUP_WIDTH:, :]))
    x1 = x_in + mod_ref[2:3, :] * y
    h2 = _rmsnorm_mod(x1, g2_ref[...], mod_ref[3:4, :], mod_ref[4:5, :]).astype(MXU_DTYPE)
    acc = jnp.zeros(x1.shape, F32)
    for f in range(D_FF // FF_CHUNK):
        sl = slice(f * FF_CHUNK, (f + 1) * FF_CHUNK)
        hf = jnp.maximum(_dot(h2, w1_ref[:, sl]), 0.0)
        acc = acc + _dot((hf * hf).astype(MXU_DTYPE), w2_ref[sl, :])
    x2 = x1 + mod_ref[5:6, :] * acc
    if final:
        ms2 = jnp.mean(x2 * x2, axis=-1, keepdims=True)
        x2 = x2 * lax.rsqrt(ms2 + EPS) * gfin_ref[...]
    out_ref[...] = x2


def _out_ffn(x_ctx, x_lat, lat_off, ya, yb, hf, hb, yf, ybw, pf, gm, gs, modv, g2, wo, w1, w2, g_final,
             n_ctx, final):
    bsz, u, _ = pf.shape
    d = x_lat.shape[-1]
    nt_ctx = n_ctx // ROW_TILE
    q_off = nt_ctx if final else 0
    nt = u // ROW_TILE - q_off
    row = lambda b, i: (b, i + q_off, 0)
    const = lambda b, i: (0, 0)
    grp_tile = pl.BlockSpec((None, ROW_TILE, GROUP_WIDTH), row)
    attn_tile = pl.BlockSpec((None, ROW_TILE, GROUP_WIDTH), lambda b, i: (b, i, 0))
    return pl.pallas_call(
        functools.partial(_out_ffn_kernel, final=final, nt_ctx=nt_ctx, q_off=q_off),
        grid=(bsz, nt),
        in_specs=_residual_specs(nt_ctx, lat_off, q_off, d) + [
            attn_tile, attn_tile, grp_tile, grp_tile,
            pl.BlockSpec((None, ROW_TILE, GROUP_WIDTH), lambda b, i: (b, i + q_off, _O_BLK)),
            grp_tile, grp_tile,
            pl.BlockSpec((None, ROW_TILE, GROUP_WIDTH), lambda b, i: (b, i + q_off, _Z_BLK)),
            pl.BlockSpec((1, GROUP_WIDTH), const),
            pl.BlockSpec((1, GROUP_WIDTH), const),
            _mod_spec(nt_ctx, q_off, d),
            pl.BlockSpec((1, d), const),
            pl.BlockSpec((d, d), const, pipeline_mode=_RESIDENT),
            pl.BlockSpec((d, D_FF), const, pipeline_mode=_RESIDENT),
            pl.BlockSpec((D_FF, d), const, pipeline_mode=_RESIDENT),
            pl.BlockSpec((1, d), const),
        ],
        out_specs=pl.BlockSpec((None, ROW_TILE, d), lambda b, i: (b, i, 0)),
        out_shape=jax.ShapeDtypeStruct((bsz, nt * ROW_TILE, d), F32),
        compiler_params=_params("parallel", "parallel"),
        name="out_ffn_final" if final else "out_ffn",
    )(x_ctx, x_lat, ya, yb, hf, hb, pf, yf, ybw, pf, gm, gs, modv, g2, wo, w1, w2, g_final)


def _gate_starts():
    cg, ddt = _OFF[10], _OFF[15]
    return [cg, cg + N_HEADS, cg + 2 * N_HEADS, cg + 3 * N_HEADS, ddt, ddt + N_HEADS]


def _gate_lanes(groups):
    pad = jnp.zeros((8 - N_HEADS,), F32)
    parts = []
    for g in groups:
        parts += [g.astype(F32), pad]
    parts.append(jnp.zeros((LANES - 8 * len(groups),), F32))
    return jnp.concatenate(parts).reshape(1, LANES)


def _proj_weight(w_in):
    (aq, ak, av, bq, bk, bv, cq, ck, cv, co, cg, dx, dz, db, dc, ddt) = _OFF[:-1]
    qscale = HEAD_DIM ** -0.5
    segs = []
    for base, scale in ((aq, qscale * LOG2E), (bq, 1.0)):
        segs += [(base + h * HEAD_DIM, HEAD_DIM, scale) for h in _Q_HEAD_ORDER]
        segs += [(base + GROUP_WIDTH, 2 * LANES, 1.0)]
    segs += [(cq, GROUP_WIDTH, qscale), (ck, 2 * GROUP_WIDTH, 1.0)]
    segs += [(dx, GROUP_WIDTH, 1.0), (db, 2 * GROUP_WIDTH, 1.0), (co, GROUP_WIDTH, 1.0), (dz, GROUP_WIDTH, 1.0)]
    parts = [(w_in[:, s:s + n] if scale == 1.0 else w_in[:, s:s + n] * scale).astype(MXU_DTYPE)
             for s, n, scale in segs]
    zeros4 = jnp.zeros((w_in.shape[0], 8 - N_HEADS), MXU_DTYPE)
    for start in _gate_starts():
        parts += [w_in[:, start:start + N_HEADS].astype(MXU_DTYPE), zeros4]
    parts.append(jnp.zeros((w_in.shape[0], LANES - 8 * len(_gate_starts())), MXU_DTYPE))
    w = jnp.concatenate(parts, axis=1)
    assert w.shape[1] == _N_PROJ
    return w


def _out_weight(w_out):
    parts = []
    for base in (0, GROUP_WIDTH):
        parts += [w_out[base + h * HEAD_DIM:base + (h + 1) * HEAD_DIM].astype(MXU_DTYPE) for h in _Q_HEAD_ORDER]
    parts.append(w_out[2 * GROUP_WIDTH:].astype(MXU_DTYPE))
    return jnp.concatenate(parts, axis=0)


def _rope_tables(n_tok, n_ctx):
    rows = n_tok // GRID_W
    half = HEAD_DIM // 2
    n_freq = half // 2
    inv_freq = ROPE_BASE ** (-jnp.arange(0, half, 2, dtype=F32) / half)
    pos = jnp.arange(max(rows, GRID_W), dtype=F32)
    ang = pos[:, None] * inv_freq[None, :]
    cos_u = jnp.tile(jnp.cos(ang), (1, LANES // n_freq))
    sin_u = jnp.tile(jnp.sin(ang), (1, LANES // n_freq))
    lane = np.arange(LANES)
    hd = lane % HEAD_DIM
    use_col = ((hd // half) == 1)[None, :]
    first = ((hd % half) < n_freq)[None, :]

    def per_token(tab):
        by_row = jnp.repeat(tab[:rows], GRID_W, axis=0)
        by_col = jnp.tile(tab[:GRID_W], (rows, 1))
        return jnp.where(use_col, by_col, by_row)

    cos = per_token(cos_u)
    sin = per_token(sin_u)
    sin = jnp.where(first, -sin, sin)
    cos = jnp.concatenate([jnp.ones((n_ctx, LANES), F32), cos], axis=0)
    sin = jnp.concatenate([jnp.zeros((n_ctx, LANES), F32), sin], axis=0)
    return cos, sin


def kernel(x, c, ctx, c_ctx, w_ada, b_ada, g_norm1, g_norm2, w_in, sink_a, g_q_b, g_k_b, b_igate, b_fgate,
           g_mlstm, conv_w, conv_b, a_log, dt_bias, d_skip, g_ssm, w_out, w_ff1, w_ff2, g_final):
    bsz, n_tok, d = x.shape
    n_ctx = ctx.shape[1]
    depth = w_in.shape[0]
    assert d == D_MODEL and n_ctx % ROW_TILE == 0 and n_tok % ROW_TILE == 0

    cos, sin = _rope_tables(n_tok, n_ctx)
    n_rows = 16
    cvec = jnp.concatenate([c, c_ctx[None, :], jnp.zeros((n_rows - bsz - 1, d), F32)], axis=0)
    mod_all = _ada_mod(cvec, w_ada, b_ada)

    u = n_ctx + n_tok
    x_ctx, x_lat, lat_off = ctx, x, 0
    for layer in range(depth):
        need_ctx = layer < depth - 1
        mod = mod_all[layer].reshape(n_rows, 6, d)
        pad = jnp.zeros((bsz, 2, d), F32)
        mod_lat = jnp.concatenate([mod[:bsz], pad], axis=1)
        mod_ctx = jnp.broadcast_to(jnp.concatenate([mod[bsz], pad[0]], axis=0), (bsz, 8, d))
        modv = jnp.stack([mod_ctx, mod_lat], axis=1)

        gq = jnp.tile(g_q_b[layer] * (HEAD_DIM ** -0.5 * LOG2E), 2).reshape(1, LANES)
        gk = jnp.tile(g_k_b[layer], 2).reshape(1, LANES)
        pa, pb, pc, pf = _in_proj(x_ctx, x_lat, lat_off, u, modv, g_norm1[layer].reshape(1, d),
                                  _proj_weight(w_in[layer]), cos, sin, gq, gk, n_ctx)

        ya = _attn_a(pa, sink_a[layer], n_ctx, need_ctx)
        yb = _attn_b(pb, n_ctx, need_ctx)

        gbias = _gate_lanes([b_igate[layer, 0], b_fgate[layer, 0], b_igate[layer, 1], b_fgate[layer, 1],
                             dt_bias[layer, 0], dt_bias[layer, 1]])
        alog = jnp.broadcast_to(jnp.pad(a_log[layer], ((0, 0), (0, 8 - N_HEADS)))[:, :, None], (2, 8, LANES))
        dsk = jnp.repeat(d_skip[layer], HEAD_DIM).reshape(1, GROUP_WIDTH)
        hf, hb, yf, ybw = _scans(pc, pf, gbias, alog, conv_w[layer], conv_b[layer].reshape(1, -1), dsk, n_ctx)

        xc = _out_ffn(x_ctx, x_lat, lat_off, ya, yb, hf, hb, yf, ybw, pf, g_mlstm[layer].reshape(1, GROUP_WIDTH),
                      g_ssm[layer].reshape(1, GROUP_WIDTH), modv, g_norm2[layer].reshape(1, d),
                      _out_weight(w_out[layer]), w_ff1[layer].astype(MXU_DTYPE), w_ff2[layer].astype(MXU_DTYPE),
                      g_final.reshape(1, d), n_ctx, final=not need_ctx)
        x_ctx, x_lat, lat_off = xc, xc, n_ctx // ROW_TILE
    return xc
```

```python
import functools
import math

import numpy as np
import jax
import jax.numpy as jnp
from jax import lax
from jax.experimental import pallas as pl
from jax.experimental.pallas import tpu as pltpu

F32 = jnp.float32
MXU_DTYPE = jnp.bfloat16

D_MODEL = 1024
HEAD_DIM = 64
LANES = 128
GRID_W = 64
WINDOW = 128
ROPE_BASE = 10000.0
EPS = 1e-6
N_HEADS = 4
GROUP_WIDTH = N_HEADS * HEAD_DIM
D_STATE = 128
CHUNK = 128
ROW_TILE = 256
CHUNKS_PER_TILE = ROW_TILE // CHUNK
KEY_TILE = 1024
D_FF = 4 * D_MODEL
FF_CHUNK = 1024
HALO = 8
VMEM_LIMIT = 56 * 1024 * 1024
LOG2E = math.log2(math.e)
EXP_ARG_MAX = 88.0

_SPLIT_SIZES = (256, 128, 128, 256, 128, 128, 256, 256, 256, 256, 16, 256, 256, 256, 256, 8)
_OFF = [int(o) for o in np.concatenate([[0], np.cumsum(_SPLIT_SIZES)])]
_Q_HEAD_ORDER = (0, 2, 1, 3)

_COL_A = 0
_COL_B = 512
_COL_C = 1024
_COL_F = 1792
_N_F = 768 + 256 + 256 + 128
_N_PROJ = _COL_F + _N_F
_GATE_BLK = (_N_F - LANES) // LANES
_O_BLK = 768 // GROUP_WIDTH
_Z_BLK = (768 + GROUP_WIDTH) // GROUP_WIDTH


def _dot(a, b):
    return jnp.dot(a, b, preferred_element_type=F32)


def _dot_nt(a, b):
    return lax.dot_general(a, b, (((1,), (1,)), ((), ())), preferred_element_type=F32)


def _dot_tn(a, b):
    return lax.dot_general(a, b, (((0,), (0,)), ((), ())), preferred_element_type=F32)


def _split3(x):
    hi = x.astype(MXU_DTYPE)
    r1 = x - hi.astype(F32)
    mid = r1.astype(MXU_DTYPE)
    lo = (r1 - mid.astype(F32)).astype(MXU_DTYPE)
    return [hi, mid, lo]


def _rows_cumsum(rows, mt3):
    return _dot(jnp.concatenate(_split3(rows), axis=1), mt3)


def _rows_to_columns(rows, sel3):
    return _dot_tn(jnp.concatenate(_split3(rows), axis=0), sel3)


def _cummax_lanes(x, direction):
    n = x.shape[-1]
    lane = lax.broadcasted_iota(jnp.int32, x.shape, 1)

    def shifted(v, shift):
        if direction == 0:
            return jnp.where(lane >= shift, pltpu.roll(v, shift, 1), -jnp.inf)
        return jnp.where(lane < n - shift, pltpu.roll(v, n - shift, 1), -jnp.inf)

    def tree_max(vals):
        while len(vals) > 1:
            vals = [jnp.maximum(a, b) for a, b in zip(vals[::2], vals[1::2])] + ([vals[-1]] if len(vals) % 2 else [])
        return vals[0]

    window = tree_max([x] + [shifted(x, s) for s in range(1, 8)])
    return tree_max([window] + [shifted(window, s) for s in range(8, n, 8)])


def _lane_is_low(shape):
    lane = lax.broadcasted_iota(jnp.int32, shape, len(shape) - 1)
    return (lane % LANES) < HEAD_DIM


def _head_mean_square(t, low):
    sq = t * t
    lo = jnp.sum(jnp.where(low, sq, 0.0), axis=-1, keepdims=True)
    hi = jnp.sum(jnp.where(low, 0.0, sq), axis=-1, keepdims=True)
    return jnp.where(low, lo, hi) * (1.0 / HEAD_DIM)


def _params(*sem):
    return pltpu.CompilerParams(dimension_semantics=sem, vmem_limit_bytes=VMEM_LIMIT)


_RESIDENT = pl.Buffered(1)


def _ada_kernel(c_ref, w_ref, b_ref, out_ref):
    cv = c_ref[...]
    cv = cv * jax.nn.sigmoid(cv)
    out_ref[...] = _dot(cv.astype(MXU_DTYPE), w_ref[...].astype(MXU_DTYPE)) + b_ref[...]


def _ada_mod(cvec, w_ada, b_ada):
    depth, d, n = w_ada.shape
    rows = cvec.shape[0]
    tn = 1536
    return pl.pallas_call(
        _ada_kernel,
        grid=(depth, n // tn),
        in_specs=[
            pl.BlockSpec((rows, d), lambda l, j: (0, 0)),
            pl.BlockSpec((None, d, tn), lambda l, j: (l, 0, j)),
            pl.BlockSpec((None, 1, tn), lambda l, j: (l, 0, j)),
        ],
        out_specs=pl.BlockSpec((None, rows, tn), lambda l, j: (l, 0, j)),
        out_shape=jax.ShapeDtypeStruct((depth, rows, n), F32),
        compiler_params=_params("parallel", "parallel"),
        name="ada_mod",
    )(cvec, w_ada, b_ada.reshape(depth, 1, n))


def _rmsnorm_mod(x, g, shift, scale):
    ms = jnp.mean(x * x, axis=-1, keepdims=True)
    return (x * lax.rsqrt(ms + EPS) * g) * (1.0 + scale) + shift


def _in_proj_kernel(xctx_ref, xlat_ref, mod_ref, g1_ref, w_ref, cos_ref, sin_ref, gq_ref, gk_ref,
                    a_ref, b_ref, c_ref, f_ref, *, nt_ctx):
    x = jnp.where(pl.program_id(1) < nt_ctx, xctx_ref[...], xlat_ref[...])
    hn = _rmsnorm_mod(x, g1_ref[...], mod_ref[0:1, :], mod_ref[1:2, :])
    hb = hn.astype(MXU_DTYPE)
    cos = cos_ref[...]
    sin = sin_ref[...]
    lane = lax.broadcasted_iota(jnp.int32, (1, LANES), 1)
    first = (lane % 32) < 16
    low = _lane_is_low((1, LANES))

    def rope(t):
        partner = jnp.where(first, pltpu.roll(t, LANES - 16, 1), pltpu.roll(t, 16, 1))
        return t * cos + partner * sin

    pa = _dot(hb, w_ref[:, _COL_A:_COL_A + 512])
    for j in range(3):
        a_ref[:, j * LANES:(j + 1) * LANES] = rope(pa[:, j * LANES:(j + 1) * LANES]).astype(a_ref.dtype)
    a_ref[:, 3 * LANES:] = pa[:, 3 * LANES:].astype(a_ref.dtype)

    pb = _dot(hb, w_ref[:, _COL_B:_COL_B + 512])
    for j in range(3):
        t = pb[:, j * LANES:(j + 1) * LANES]
        g = gq_ref[...] if j < 2 else gk_ref[...]
        t = t * lax.rsqrt(_head_mean_square(t, low) + EPS) * g
        b_ref[:, j * LANES:(j + 1) * LANES] = rope(t).astype(b_ref.dtype)
    b_ref[:, 3 * LANES:] = pb[:, 3 * LANES:].astype(b_ref.dtype)

    c_ref[...] = _dot(hb, w_ref[:, _COL_C:_COL_F]).astype(c_ref.dtype)
    f_ref[...] = _dot(hb, w_ref[:, _COL_F:])


def _residual_specs(nt_ctx, lat_off, q_off, d):
    ctx_spec = pl.BlockSpec((None, ROW_TILE, d), lambda b, i: (b, jnp.minimum(i + q_off, nt_ctx - 1), 0))
    lat_spec = pl.BlockSpec((None, ROW_TILE, d),
                            lambda b, i: (b, jnp.maximum(i + q_off - nt_ctx, 0) + lat_off, 0))
    return [ctx_spec, lat_spec]


def _mod_spec(nt_ctx, q_off, d):
    return pl.BlockSpec((None, None, 8, d), lambda b, i: (b, jnp.where(i + q_off < nt_ctx, 0, 1), 0, 0))


def _in_proj(x_ctx, x_lat, lat_off, u, modv, g1, w, cos, sin, gq, gk, n_ctx):
    bsz, _, d = x_lat.shape
    nt = u // ROW_TILE
    nt_ctx = n_ctx // ROW_TILE
    row = lambda b, i: (b, i, 0)
    const = lambda b, i: (0, 0)
    return pl.pallas_call(
        functools.partial(_in_proj_kernel, nt_ctx=nt_ctx),
        grid=(bsz, nt),
        in_specs=_residual_specs(nt_ctx, lat_off, 0, d) + [
            _mod_spec(nt_ctx, 0, d),
            pl.BlockSpec((1, d), const),
            pl.BlockSpec((d, _N_PROJ), const, pipeline_mode=_RESIDENT),
            pl.BlockSpec((ROW_TILE, LANES), lambda b, i: (i, 0)),
            pl.BlockSpec((ROW_TILE, LANES), lambda b, i: (i, 0)),
            pl.BlockSpec((1, LANES), const),
            pl.BlockSpec((1, LANES), const),
        ],
        out_specs=[
            pl.BlockSpec((None, ROW_TILE, 512), row),
            pl.BlockSpec((None, ROW_TILE, 512), row),
            pl.BlockSpec((None, ROW_TILE, 768), row),
            pl.BlockSpec((None, ROW_TILE, _N_F), row),
        ],
        out_shape=[
            jax.ShapeDtypeStruct((bsz, u, 512), MXU_DTYPE),
            jax.ShapeDtypeStruct((bsz, u, 512), MXU_DTYPE),
            jax.ShapeDtypeStruct((bsz, u, 768), MXU_DTYPE),
            jax.ShapeDtypeStruct((bsz, u, _N_F), F32),
        ],
        compiler_params=_params("parallel", "parallel"),
        name="in_proj",
    )(x_ctx, x_lat, modv, g1, w, cos, sin, gq, gk)


def _stack_heads(q_ref, qs_ref):
    low = _lane_is_low((1, LANES))
    for jb in range(2):
        q = q_ref[:, jb * LANES:(jb + 1) * LANES]
        for half in range(2):
            idx = 2 * jb + half
            keep = low if half == 0 else jnp.logical_not(low)
            qs_ref[idx * ROW_TILE:(idx + 1) * ROW_TILE, :] = jnp.where(keep, q, jnp.zeros_like(q))


def _unstack_heads(o, out_ref):
    low = _lane_is_low((1, LANES))
    for jb in range(2):
        lo = o[(2 * jb) * ROW_TILE:(2 * jb + 1) * ROW_TILE, :]
        hi = o[(2 * jb + 1) * ROW_TILE:(2 * jb + 2) * ROW_TILE, :]
        out_ref[:, jb * LANES:(jb + 1) * LANES] = jnp.where(low, lo, hi).astype(out_ref.dtype)


def _attn_a_kernel(sink_ref, q_ref, kp_ref, ko_ref, kn_ref, kc_ref, vp_ref, vo_ref, vn_ref, vc_ref,
                   out_ref, qs_ref, kbuf, vbuf, s_ref, o_ref, *, n_tok, n_ctx, q_off):
    i = pl.program_id(1) + q_off
    nb = 2 * WINDOW + ROW_TILE
    nk = nb + n_ctx
    kbuf[0:WINDOW] = kp_ref[...]
    kbuf[WINDOW:WINDOW + ROW_TILE] = ko_ref[...]
    kbuf[WINDOW + ROW_TILE:nb] = kn_ref[...]
    kbuf[nb:nk] = kc_ref[...]
    vbuf[0:WINDOW, 0:LANES] = vp_ref[...]
    vbuf[WINDOW:WINDOW + ROW_TILE, 0:LANES] = vo_ref[...]
    vbuf[WINDOW + ROW_TILE:nb, 0:LANES] = vn_ref[...]
    vbuf[nb:nk, 0:LANES] = vc_ref[...]
    vbuf[:, LANES:] = jnp.ones((nk, LANES), vbuf.dtype)
    _stack_heads(q_ref, qs_ref)

    r = lax.broadcasted_iota(jnp.int32, (ROW_TILE, nk), 0)
    c = lax.broadcasted_iota(jnp.int32, (ROW_TILE, nk), 1)
    kpos = (i - 1) * ROW_TILE - WINDOW + c
    dist = c - r
    band = jnp.where(dist >= 0, jnp.where(dist <= 2 * WINDOW, 1, 0), 0)
    band = jnp.where(kpos >= 0, jnp.where(kpos < n_tok, band, 0), 0)
    band = jnp.where(i >= 1, band, 0)
    bias = jnp.where(c >= nb, 0.0, jnp.where(band > 0, 0.0, -jnp.inf)).astype(F32)

    for idx in range(N_HEADS):
        rows = slice(idx * ROW_TILE, (idx + 1) * ROW_TILE)
        s_ref[rows, :] = _dot_nt(qs_ref[rows, :], kbuf[...]) + bias
    for idx in range(N_HEADS):
        rows = slice(idx * ROW_TILE, (idx + 1) * ROW_TILE)
        sink = sink_ref[_Q_HEAD_ORDER[idx]] * LOG2E
        m = jnp.maximum(jnp.max(s_ref[rows, :], axis=-1, keepdims=True), sink)
        p = jnp.exp2(s_ref[rows, :] - m)
        pv = _dot(p.astype(MXU_DTYPE), vbuf[...])
        o_ref[rows, :] = pv[:, :LANES] / (pv[:, LANES:] + jnp.exp2(sink - m))
    _unstack_heads(o_ref[...], out_ref)


def _attn_a(pa, sink, n_ctx, need_ctx):
    bsz, u, _ = pa.shape
    n_tok = u - n_ctx
    q_off = 0 if need_ctx else n_ctx // ROW_TILE
    nt = u // ROW_TILE - q_off
    last_blk = u // WINDOW - 1
    rpw = ROW_TILE // WINDOW
    nk = 2 * WINDOW + ROW_TILE + n_ctx

    def own(col):
        return lambda b, i: (b, i + q_off, col)

    def prev(col):
        return lambda b, i: (b, jnp.maximum((i + q_off) * rpw - 1, 0), col)

    def nxt(col):
        return lambda b, i: (b, jnp.minimum((i + q_off + 1) * rpw, last_blk), col)

    def ctx(col):
        return lambda b, i: (b, 0, col)

    kern = functools.partial(_attn_a_kernel, n_tok=n_tok, n_ctx=n_ctx, q_off=q_off)
    return pl.pallas_call(
        kern,
        grid=(bsz, nt),
        in_specs=[
            pl.BlockSpec(memory_space=pltpu.SMEM),
            pl.BlockSpec((None, ROW_TILE, 2 * LANES), own(0)),
            pl.BlockSpec((None, WINDOW, LANES), prev(2)),
            pl.BlockSpec((None, ROW_TILE, LANES), own(2)),
            pl.BlockSpec((None, WINDOW, LANES), nxt(2)),
            pl.BlockSpec((None, n_ctx, LANES), ctx(2)),
            pl.BlockSpec((None, WINDOW, LANES), prev(3)),
            pl.BlockSpec((None, ROW_TILE, LANES), own(3)),
            pl.BlockSpec((None, WINDOW, LANES), nxt(3)),
            pl.BlockSpec((None, n_ctx, LANES), ctx(3)),
        ],
        out_specs=pl.BlockSpec((None, ROW_TILE, 2 * LANES), lambda b, i: (b, i, 0)),
        out_shape=jax.ShapeDtypeStruct((bsz, nt * ROW_TILE, 2 * LANES), MXU_DTYPE),
        scratch_shapes=[pltpu.VMEM((N_HEADS * ROW_TILE, LANES), MXU_DTYPE),
                        pltpu.VMEM((nk, LANES), MXU_DTYPE), pltpu.VMEM((nk, 2 * LANES), MXU_DTYPE),
                        pltpu.VMEM((N_HEADS * ROW_TILE, nk), F32),
                        pltpu.VMEM((N_HEADS * ROW_TILE, LANES), F32)],
        compiler_params=_params("parallel", "parallel"),
        name="attn_window",
    )(sink, pa, pa, pa, pa, pa, pa, pa, pa, pa)


def _attn_b_kernel(q_ref, k_ref, v_ref, out_ref, qs_ref, vext_ref, s_ref, p_ref, m_ref, alpha_ref, acc_ref,
                   *, n_ctx, n_tok, q_off, tk):
    qi = pl.program_id(1) + q_off
    n_rows = N_HEADS * ROW_TILE

    @pl.when(pl.program_id(1) == 0)
    def _():
        vext_ref[:, 0:LANES] = v_ref[...]
        vext_ref[:, LANES:] = jnp.ones((vext_ref.shape[0], LANES), vext_ref.dtype)

    _stack_heads(q_ref, qs_ref)

    def scores(slot, start, size):
        s_ref[slot, :, 0:size] = _dot_nt(qs_ref[...], k_ref[start:start + size, :])

    def softmax(slot, size, first):
        m_new = jnp.max(s_ref[slot, :, 0:size], axis=-1, keepdims=True)
        if not first:
            m_old = m_ref[...]
            m_new = jnp.maximum(m_old, m_new)
            alpha_ref[slot] = jnp.exp2(m_old - m_new)
        m_ref[...] = m_new
        p_ref[slot, :, 0:size] = jnp.exp2(s_ref[slot, :, 0:size] - m_new).astype(p_ref.dtype)

    def accumulate(slot, start, size, first):
        pv = _dot(p_ref[slot, :, 0:size], vext_ref[start:start + size, :])
        if first:
            acc_ref[...] = pv
        else:
            acc_ref[...] = alpha_ref[slot] * acc_ref[...] + pv

    def attend(tiles):
        scores(0, *tiles[0])
        for t, (start, size) in enumerate(tiles):
            if t + 1 < len(tiles):
                scores((t + 1) % 2, *tiles[t + 1])
            softmax(t % 2, size, t == 0)
            accumulate(t % 2, start, size, t == 0)
        _unstack_heads(acc_ref[:, 0:LANES] / acc_ref[:, LANES:], out_ref)

    ctx_tiles = [(0, n_ctx)]
    half = tk // 2
    all_tiles = (ctx_tiles + [(n_ctx, half)] + [(n_ctx + half + t * tk, tk) for t in range(n_tok // tk - 1)]
                 + [(n_ctx + n_tok - half, half)])
    if q_off == 0:
        pl.when(qi == 0)(functools.partial(attend, ctx_tiles))
        pl.when(qi > 0)(functools.partial(attend, all_tiles))
    else:
        attend(all_tiles)


def _attn_b(pb, n_ctx, need_ctx):
    bsz, u, _ = pb.shape
    n_tok = u - n_ctx
    q_off = 0 if need_ctx else n_ctx // ROW_TILE
    nt = u // ROW_TILE - q_off
    tk = min(KEY_TILE, n_tok)
    assert n_tok % tk == 0
    kern = functools.partial(_attn_b_kernel, n_ctx=n_ctx, n_tok=n_tok, q_off=q_off, tk=tk)
    return pl.pallas_call(
        kern,
        grid=(bsz, nt),
        in_specs=[
            pl.BlockSpec((None, ROW_TILE, 2 * LANES), lambda b, i: (b, i + q_off, 0)),
            pl.BlockSpec((None, u, LANES), lambda b, i: (b, 0, 2)),
            pl.BlockSpec((None, u, LANES), lambda b, i: (b, 0, 3)),
        ],
        out_specs=pl.BlockSpec((None, ROW_TILE, 2 * LANES), lambda b, i: (b, i, 0)),
        out_shape=jax.ShapeDtypeStruct((bsz, nt * ROW_TILE, 2 * LANES), MXU_DTYPE),
        scratch_shapes=[
            pltpu.VMEM((N_HEADS * ROW_TILE, LANES), MXU_DTYPE),
            pltpu.VMEM((u, 2 * LANES), MXU_DTYPE),
            pltpu.VMEM((2, N_HEADS * ROW_TILE, tk), F32),
            pltpu.VMEM((2, N_HEADS * ROW_TILE, tk), MXU_DTYPE),
            pltpu.VMEM((N_HEADS * ROW_TILE, 1), F32),
            pltpu.VMEM((2, N_HEADS * ROW_TILE, 1), F32),
            pltpu.VMEM((N_HEADS * ROW_TILE, 2 * LANES), F32),
        ],
        compiler_params=_params("parallel", "arbitrary"),
        name="attn_dense",
    )(pb, pb, pb)


def _scan_tile(j, direction, nt_ctx, nt):
    if direction == 0:
        return j
    return jnp.where(j < nt_ctx, nt_ctx - 1 - j, nt + nt_ctx - 1 - j)


def _causal_mask(direction):
    t = lax.broadcasted_iota(jnp.int32, (CHUNK, CHUNK), 0)
    s = lax.broadcasted_iota(jnp.int32, (CHUNK, CHUNK), 1)
    return (s <= t) if direction == 0 else (s >= t)


def _chunk_order(direction):
    order = range(CHUNKS_PER_TILE)
    return order if direction == 0 else reversed(order)


def _log_sigmoid(x):
    return jnp.minimum(x, 0.0) - jnp.log1p(jnp.exp(-jnp.abs(x)))


def _softplus(x):
    return jnp.maximum(x, 0.0) + jnp.log1p(jnp.exp(-jnp.abs(x)))


def _mlstm_gates(direction, g_rows, m_prev, mt3):
    end = CHUNK - 1 if direction == 0 else 0
    i_r = jnp.concatenate([g[16 * direction:16 * direction + 8] for g in g_rows], axis=0)
    f_r = jnp.concatenate([g[16 * direction + 8:16 * direction + 16] for g in g_rows], axis=0)
    b_r = _rows_cumsum(_log_sigmoid(f_r), mt3)
    r_r = i_r - b_r
    cmax = _cummax_lanes(r_r, direction)
    tot = jnp.broadcast_to(b_r[:, end:end + 1], b_r.shape)
    g2 = tot - b_r + i_r
    g2max = jnp.broadcast_to(jnp.max(g2, axis=-1, keepdims=True), g2.shape)
    per_chunk = []
    for idx in range(len(g_rows)):
        sl = slice(8 * idx, 8 * idx + 8)
        c_r = jnp.maximum(m_prev, cmax[sl])
        dp_r = jnp.exp(m_prev - c_r)
        em_r = jnp.exp(jnp.minimum(-(b_r[sl] + c_r), EXP_ARG_MAX))
        m_new = jnp.maximum(tot[sl] + m_prev, g2max[sl])
        wk_r = jnp.exp(g2[sl] - m_new)
        cd_r = jnp.exp(tot[sl] + m_prev - m_new)
        per_chunk.append((jnp.concatenate([c_r * LOG2E, dp_r, em_r, wk_r], axis=0), r_r[sl] * LOG2E, cd_r))
        m_prev = m_new
    return per_chunk, m_prev


def _mlstm_scores(qkvs):
    low1 = _lane_is_low((1, LANES))
    ones = jnp.ones((CHUNK, LANES), MXU_DTYPE)
    pre = []
    for qkv in qkvs:
        per_pair = []
        for pair in range(2):
            q = qkv[:, pair * LANES:(pair + 1) * LANES]
            k = qkv[:, GROUP_WIDTH + pair * LANES:GROUP_WIDTH + (pair + 1) * LANES]
            v = qkv[:, 2 * GROUP_WIDTH + pair * LANES:2 * GROUP_WIDTH + (pair + 1) * LANES]
            qk = [_dot_nt(jnp.where(low1 if half == 0 else jnp.logical_not(low1), q, jnp.zeros_like(q)), k)
                  for half in range(2)]
            per_pair.append((q, k, jnp.concatenate([v, ones], axis=1), qk))
        pre.append(per_pair)
    return pre


def _mlstm_intra(items, pre):
    low1 = _lane_is_low((1, LANES))
    r2 = lax.broadcasted_iota(jnp.int32, (LANES, 2 * LANES), 0)
    c2 = lax.broadcasted_iota(jnp.int32, (LANES, 2 * LANES), 1)
    state_mask = (r2 < HEAD_DIM) == ((c2 % LANES) < HEAD_DIM)
    mid = []
    for (direction, qkv, r_r, cd_r, cols), per_pair in zip(items, pre):
        mask = _causal_mask(direction)
        res = []
        for pair, (q, k, vext, qk) in enumerate(per_pair):
            svs, dens = [], []
            for half in range(2):
                h = 2 * pair + half
                w = jnp.exp2(jnp.where(mask, r_r[h:h + 1, :] - cols[:, h * LANES:(h + 1) * LANES], -jnp.inf))
                s = qk[half] * w
                svs.append(_dot(s.astype(MXU_DTYPE), vext[:, :LANES]))
                dens.append(jnp.sum(s, axis=-1, keepdims=True))
            wk = cols[:, (8 + pair) * LANES:(9 + pair) * LANES]
            upd = _dot_tn((k.astype(F32) * wk).astype(MXU_DTYPE), vext)
            res.append((jnp.where(low1, svs[0], svs[1]), jnp.where(low1, dens[0], dens[1]),
                        jnp.where(state_mask, upd, 0.0)))
        mid.append(res)
    return mid


def _mlstm_recurrence(items, pre, mid, state):
    low2 = _lane_is_low((1, 2 * LANES))
    outs = []
    for (direction, qkv, r_r, cd_r, cols), per_pair, res in zip(items, pre, mid):
        out = []
        for pair in range(2):
            q = per_pair[pair][0]
            sv, den_intra, upd = res[pair]
            prior = _dot(q, state[direction][pair].astype(MXU_DTYPE))
            dp = cols[:, (4 + pair) * LANES:(5 + pair) * LANES]
            em = cols[:, (6 + pair) * LANES:(7 + pair) * LANES]
            num = sv + dp * prior[:, :LANES]
            den = den_intra + dp * prior[:, LANES:]
            out.append(num / jnp.maximum(jnp.abs(den), em))
            cd0 = jnp.concatenate([cd_r[2 * pair:2 * pair + 1]] * 2, axis=1)
            cd1 = jnp.concatenate([cd_r[2 * pair + 1:2 * pair + 2]] * 2, axis=1)
            state[direction][pair] = state[direction][pair] * jnp.where(low2, cd0, cd1) + upd
        outs.append(out)
    return outs


def _gate_rows(g_refs, gbias_ref):
    gbias = gbias_ref[...]
    return [[(g_ref[ci * CHUNK:(ci + 1) * CHUNK, :] + gbias).T for ci in _chunk_order(direction)]
            for direction, g_ref in enumerate(g_refs)]


def _mlstm_body(g_rows, qkvf_ref, qkvb_ref, mt3_ref, sel3_ref, hf_ref, hb_ref, st_ref, m_ref):
    dirs = ((qkvf_ref, None, hf_ref), (qkvb_ref, None, hb_ref))
    work = [(direction, idx) for idx in range(CHUNKS_PER_TILE) for direction in range(2)]
    row_slices = []
    for direction, idx in work:
        ci = list(_chunk_order(direction))[idx]
        row_slices.append(slice(ci * CHUNK, (ci + 1) * CHUNK))
    gates = []
    for direction in range(2):
        per_chunk, m_new = _mlstm_gates(direction, g_rows[direction], m_ref[direction], mt3_ref[direction])
        m_ref[direction] = m_new
        gates.append(per_chunk)
    yield
    rows_all = jnp.concatenate([gates[direction][idx][0] for direction, idx in work], axis=1)
    cols_all = _rows_to_columns(rows_all, sel3_ref[...])
    qkvs = [dirs[direction][0][rows, :] for (direction, _), rows in zip(work, row_slices)]
    pre = _mlstm_scores(qkvs)
    yield
    state = [[st_ref[direction, pair] for pair in range(2)] for direction in range(2)]
    items = []
    for n, (direction, idx) in enumerate(work):
        _, r_r, cd_r = gates[direction][idx]
        items.append((direction, qkvs[n], r_r, cd_r, cols_all[n * CHUNK:(n + 1) * CHUNK, :]))
    mid = _mlstm_intra(items, pre)
    yield
    outs = _mlstm_recurrence(items, pre, mid, state)
    for (direction, *_), rows, out in zip(items, row_slices, outs):
        for pair in range(2):
            dirs[direction][2][rows, pair * LANES:(pair + 1) * LANES] = out[pair]
    for direction in range(2):
        for pair in range(2):
            st_ref[direction, pair] = state[direction][pair]


def _scan_masks():
    s = np.arange(CHUNK)[:, None]
    t = np.arange(CHUNK)[None, :]
    mats = [np.tile((s <= t).astype(np.float32), (3, 1)), np.tile((s >= t).astype(np.float32), (3, 1))]
    return jnp.asarray(np.stack(mats), MXU_DTYPE)


def _column_selector(n_head_groups, n_pair_groups):
    n_groups = n_head_groups + n_pair_groups
    n_rows = 8 * n_groups + (-8 * n_groups) % 16
    n_cols = (N_HEADS * n_head_groups + 2 * n_pair_groups) * LANES
    sel = np.zeros((n_rows, n_cols), np.float32)
    col = 0
    for g in range(n_head_groups):
        for h in range(N_HEADS):
            sel[8 * g + h, col:col + LANES] = 1.0
            col += LANES
    for g in range(n_head_groups, n_groups):
        for pair in range(2):
            sel[8 * g + 2 * pair, col:col + HEAD_DIM] = 1.0
            sel[8 * g + 2 * pair + 1, col + HEAD_DIM:col + LANES] = 1.0
            col += LANES
    return jnp.asarray(np.tile(sel, (3, 1)), MXU_DTYPE)


def _ssd_conv(x_ref, xp_ref, xn_ref, cw_ref, cb_ref, tile, nt_ctx, nt):
    xin = x_ref[...]
    has_prev = (tile != 0) & (tile != nt_ctx)
    has_next = (tile != nt_ctx - 1) & (tile != nt - 1)
    prow = jnp.where(has_prev, xp_ref[HALO - 1:HALO, :], 0.0)
    nrow = jnp.where(has_next, xn_ref[0:1, :], 0.0)
    ridx = lax.broadcasted_iota(jnp.int32, (ROW_TILE, 1), 0)
    up = jnp.where(ridx == 0, prow, pltpu.roll(xin, 1, 0))
    dn = jnp.where(ridx == ROW_TILE - 1, nrow, pltpu.roll(xin, ROW_TILE - 1, 0))
    u = cw_ref[0:1, :] * up + cw_ref[1:2, :] * xin + cw_ref[2:3, :] * dn + cb_ref[...]
    return u * jax.nn.sigmoid(u)


def _ssd_gates(direction, g_rows, neg_a, mt3):
    end = CHUNK - 1 if direction == 0 else 0
    n = len(g_rows)
    dt_r = _softplus(jnp.concatenate([g[32 + 8 * direction:40 + 8 * direction] for g in g_rows], axis=0))
    cum_r = _rows_cumsum(dt_r * jnp.concatenate([neg_a] * n, axis=0), mt3)
    tot = jnp.broadcast_to(cum_r[:, end:end + 1], cum_r.shape)
    e_r = jnp.exp(cum_r)
    wend_r = jnp.exp(tot - cum_r) * dt_r
    d_r = jnp.exp(tot)
    zero = jnp.zeros((8, CHUNK), F32)
    per_chunk = []
    for idx in range(n):
        sl = slice(8 * idx, 8 * idx + 8)
        per_chunk.append((jnp.concatenate([cum_r[sl] * LOG2E, e_r[sl], wend_r[sl], zero], axis=0),
                          cum_r[sl] * LOG2E, dt_r[sl], d_r[sl]))
    return per_chunk


def _ssd_scores(us):
    pre = []
    for u in us:
        per_grp = []
        for grp in range(2):
            bm = u[:, GROUP_WIDTH + grp * D_STATE:GROUP_WIDTH + (grp + 1) * D_STATE].astype(MXU_DTYPE)
            cm = u[:, 2 * GROUP_WIDTH + grp * D_STATE:2 * GROUP_WIDTH + (grp + 1) * D_STATE].astype(MXU_DTYPE)
            x_pair = u[:, grp * LANES:(grp + 1) * LANES]
            per_grp.append((bm, cm, x_pair, _dot_nt(cm, bm)))
        pre.append(per_grp)
    return pre


def _ssd_intra(items, pre):
    low1 = _lane_is_low((1, LANES))
    mid = []
    for (direction, u, cum_r, dt_r, d_r, cols), per_grp in zip(items, pre):
        mask = _causal_mask(direction)
        res = []
        for grp, (bm, cm, x_pair, gmat) in enumerate(per_grp):
            xb = x_pair.astype(MXU_DTYPE)
            ys = []
            for half in range(2):
                h = 2 * grp + half
                decay = jnp.exp2(jnp.where(mask, cols[:, h * LANES:(h + 1) * LANES] - cum_r[h:h + 1, :], -jnp.inf))
                s = gmat * decay * dt_r[h:h + 1, :]
                ys.append(_dot(s.astype(MXU_DTYPE), xb))
            w_pair = cols[:, (6 + grp) * LANES:(7 + grp) * LANES]
            upd = _dot_tn(bm, (x_pair * w_pair).astype(MXU_DTYPE))
            res.append((jnp.where(low1, ys[0], ys[1]), upd))
        mid.append(res)
    return mid


def _ssd_recurrence(items, pre, mid, state):
    low1 = _lane_is_low((1, LANES))
    outs = []
    for (direction, u, cum_r, dt_r, d_r, cols), per_grp, res in zip(items, pre, mid):
        out = []
        for grp in range(2):
            cm = per_grp[grp][1]
            y_intra, upd = res[grp]
            ch = _dot(cm, state[direction][grp].astype(MXU_DTYPE))
            e_pair = cols[:, (4 + grp) * LANES:(5 + grp) * LANES]
            out.append(y_intra + e_pair * ch)
            d_pair = jnp.where(low1, d_r[2 * grp:2 * grp + 1], d_r[2 * grp + 1:2 * grp + 2])
            state[direction][grp] = state[direction][grp] * d_pair + upd
        outs.append(out)
    return outs


def _ssd_convs(xf_ref, xfp_ref, xfn_ref, xb_ref, xbp_ref, xbn_ref, cw_ref, cb_ref, u_ref, *, nt_ctx, nt):
    j = pl.program_id(1)
    tile_f = _scan_tile(j, 0, nt_ctx, nt)
    tile_b = _scan_tile(j, 1, nt_ctx, nt)
    step_b_visits_f = jnp.where(tile_f < nt_ctx, nt_ctx - 1 - tile_f, nt + nt_ctx - 1 - tile_f)
    rows_f = pl.ds(pl.multiple_of(tile_f * ROW_TILE, ROW_TILE), ROW_TILE)
    rows_b = pl.ds(pl.multiple_of(tile_b * ROW_TILE, ROW_TILE), ROW_TILE)

    @pl.when(step_b_visits_f >= j)
    def _():
        u_ref[rows_f, :] = _ssd_conv(xf_ref, xfp_ref, xfn_ref, cw_ref, cb_ref, tile_f, nt_ctx, nt)

    @pl.when(tile_b > j)
    def _():
        u_ref[rows_b, :] = _ssd_conv(xb_ref, xbp_ref, xbn_ref, cw_ref, cb_ref, tile_b, nt_ctx, nt)

    return [u_ref[rows_f, :], u_ref[rows_b, :]]


def _ssd_body(g_rows, convs, alog_ref, dskip_ref, mt3_ref, sel3_ref, yf_ref, yb_ref, st_ref):
    dirs = ((None, None, None, None, yf_ref), (None, None, None, None, yb_ref))
    gates = []
    for direction in range(2):
        gates.append(_ssd_gates(direction, g_rows[direction], -jnp.exp(alog_ref[direction]), mt3_ref[direction]))
    work = [(direction, idx) for idx in range(CHUNKS_PER_TILE) for direction in range(2)]
    yield
    rows_all = jnp.concatenate([gates[direction][idx][0] for direction, idx in work], axis=1)
    cols_all = _rows_to_columns(rows_all, sel3_ref[...])
    items, row_slices = [], []
    for n, (direction, idx) in enumerate(work):
        ci = list(_chunk_order(direction))[idx]
        rows = slice(ci * CHUNK, (ci + 1) * CHUNK)
        _, cum_r, dt_r, d_r = gates[direction][idx]
        items.append((direction, convs[direction][rows, :], cum_r, dt_r, d_r,
                      cols_all[n * CHUNK:(n + 1) * CHUNK, :]))
        row_slices.append(rows)
    pre = _ssd_scores([item[1] for item in items])
    yield
    mid = _ssd_intra(items, pre)
    yield
    state = [[st_ref[direction, grp] for grp in range(2)] for direction in range(2)]
    outs = _ssd_recurrence(items, pre, mid, state)
    for (direction, u, *_), rows, out in zip(items, row_slices, outs):
        for grp in range(2):
            sl = slice(grp * LANES, (grp + 1) * LANES)
            y = out[grp]
            if direction == 0:
                y = y + dskip_ref[:, sl] * u[:, sl]
            dirs[direction][4][rows, sl] = y
    for direction in range(2):
        for grp in range(2):
            st_ref[direction, grp] = state[direction][grp]


def _scans_kernel(qkvf_ref, qkvb_ref, gf_ref, gb_ref, xf_ref, xfp_ref, xfn_ref, xb_ref, xbp_ref, xbn_ref,
                  gbias_ref, alog_ref, cw_ref, cb_ref, dskip_ref, mt3_ref, selm_ref, sels_ref,
                  hf_ref, hb_ref, yf_ref, yb_ref, stm_ref, m_ref, sts_ref, u_ref, *, nt_ctx, nt):
    @pl.when(pl.program_id(1) == 0)
    def _():
        stm_ref[...] = jnp.zeros(stm_ref.shape, F32)
        m_ref[...] = jnp.zeros(m_ref.shape, F32)
        sts_ref[...] = jnp.zeros(sts_ref.shape, F32)

    convs = _ssd_convs(xf_ref, xfp_ref, xfn_ref, xb_ref, xbp_ref, xbn_ref, cw_ref, cb_ref, u_ref,
                       nt_ctx=nt_ctx, nt=nt)
    g_rows = _gate_rows((gf_ref, gb_ref), gbias_ref)
    bodies = [
        _mlstm_body(g_rows, qkvf_ref, qkvb_ref, mt3_ref, selm_ref, hf_ref, hb_ref, stm_ref, m_ref),
        _ssd_body(g_rows, convs, alog_ref, dskip_ref, mt3_ref, sels_ref, yf_ref, yb_ref, sts_ref),
    ]
    while bodies:
        bodies = [body for body in bodies if next(body, "done") != "done"]


def _scans(pc, pf, gbias, alog, conv_w, conv_b, d_skip, n_ctx):
    bsz, u, _ = pf.shape
    nt = u // ROW_TILE
    nt_ctx = n_ctx // ROW_TILE
    hpt = ROW_TILE // HALO
    last_halo = u // HALO - 1
    const = lambda b, j: (0, 0)
    const3 = lambda b, j: (0, 0, 0)
    tile = [lambda b, j, d=d: _scan_tile(j, d, nt_ctx, nt) for d in range(2)]

    def tile_spec(direction, width, col):
        return pl.BlockSpec((None, ROW_TILE, width), lambda b, j: (b, tile[direction](b, j), col))

    def conv_specs(direction):
        t = tile[direction]
        return [
            tile_spec(direction, 768, 0),
            pl.BlockSpec((None, HALO, 768), lambda b, j: (b, jnp.maximum(t(b, j) * hpt - 1, 0), 0)),
            pl.BlockSpec((None, HALO, 768), lambda b, j: (b, jnp.minimum((t(b, j) + 1) * hpt, last_halo), 0)),
        ]

    out = jax.ShapeDtypeStruct((bsz, u, GROUP_WIDTH), F32)
    mt3 = _scan_masks()
    sel_m = _column_selector(1, 3)
    sel_s = _column_selector(1, 2)
    return pl.pallas_call(
        functools.partial(_scans_kernel, nt_ctx=nt_ctx, nt=nt),
        grid=(bsz, nt),
        in_specs=[tile_spec(0, 768, 0), tile_spec(1, 768, 0),
                  tile_spec(0, LANES, _GATE_BLK), tile_spec(1, LANES, _GATE_BLK)]
        + conv_specs(0) + conv_specs(1) + [
            pl.BlockSpec((1, LANES), const),
            pl.BlockSpec((2, 8, LANES), const3),
            pl.BlockSpec((3, 768), const),
            pl.BlockSpec((1, 768), const),
            pl.BlockSpec((1, GROUP_WIDTH), const),
            pl.BlockSpec(mt3.shape, const3),
            pl.BlockSpec(sel_m.shape, const),
            pl.BlockSpec(sel_s.shape, const),
        ],
        out_specs=[tile_spec(0, GROUP_WIDTH, 0), tile_spec(1, GROUP_WIDTH, 0),
                   tile_spec(0, GROUP_WIDTH, 0), tile_spec(1, GROUP_WIDTH, 0)],
        out_shape=[out, out, out, out],
        scratch_shapes=[pltpu.VMEM((2, 2, LANES, 2 * LANES), F32), pltpu.VMEM((2, 8, LANES), F32),
                        pltpu.VMEM((2, 2, D_STATE, LANES), F32),
                        pltpu.VMEM((u, 768), F32)],
        compiler_params=_params("parallel", "arbitrary"),
        name="scans",
    )(pc, pc, pf, pf, pf, pf, pf, pf, pf, pf, gbias, alog, conv_w, conv_b, d_skip, mt3, sel_m, sel_s)


def _out_ffn_kernel(xctx_ref, xlat_ref, ya_ref, yb_ref, hf_ref, hb_ref, o_ref, yf_ref, ybw_ref, z_ref,
                    gm_ref, gs_ref, mod_ref, g2_ref, wo_ref, w1_ref, w2_ref, gfin_ref, out_ref,
                    *, final, nt_ctx, q_off):
    if q_off >= nt_ctx:
        x_in = xlat_ref[...]
    else:
        x_in = jnp.where(pl.program_id(1) + q_off < nt_ctx, xctx_ref[...], xlat_ref[...])
    low = _lane_is_low((1, LANES))
    ym = []
    for pair in range(2):
        sl = slice(pair * LANES, (pair + 1) * LANES)
        h = hf_ref[:, sl] + hb_ref[:, sl]
        hn = h * lax.rsqrt(_head_mean_square(h, low) + EPS) * gm_ref[:, sl]
        ym.append((hn * jax.nn.sigmoid(o_ref[:, sl])).astype(MXU_DTYPE))
    ys = []
    for grp in range(2):
        sl = slice(grp * LANES, (grp + 1) * LANES)
        z = z_ref[:, sl]
        ys.append((yf_ref[:, sl] + ybw_ref[:, sl]) * (z * jax.nn.sigmoid(z)))
    ms = (jnp.sum(ys[0] * ys[0], axis=-1, keepdims=True)
          + jnp.sum(ys[1] * ys[1], axis=-1, keepdims=True)) * (1.0 / GROUP_WIDTH)
    rs = lax.rsqrt(ms + EPS)
    yd = [(ys[grp] * rs * gs_ref[:, grp * LANES:(grp + 1) * LANES]).astype(MXU_DTYPE) for grp in range(2)]

    y = (_dot(ya_ref[...], wo_ref[0:GROUP_WIDTH, :])
         + _dot(yb_ref[...], wo_ref[GROUP_WIDTH:2 * GROUP_WIDTH, :])
         + _dot(jnp.concatenate(ym, axis=1), wo_ref[2 * GROUP_WIDTH:3 * GROUP_WIDTH, :])
         + _dot(jnp.concatenate(yd, axis=1), wo_ref[3 * GROUP_WIDTH:, :]))
    x1 = x_in + mod_ref[2:3, :] * y
    h2 = _rmsnorm_mod(x1, g2_ref[...], mod_ref[3:4, :], mod_ref[4:5, :]).astype(MXU_DTYPE)
    acc = jnp.zeros(x1.shape, F32)
    for f in range(D_FF // FF_CHUNK):
        sl = slice(f * FF_CHUNK, (f + 1) * FF_CHUNK)
        hf = jnp.maximum(_dot(h2, w1_ref[:, sl]), 0.0)
        acc = acc + _dot((hf * hf).astype(MXU_DTYPE), w2_ref[sl, :])
    x2 = x1 + mod_ref[5:6, :] * acc
    if final:
        ms2 = jnp.mean(x2 * x2, axis=-1, keepdims=True)
        x2 = x2 * lax.rsqrt(ms2 + EPS) * gfin_ref[...]
    out_ref[...] = x2


def _out_ffn(x_ctx, x_lat, lat_off, ya, yb, hf, hb, yf, ybw, pf, gm, gs, modv, g2, wo, w1, w2, g_final,
             n_ctx, final):
    bsz, u, _ = pf.shape
    d = x_lat.shape[-1]
    nt_ctx = n_ctx // ROW_TILE
    q_off = nt_ctx if final else 0
    nt = u // ROW_TILE - q_off
    row = lambda b, i: (b, i + q_off, 0)
    const = lambda b, i: (0, 0)
    grp_tile = pl.BlockSpec((None, ROW_TILE, GROUP_WIDTH), row)
    attn_tile = pl.BlockSpec((None, ROW_TILE, GROUP_WIDTH), lambda b, i: (b, i, 0))
    return pl.pallas_call(
        functools.partial(_out_ffn_kernel, final=final, nt_ctx=nt_ctx, q_off=q_off),
        grid=(bsz, nt),
        in_specs=_residual_specs(nt_ctx, lat_off, q_off, d) + [
            attn_tile, attn_tile, grp_tile, grp_tile,
            pl.BlockSpec((None, ROW_TILE, GROUP_WIDTH), lambda b, i: (b, i + q_off, _O_BLK)),
            grp_tile, grp_tile,
            pl.BlockSpec((None, ROW_TILE, GROUP_WIDTH), lambda b, i: (b, i + q_off, _Z_BLK)),
            pl.BlockSpec((1, GROUP_WIDTH), const),
            pl.BlockSpec((1, GROUP_WIDTH), const),
            _mod_spec(nt_ctx, q_off, d),
            pl.BlockSpec((1, d), const),
            pl.BlockSpec((d, d), const, pipeline_mode=_RESIDENT),
            pl.BlockSpec((d, D_FF), const, pipeline_mode=_RESIDENT),
            pl.BlockSpec((D_FF, d), const, pipeline_mode=_RESIDENT),
            pl.BlockSpec((1, d), const),
        ],
        out_specs=pl.BlockSpec((None, ROW_TILE, d), lambda b, i: (b, i, 0)),
        out_shape=jax.ShapeDtypeStruct((bsz, nt * ROW_TILE, d), F32),
        compiler_params=_params("parallel", "parallel"),
        name="out_ffn_final" if final else "out_ffn",
    )(x_ctx, x_lat, ya, yb, hf, hb, pf, yf, ybw, pf, gm, gs, modv, g2, wo, w1, w2, g_final)


def _gate_starts():
    cg, ddt = _OFF[10], _OFF[15]
    return [cg, cg + N_HEADS, cg + 2 * N_HEADS, cg + 3 * N_HEADS, ddt, ddt + N_HEADS]


def _gate_lanes(groups):
    pad = jnp.zeros((8 - N_HEADS,), F32)
    parts = []
    for g in groups:
        parts += [g.astype(F32), pad]
    parts.append(jnp.zeros((LANES - 8 * len(groups),), F32))
    return jnp.concatenate(parts).reshape(1, LANES)


def _proj_weight(w_in):
    (aq, ak, av, bq, bk, bv, cq, ck, cv, co, cg, dx, dz, db, dc, ddt) = _OFF[:-1]
    qscale = HEAD_DIM ** -0.5
    segs = []
    for base, scale in ((aq, qscale * LOG2E), (bq, 1.0)):
        segs += [(base + h * HEAD_DIM, HEAD_DIM, scale) for h in _Q_HEAD_ORDER]
        segs += [(base + GROUP_WIDTH, 2 * LANES, 1.0)]
    segs += [(cq, GROUP_WIDTH, qscale), (ck, 2 * GROUP_WIDTH, 1.0)]
    segs += [(dx, GROUP_WIDTH, 1.0), (db, 2 * GROUP_WIDTH, 1.0), (co, GROUP_WIDTH, 1.0), (dz, GROUP_WIDTH, 1.0)]
    parts = [(w_in[:, s:s + n] if scale == 1.0 else w_in[:, s:s + n] * scale).astype(MXU_DTYPE)
             for s, n, scale in segs]
    zeros4 = jnp.zeros((w_in.shape[0], 8 - N_HEADS), MXU_DTYPE)
    for start in _gate_starts():
        parts += [w_in[:, start:start + N_HEADS].astype(MXU_DTYPE), zeros4]
    parts.append(jnp.zeros((w_in.shape[0], LANES - 8 * len(_gate_starts())), MXU_DTYPE))
    w = jnp.concatenate(parts, axis=1)
    assert w.shape[1] == _N_PROJ
    return w


def _out_weight(w_out):
    parts = []
    for base in (0, GROUP_WIDTH):
        parts += [w_out[base + h * HEAD_DIM:base + (h + 1) * HEAD_DIM].astype(MXU_DTYPE) for h in _Q_HEAD_ORDER]
    parts.append(w_out[2 * GROUP_WIDTH:].astype(MXU_DTYPE))
    return jnp.concatenate(parts, axis=0)


def _rope_tables(n_tok, n_ctx):
    rows = n_tok // GRID_W
    half = HEAD_DIM // 2
    n_freq = half // 2
    inv_freq = ROPE_BASE ** (-jnp.arange(0, half, 2, dtype=F32) / half)
    pos = jnp.arange(max(rows, GRID_W), dtype=F32)
    ang = pos[:, None] * inv_freq[None, :]
    cos_u = jnp.tile(jnp.cos(ang), (1, LANES // n_freq))
    sin_u = jnp.tile(jnp.sin(ang), (1, LANES // n_freq))
    lane = np.arange(LANES)
    hd = lane % HEAD_DIM
    use_col = ((hd // half) == 1)[None, :]
    first = ((hd % half) < n_freq)[None, :]

    def per_token(tab):
        by_row = jnp.repeat(tab[:rows], GRID_W, axis=0)
        by_col = jnp.tile(tab[:GRID_W], (rows, 1))
        return jnp.where(use_col, by_col, by_row)

    cos = per_token(cos_u)
    sin = per_token(sin_u)
    sin = jnp.where(first, -sin, sin)
    cos = jnp.concatenate([jnp.ones((n_ctx, LANES), F32), cos], axis=0)
    sin = jnp.concatenate([jnp.zeros((n_ctx, LANES), F32), sin], axis=0)
    return cos, sin


def kernel(x, c, ctx, c_ctx, w_ada, b_ada, g_norm1, g_norm2, w_in, sink_a, g_q_b, g_k_b, b_igate, b_fgate,
           g_mlstm, conv_w, conv_b, a_log, dt_bias, d_skip, g_ssm, w_out, w_ff1, w_ff2, g_final):
    bsz, n_tok, d = x.shape
    n_ctx = ctx.shape[1]
    depth = w_in.shape[0]
    assert d == D_MODEL and n_ctx % ROW_TILE == 0 and n_tok % ROW_TILE == 0

    cos, sin = _rope_tables(n_tok, n_ctx)
    n_rows = 16
    cvec = jnp.concatenate([c, c_ctx[None, :], jnp.zeros((n_rows - bsz - 1, d), F32)], axis=0)
    mod_all = _ada_mod(cvec, w_ada, b_ada)

    u = n_ctx + n_tok
    x_ctx, x_lat, lat_off = ctx, x, 0
    for layer in range(depth):
        need_ctx = layer < depth - 1
        mod = mod_all[layer].reshape(n_rows, 6, d)
        pad = jnp.zeros((bsz, 2, d), F32)
        mod_lat = jnp.concatenate([mod[:bsz], pad], axis=1)
        mod_ctx = jnp.broadcast_to(jnp.concatenate([mod[bsz], pad[0]], axis=0), (bsz, 8, d))
        modv = jnp.stack([mod_ctx, mod_lat], axis=1)

        gq = jnp.tile(g_q_b[layer] * (HEAD_DIM ** -0.5 * LOG2E), 2).reshape(1, LANES)
        gk = jnp.tile(g_k_b[layer], 2).reshape(1, LANES)
        pa, pb, pc, pf = _in_proj(x_ctx, x_lat, lat_off, u, modv, g_norm1[layer].reshape(1, d),
                                  _proj_weight(w_in[layer]), cos, sin, gq, gk, n_ctx)

        ya = _attn_a(pa, sink_a[layer], n_ctx, need_ctx)
        yb = _attn_b(pb, n_ctx, need_ctx)

        gbias = _gate_lanes([b_igate[layer, 0], b_fgate[layer, 0], b_igate[layer, 1], b_fgate[layer, 1],
                             dt_bias[layer, 0], dt_bias[layer, 1]])
        alog = jnp.broadcast_to(jnp.pad(a_log[layer], ((0, 0), (0, 8 - N_HEADS)))[:, :, None], (2, 8, LANES))
        dsk = jnp.repeat(d_skip[layer], HEAD_DIM).reshape(1, GROUP_WIDTH)
        hf, hb, yf, ybw = _scans(pc, pf, gbias, alog, conv_w[layer], conv_b[layer].reshape(1, -1), dsk, n_ctx)

        xc = _out_ffn(x_ctx, x_lat, lat_off, ya, yb, hf, hb, yf, ybw, pf, g_mlstm[layer].reshape(1, GROUP_WIDTH),
                      g_ssm[layer].reshape(1, GROUP_WIDTH), modv, g_norm2[layer].reshape(1, d),
                      _out_weight(w_out[layer]), w_ff1[layer].astype(MXU_DTYPE), w_ff2[layer].astype(MXU_DTYPE),
                      g_final.reshape(1, d), n_ctx, final=not need_ctx)
        x_ctx, x_lat, lat_off = xc, xc, n_ctx // ROW_TILE
    return xc
```
